```python
import jax, jax.numpy as jnp
from jax import lax
import numpy as np

D_MODEL = 2048
BATCH = 2
SEQ = 4096
DEPTH = 4

GRID_W = 64
CTX_LEN = 256
EPS = 1e-6

D_MIX = D_MODEL
GLA_HEADS = 4
GLA_DK = 128
GLA_DV = 128
GLA_W = GLA_HEADS * GLA_DV
GLA_GATE_RANK = 16
GLA_GATE_NORM = 16.0
GLA_CHUNK = 64
CONF_W = 512
CONF_KERNEL = 31
SC_W = 512
SC_KERNEL = 3
MLA_HEADS = 4
MLA_NOPE = 128
MLA_ROPE = 64
MLA_V = 128
MLA_W = MLA_HEADS * MLA_V
MLA_Q_RANK = 384
MLA_KV_RANK = 128
Q_BLOCK = 128
AXIS_DIM = MLA_ROPE // 2
ROPE_FREQS = AXIS_DIM // 2
ROPE_BASE = 10000.0
D_FF = -(-8 * D_MODEL // (3 * 256)) * 256

IN_SIZES = (GLA_HEADS * GLA_DK, GLA_HEADS * GLA_DK, GLA_W, GLA_W, GLA_GATE_RANK, GLA_GATE_RANK,
            CONF_W, CONF_W,
            SC_W, SC_W, SC_W,
            MLA_Q_RANK, MLA_KV_RANK, MLA_ROPE)
IN_W = sum(IN_SIZES)
IN_SPLITS = tuple(int(s) for s in np.cumsum(IN_SIZES)[:-1])

kernel_name = "hybrid_parallel_heads_dit_trunk"


def rms_norm(x, g):
    xf = x.astype(jnp.float32)
    y = xf * lax.rsqrt(jnp.mean(xf * xf, axis=-1, keepdims=True) + EPS)
    return (y * g.astype(jnp.float32)).astype(x.dtype)


def layer_norm(x, g, b):
    xf = x.astype(jnp.float32)
    mu = jnp.mean(xf, axis=-1, keepdims=True)
    xc = xf - mu
    y = xc * lax.rsqrt(jnp.mean(xc * xc, axis=-1, keepdims=True) + EPS)
    return (y * g.astype(jnp.float32) + b.astype(jnp.float32)).astype(x.dtype)


def depthwise_conv(u, w):
    return lax.conv_general_dilated(u, w[:, None, :].astype(u.dtype), window_strides=(1,), padding='SAME',
                                    dimension_numbers=('NWC', 'WIO', 'NWC'),
                                    feature_group_count=u.shape[-1])


def to_heads(t, d):
    return jnp.swapaxes(t.reshape(t.shape[0], t.shape[1], -1, d), 1, 2)


def rope_tables(L):
    rows = L // GRID_W
    row = jnp.repeat(jnp.arange(rows, dtype=jnp.float32), GRID_W)
    col = jnp.tile(jnp.arange(GRID_W, dtype=jnp.float32), rows)
    inv = ROPE_BASE ** (-jnp.arange(ROPE_FREQS, dtype=jnp.float32) * 2.0 / AXIS_DIM)
    ang = jnp.stack([row[:, None] * inv, col[:, None] * inv], axis=1)
    return jnp.cos(ang), jnp.sin(ang)


def apply_rope(t, cos, sin):
    tr = t.reshape(t.shape[:-1] + (2, 2, ROPE_FREQS)).astype(jnp.float32)
    t1, t2 = tr[..., 0, :], tr[..., 1, :]
    out = jnp.stack([t1 * cos - t2 * sin, t2 * cos + t1 * sin], axis=-2)
    return out.reshape(t.shape).astype(t.dtype)


def gla_chunked(q, k, v, logd, s0):
    q, k, v, logd = (t.astype(jnp.float32) for t in (q, k, v, logd))
    b_, h_, L, _ = q.shape
    n = L // GLA_CHUNK

    def chunks(t):
        return jnp.moveaxis(t.reshape(b_, h_, n, GLA_CHUNK, t.shape[-1]), 2, 0)

    idx = jnp.arange(GLA_CHUNK)
    lower = (idx[:, None] >= idx[None, :])[..., None]

    def step(s, inp):
        qi, ki, vi, gi = inp
        bcum = jnp.cumsum(gi, axis=2)
        o_inter = jnp.einsum('bhcd,bhde->bhce', qi * jnp.exp(bcum), s)
        diff = bcum[:, :, :, None, :] - bcum[:, :, None, :, :]
        decay = jnp.where(lower, jnp.exp(jnp.minimum(diff, 0.0)), 0.0)
        scores = jnp.einsum('bhid,bhjd,bhijd->bhij', qi, ki, decay)
        o_intra = jnp.einsum('bhij,bhje->bhie', scores, vi)
        blast = bcum[:, :, -1:, :]
        s_new = jnp.exp(blast[:, :, 0, :])[..., None] * s + jnp.einsum(
            'bhcd,bhce->bhde', ki * jnp.exp(blast - bcum), vi)
        return s_new, o_inter + o_intra

    s_fin, oc = lax.scan(step, s0.astype(jnp.float32), (chunks(q), chunks(k), chunks(v), chunks(logd)))
    return jnp.moveaxis(oc, 0, 2).reshape(b_, h_, L, -1), s_fin


def gla_mixer(zc, zl, fg_up, fg_b, onorm_g, need_ctx):
    def prep(z):
        q, k, v, g, lr_f, lr_b = z
        q = to_heads(q, GLA_DK) * (GLA_DK ** -0.5)
        lds = [to_heads(jax.nn.log_sigmoid((lr @ fg_up[d] + fg_b[d]).astype(jnp.float32)) / GLA_GATE_NORM,
                        GLA_DK) for d, lr in enumerate((lr_f, lr_b))]
        return q, to_heads(k, GLA_DK), to_heads(v, GLA_DV), g, lds

    qc, kc, vc, gc, ldc = prep(zc)
    ql, kl, vl, gl, ldl = prep(zl)
    flip = lambda t: jnp.flip(t, axis=2)
    zero = jnp.zeros(qc.shape[:2] + (GLA_DK, GLA_DV), jnp.float32)
    oc_f, s_cf = gla_chunked(qc, kc, vc, ldc[0], zero)
    ol_f, _ = gla_chunked(ql, kl, vl, ldl[0], s_cf)
    oc_b, s_cb = gla_chunked(flip(qc), flip(kc), flip(vc), flip(ldc[1]), zero)
    ol_b, _ = gla_chunked(flip(ql), flip(kl), flip(vl), flip(ldl[1]), s_cb)

    def finish(o, g):
        o = rms_norm(o, onorm_g)
        o = jnp.swapaxes(o, 1, 2).reshape(g.shape[0], g.shape[1], GLA_W)
        return (o * jax.nn.silu(g.astype(jnp.float32))).astype(g.dtype)

    out_l = finish(ol_f + flip(ol_b), gl)
    out_c = finish(oc_f + flip(oc_b), gc) if need_ctx else None
    return out_c, out_l


def conformer_conv(a, gate, dw, dw_b, ln_g, ln_b):
    u = a * jax.nn.sigmoid(gate)
    u = depthwise_conv(u, dw) + dw_b
    return jax.nn.silu(layer_norm(u, ln_g, ln_b))


def short_conv(bg, cg, h, dw):
    return bg * depthwise_conv(cg * h, dw)


def block_attention(qn, qr, kn, kr, v):
    b_, lq, h_, _ = qn.shape
    nb = lq // Q_BLOCK
    scale = (MLA_NOPE + MLA_ROPE) ** -0.5

    def blk(args):
        qn_b, qr_b = args
        s = jnp.einsum('bqhd,bkhd->bhqk', qn_b, kn) + jnp.einsum('bqhr,bkr->bhqk', qr_b, kr)
        p = jax.nn.softmax(s.astype(jnp.float32) * scale, axis=-1).astype(v.dtype)
        return jnp.einsum('bhqk,bkhd->bqhd', p, v)

    split = lambda t: jnp.moveaxis(t.reshape(b_, nb, Q_BLOCK, h_, t.shape[-1]), 1, 0)
    out = lax.map(blk, (split(qn), split(qr)))
    return jnp.moveaxis(out, 0, 1).reshape(b_, lq, h_ * MLA_V)


def mla_mixer(zc, zl, q_norm_g, kv_norm_g, w_uq, w_ukv, cos, sin, need_ctx):
    def project(z):
        cq, ckv, kr = z
        b_, n = cq.shape[:2]
        q = (rms_norm(cq, q_norm_g) @ w_uq).reshape(b_, n, MLA_HEADS, MLA_NOPE + MLA_ROPE)
        kv = (rms_norm(ckv, kv_norm_g) @ w_ukv).reshape(b_, n, MLA_HEADS, MLA_NOPE + MLA_V)
        return q[..., :MLA_NOPE], q[..., MLA_NOPE:], kv[..., :MLA_NOPE], kr, kv[..., MLA_NOPE:]

    qn_c, qr_c, kn_c, kr_c, v_c = project(zc)
    qn_l, qr_l, kn_l, kr_l, v_l = project(zl)
    qr_l = apply_rope(qr_l, cos[:, None], sin[:, None])
    kr_l = apply_rope(kr_l, cos, sin)
    out_l = block_attention(qn_l, qr_l, jnp.concatenate([kn_c, kn_l], axis=1),
                            jnp.concatenate([kr_c, kr_l], axis=1), jnp.concatenate([v_c, v_l], axis=1))
    out_c = block_attention(qn_c, qr_c, kn_c, kr_c, v_c) if need_ctx else None
    return out_c, out_l


def swiglu(h, w1, w3, w2):
    return (jax.nn.silu(h @ w1) * (h @ w3)) @ w2


def trunk_layer(x, ctx, mod_l, mod_c, p, cos, sin, need_ctx):
    sh1, sc1, g1, sh2, sc2, g2 = jnp.split(mod_l[:, None, :], 6, axis=-1)
    csh1, csc1, cg1, csh2, csc2, cg2 = jnp.split(mod_c, 6, axis=-1)
    n_ctx = ctx.shape[1]
    hl = rms_norm(x, p['norm1_g']) * (1 + sc1) + sh1
    hc = rms_norm(ctx, p['norm1_g']) * (1 + csc1) + csh1
    z = jnp.concatenate([hc, hl], axis=1) @ p['w_in']
    zc = jnp.split(z[:, :n_ctx], IN_SPLITS, axis=-1)
    zl = jnp.split(z[:, n_ctx:], IN_SPLITS, axis=-1)

    gla_c, gla_l = gla_mixer(zc[0:6], zl[0:6], p['gla_fg_up'], p['gla_fg_b'], p['gla_onorm_g'], need_ctx)
    conf = lambda zz: conformer_conv(zz[6], zz[7], p['conf_dw'], p['conf_dw_b'], p['conf_ln_g'], p['conf_ln_b'])
    sconv = lambda zz: short_conv(zz[8], zz[9], zz[10], p['sc_dw'])
    mla_c, mla_l = mla_mixer(zc[11:14], zl[11:14], p['mla_q_norm_g'], p['mla_kv_norm_g'],
                             p['mla_w_uq'], p['mla_w_ukv'], cos, sin, need_ctx)

    o_l = jnp.concatenate([gla_l, conf(zl), sconv(zl), mla_l], axis=-1) @ p['w_out']
    x = x + g1 * o_l
    x = x + g2 * swiglu(rms_norm(x, p['norm2_g']) * (1 + sc2) + sh2, p['ffn_w1'], p['ffn_w3'], p['ffn_w2'])
    if need_ctx:
        o_c = jnp.concatenate([gla_c, conf(zc), sconv(zc), mla_c], axis=-1) @ p['w_out']
        ctx = ctx + cg1 * o_c
        ctx = ctx + cg2 * swiglu(rms_norm(ctx, p['norm2_g']) * (1 + csc2) + csh2,
                                 p['ffn_w1'], p['ffn_w3'], p['ffn_w2'])
    return x, ctx


def setup_inputs(seed: int = 0) -> dict:
    key = jax.random.key(seed)
    ks = iter(jax.random.split(key, 32))
    D = D_MODEL
    nrm = lambda shape, scale: jax.random.normal(next(ks), shape, jnp.float32) * scale
    gain = lambda shape: 1.0 + nrm(shape, 0.05)
    return {
        "x": nrm((BATCH, SEQ, D), 1.0),
        "c": nrm((BATCH, D), 1.0),
        "ctx": nrm((BATCH, CTX_LEN, D), 1.0),
        "c_ctx": nrm((D,), 1.0),
        "norm1_g": gain((DEPTH, D)),
        "w_mod": nrm((DEPTH, D, 6 * D), 0.5 * D ** -0.5),
        "b_mod": nrm((DEPTH, 6 * D), 0.02),
        "w_in": nrm((DEPTH, D, IN_W), D ** -0.5),
        "gla_fg_up": nrm((DEPTH, 2, GLA_GATE_RANK, GLA_HEADS * GLA_DK), GLA_GATE_RANK ** -0.5),
        "gla_fg_b": nrm((DEPTH, 2, GLA_HEADS * GLA_DK), 0.1),
        "gla_onorm_g": gain((DEPTH, GLA_DV)),
        "conf_dw": nrm((DEPTH, CONF_KERNEL, CONF_W), CONF_KERNEL ** -0.5),
        "conf_dw_b": nrm((DEPTH, CONF_W), 0.02),
        "conf_ln_g": gain((DEPTH, CONF_W)),
        "conf_ln_b": nrm((DEPTH, CONF_W), 0.02),
        "sc_dw": nrm((DEPTH, SC_KERNEL, SC_W), SC_KERNEL ** -0.5),
        "mla_q_norm_g": gain((DEPTH, MLA_Q_RANK)),
        "mla_kv_norm_g": gain((DEPTH, MLA_KV_RANK)),
        "mla_w_uq": nrm((DEPTH, MLA_Q_RANK, MLA_HEADS * (MLA_NOPE + MLA_ROPE)), MLA_Q_RANK ** -0.5),
        "mla_w_ukv": nrm((DEPTH, MLA_KV_RANK, MLA_HEADS * (MLA_NOPE + MLA_V)), MLA_KV_RANK ** -0.5),
        "w_out": nrm((DEPTH, D_MIX, D), D_MIX ** -0.5),
        "norm2_g": gain((DEPTH, D)),
        "ffn_w1": nrm((DEPTH, D, D_FF), D ** -0.5),
        "ffn_w3": nrm((DEPTH, D, D_FF), D ** -0.5),
        "ffn_w2": nrm((DEPTH, D_FF, D), D_FF ** -0.5),
        "final_norm_g": gain((D,)),
    }


def reference(x, c, ctx, c_ctx, norm1_g, w_mod, b_mod, w_in, gla_fg_up, gla_fg_b, gla_onorm_g,
              conf_dw, conf_dw_b, conf_ln_g, conf_ln_b, sc_dw, mla_q_norm_g, mla_kv_norm_g,
              mla_w_uq, mla_w_ukv, w_out, norm2_g, ffn_w1, ffn_w3, ffn_w2, final_norm_g):
    cos, sin = rope_tables(x.shape[1])
    c_act = jax.nn.silu(c)
    cc_act = jax.nn.silu(c_ctx)
    for i in range(DEPTH):
        mod_l = c_act @ w_mod[i] + b_mod[i]
        mod_c = cc_act @ w_mod[i] + b_mod[i]
        p = dict(norm1_g=norm1_g[i], w_in=w_in[i], gla_fg_up=gla_fg_up[i], gla_fg_b=gla_fg_b[i],
                 gla_onorm_g=gla_onorm_g[i], conf_dw=conf_dw[i], conf_dw_b=conf_dw_b[i],
                 conf_ln_g=conf_ln_g[i], conf_ln_b=conf_ln_b[i], sc_dw=sc_dw[i],
                 mla_q_norm_g=mla_q_norm_g[i], mla_kv_norm_g=mla_kv_norm_g[i], mla_w_uq=mla_w_uq[i],
                 mla_w_ukv=mla_w_ukv[i], w_out=w_out[i], norm2_g=norm2_g[i], ffn_w1=ffn_w1[i],
                 ffn_w3=ffn_w3[i], ffn_w2=ffn_w2[i])
        x, ctx = trunk_layer(x, ctx, mod_l, mod_c, p, cos, sin, need_ctx=(i < DEPTH - 1))
    return rms_norm(x, final_norm_g)
```

```python
import functools

import jax
import jax.numpy as jnp
import numpy as np
from jax import lax
from jax.experimental import pallas as pl
from jax.experimental.pallas import tpu as pltpu

F32 = jnp.float32
BF16 = jnp.bfloat16

EPS = 1e-6
GRID_W = 64
HEADS = 4
HEAD_W = 128
GLA_GATE_RANK = 16
GLA_GATE_NORM = 16.0
GLA_CHUNK = 64
CONF_KERNEL = 31
SC_KERNEL = 3
MIX_W = 512
MLA_ROPE = 64
MLA_Q_RANK = 384
MLA_KV_RANK = 128
ROPE_FREQS = 16
ROPE_BASE = 10000.0
Q_BLOCK = 256
SEQ_BLOCK = 256
HALO = 16
ZC_W = 5 * MIX_W
ZG_W = 4 * MIX_W
MOD_ROWS = 8
ZS_W = 640
VMEM_LIMIT_V7X = 56 * 1024 * 1024


def _dot(a, b):
    return jnp.dot(a, b, preferred_element_type=F32)


def _dot_nt(a, b):
    return lax.dot_general(a, b, (((1,), (1,)), ((), ())), preferred_element_type=F32)


def _dot_tn(a, b):
    return lax.dot_general(a, b, (((0,), (0,)), ((), ())), preferred_element_type=F32)


def _dot_hi(a, b):
    return jnp.dot(a, b, preferred_element_type=F32, precision=lax.Precision.HIGHEST)


def _sigmoid(x):
    return 1.0 / (1.0 + jnp.exp(-x))


def _silu(x):
    return x * _sigmoid(x)


def _rms(x, g):
    return x * lax.rsqrt(jnp.mean(x * x, axis=-1, keepdims=True) + EPS) * g


def _params(sem):
    return pltpu.CompilerParams(dimension_semantics=sem, vmem_limit_bytes=VMEM_LIMIT_V7X)


def _mod_kernel(c_ref, w_ref, b_ref, o_ref):
    tn = w_ref.shape[2]
    rows = []
    nrow = c_ref.shape[0]
    for m in range(nrow):
        am = _silu(c_ref[m])
        cols = [jnp.sum(w_ref[0, :, cb * 128:(cb + 1) * 128] * am, axis=0, keepdims=True)
                for cb in range(tn // 128)]
        rows.append(jnp.concatenate(cols, axis=1) + b_ref[0])
    rows.append(jnp.zeros((MOD_ROWS - nrow, tn), F32))
    o_ref[0] = jnp.concatenate(rows, axis=0)


def _modulation(cb, w_mod, b_mod):
    depth, d, n6 = w_mod.shape
    tn = 1024
    return pl.pallas_call(
        _mod_kernel,
        grid=(depth, n6 // tn),
        in_specs=[pl.BlockSpec(cb.shape, lambda i, j: (0, 0, 0)),
                  pl.BlockSpec((1, d, tn), lambda i, j: (i, 0, j)),
                  pl.BlockSpec((1, 1, tn), lambda i, j: (i, 0, j))],
        out_specs=pl.BlockSpec((1, MOD_ROWS, tn), lambda i, j: (i, 0, j)),
        out_shape=jax.ShapeDtypeStruct((depth, MOD_ROWS, n6), F32),
        compiler_params=_params(("arbitrary", "arbitrary")),
        name="modulation",
    )(cb, w_mod, b_mod.reshape(depth, 1, n6))


def _win_kernel(x_ref, ml_ref, mc_ref, g_ref, w_ref, ws_ref, z_ref, zs_ref, h_sc, *, n_lat, rc):
    t = pl.program_id(1)
    j = pl.program_id(2)
    tm = x_ref.shape[1]

    @pl.when(j == 0)
    def _():
        for r in range(tm // rc):
            rows = slice(r * rc, (r + 1) * rc)
            y = _rms(x_ref[0, rows, :], g_ref[...])
            rowid = t * tm + r * rc + lax.broadcasted_iota(jnp.int32, (rc, 1), 0)
            is_ctx = rowid >= n_lat
            sh = jnp.where(is_ctx, mc_ref[0, 0:1, :], ml_ref[0, 0:1, :])
            sc = jnp.where(is_ctx, mc_ref[0, 1:2, :], ml_ref[0, 1:2, :])
            h = (y * (1.0 + sc) + sh).astype(BF16)
            h_sc[rows, :] = h
            zs_ref[0, rows, :] = _dot(h, ws_ref[...])
            z_ref[0, rows, :] = _dot(h, w_ref[...]).astype(BF16)

    @pl.when(j != 0)
    def _():
        for r in range(tm // rc):
            rows = slice(r * rc, (r + 1) * rc)
            z_ref[0, rows, :] = _dot(h_sc[rows, :], w_ref[...]).astype(BF16)


def _win(xx, modl, g, w_main, w_small, n_lat):
    b, nt, d = xx.shape
    tm = nt // 4
    tn = 512
    nw = w_main.shape[1]
    kern = functools.partial(_win_kernel, n_lat=n_lat, rc=tm // 4)
    return pl.pallas_call(
        kern,
        grid=(b, nt // tm, nw // tn),
        in_specs=[pl.BlockSpec((1, tm, d), lambda bi, t, j: (bi, t, 0)),
                  pl.BlockSpec((1, 6, d), lambda bi, t, j: (bi, 0, 0)),
                  pl.BlockSpec((1, 6, d), lambda bi, t, j: (b, 0, 0)),
                  pl.BlockSpec((1, d), lambda bi, t, j: (0, 0)),
                  pl.BlockSpec((d, tn), lambda bi, t, j: (0, j)),
                  pl.BlockSpec((d, ZS_W), lambda bi, t, j: (0, 0))],
        out_specs=[pl.BlockSpec((1, tm, tn), lambda bi, t, j: (bi, t, j)),
                   pl.BlockSpec((1, tm, ZS_W), lambda bi, t, j: (bi, t, 0))],
        out_shape=[jax.ShapeDtypeStruct((b, nt, nw), BF16),
                   jax.ShapeDtypeStruct((b, nt, ZS_W), F32)],
        scratch_shapes=[pltpu.VMEM((tm, d), BF16)],
        compiler_params=_params(("arbitrary", "arbitrary", "arbitrary")),
        name="norm1_w_in",
    )(xx, modl, modl, g, w_main, w_small)


def _gla_kernel(q_ref, k_ref, v_ref, g_ref, zs_ref, fw_ref, fb_ref, on_ref, o_ref,
                bc_sc, qin_sc, kv_sc, er_sc, oacc_sc, *, n_lat):
    n_tot = q_ref.shape[1]
    blk = SEQ_BLOCK
    cpb = blk // GLA_CHUNK
    nblk = n_tot // blk
    nchunk = n_tot // GLA_CHUNK
    lat_chunks = n_lat // GLA_CHUNK
    ctx_chunks = nchunk - lat_chunks
    qscale = HEAD_W ** -0.5

    ri = lax.broadcasted_iota(jnp.int32, (blk, blk), 0)
    ci = lax.broadcasted_iota(jnp.int32, (blk, blk), 1)
    same64 = (ri >> 6) == (ci >> 6)
    same32 = (ri >> 5) == (ci >> 5)
    row = lax.broadcasted_iota(jnp.int32, (blk, 1), 0)
    half0 = (row & 63) < 32
    quart0 = (row & 31) < 16

    tri = (jnp.where(same64 & (ri >= ci), 1.0, 0.0).astype(F32),
           jnp.where(same64 & (ri <= ci), 1.0, 0.0).astype(F32))

    def prologue(i, carry):
        rows = pl.ds(pl.multiple_of(i * blk, blk), blk)
        zsb = zs_ref[0, rows, :]
        for d in range(2):
            xg = _dot_hi(zsb, fw_ref[d, 0]) + fb_ref[d, 0]
            logd = (jnp.minimum(xg, 0.0) - jnp.log1p(jnp.exp(-jnp.abs(xg)))) * (1.0 / GLA_GATE_NORM)
            bc_sc[d, rows, :] = _dot_hi(tri[d], logd)
        return carry

    lax.fori_loop(0, nblk, prologue, 0)

    def run_direction(d):
        rev = d == 1

        def intra(i, carry):
            rows = pl.ds(pl.multiple_of(i * blk, blk), blk)
            b = bc_sc[d, rows, :]
            q = q_ref[0, rows, :].astype(F32) * qscale
            k = k_ref[0, rows, :].astype(F32)
            v = v_ref[0, rows, :]
            b4 = b.reshape(cpb, GLA_CHUNK, HEAD_W)
            blast = b4[:, 0:1, :] if rev else b4[:, GLA_CHUNK - 1:GLA_CHUNK, :]
            kst = (k.reshape(cpb, GLA_CHUNK, HEAD_W) * jnp.exp(blast - b4)).astype(BF16)
            v4 = v.reshape(cpb, GLA_CHUNK, HEAD_W)
            ebl = jnp.exp(blast)
            for c in range(cpb):
                kv_sc[i * cpb + c] = _dot_tn(v4[c], kst[c])
                er_sc[i * cpb + c] = jnp.broadcast_to(ebl[c], (8, HEAD_W))
            qin_sc[rows, :] = (q * jnp.exp(b)).astype(BF16)

            r1 = b4[:, 32:33, :] if rev else b4[:, 31:32, :]
            r1 = jnp.broadcast_to(r1, b4.shape).reshape(blk, HEAD_W)
            qsel, ksel = (half0, ~half0) if rev else (~half0, half0)
            qa = q * jnp.where(qsel, jnp.exp(jnp.minimum(b - r1, 0.0)), 0.0)
            ka = k * jnp.where(ksel, jnp.exp(jnp.minimum(r1 - b, 0.0)), 0.0)
            s1 = _dot_nt(qa.astype(BF16), ka.astype(BF16))
            b8 = b.reshape(blk // 32, 32, HEAD_W)
            r2 = b8[:, 16:17, :] if rev else b8[:, 15:16, :]
            r2 = jnp.broadcast_to(r2, b8.shape).reshape(blk, HEAD_W)
            qsel, ksel = (quart0, ~quart0) if rev else (~quart0, quart0)
            qb = q * jnp.where(qsel, jnp.exp(jnp.minimum(b - r2, 0.0)), 0.0)
            kb = k * jnp.where(ksel, jnp.exp(jnp.minimum(r2 - b, 0.0)), 0.0)
            s2 = _dot_nt(qb.astype(BF16), kb.astype(BF16))
            scores = jnp.where(same64, s1, 0.0) + jnp.where(same32, s2, 0.0)
            q16 = q.reshape(blk // 16, 16, HEAD_W)
            k16 = k.reshape(blk // 16, 16, HEAD_W)
            b16 = b.reshape(blk // 16, 16, HEAD_W)
            for jj in range(16):
                e = jnp.exp(jnp.minimum(b16 - b16[:, jj:jj + 1, :], 0.0))
                tj = (q16 * k16[:, jj:jj + 1, :] * e).reshape(blk, HEAD_W)
                sj = jnp.sum(tj, axis=-1, keepdims=True)
                keep = (ci == ((ri >> 4) << 4) + jj)
                keep = keep & (((ri & 15) <= jj) if rev else ((ri & 15) >= jj))
                scores = scores + jnp.where(keep, sj, 0.0)
            o_blk = _dot(scores.astype(BF16), v)
            if rev:
                oacc_sc[rows, :] += o_blk
            else:
                oacc_sc[rows, :] = o_blk
            return carry

        lax.fori_loop(0, nblk, intra, 0)

        def inter(n, st):
            if rev:
                cid = jnp.where(n < ctx_chunks, lat_chunks + (ctx_chunks - 1 - n),
                                lat_chunks - 1 - (n - ctx_chunks))
            else:
                cid = jnp.where(n < ctx_chunks, lat_chunks + n, n - ctx_chunks)
            rows = pl.ds(pl.multiple_of(cid * GLA_CHUNK, GLA_CHUNK), GLA_CHUNK)
            oacc_sc[rows, :] += _dot_nt(qin_sc[rows, :], st.astype(BF16))
            return st * er_sc[cid][0:1, :] + kv_sc[cid]

        lax.fori_loop(0, nchunk, inter, jnp.zeros((HEAD_W, HEAD_W), F32))

    run_direction(0)
    run_direction(1)

    def finish(i, carry):
        rows = pl.ds(pl.multiple_of(i * blk, blk), blk)
        y = _rms(oacc_sc[rows, :], on_ref[...])
        o_ref[0, rows, :] = (y * _silu(g_ref[0, rows, :].astype(F32))).astype(BF16)
        return carry

    lax.fori_loop(0, nblk, finish, 0)


def _gla(z, zs, fw, fb, on, n_lat):
    b, nt, _ = z.shape
    cb0 = ZC_W // HEAD_W
    seq = lambda off: pl.BlockSpec((1, nt, HEAD_W), lambda bi, h: (bi, 0, cb0 + off * HEADS + h))
    kern = functools.partial(_gla_kernel, n_lat=n_lat)
    return pl.pallas_call(
        kern,
        grid=(b, HEADS),
        in_specs=[seq(0), seq(1), seq(2), seq(3),
                  pl.BlockSpec((1, nt, HEAD_W), lambda bi, h: (bi, 0, ZS_W // HEAD_W - 1)),
                  pl.BlockSpec((2, 1, HEAD_W, HEAD_W), lambda bi, h: (0, h, 0, 0)),
                  pl.BlockSpec((2, 1, 1, HEAD_W), lambda bi, h: (0, h, 0, 0)),
                  pl.BlockSpec((1, HEAD_W), lambda bi, h: (0, 0))],
        out_specs=pl.BlockSpec((1, nt, HEAD_W), lambda bi, h: (bi, 0, h)),
        out_shape=jax.ShapeDtypeStruct((b, nt, MIX_W), BF16),
        scratch_shapes=[pltpu.VMEM((2, nt, HEAD_W), F32),
                        pltpu.VMEM((nt, HEAD_W), BF16),
                        pltpu.VMEM((nt // GLA_CHUNK, HEAD_W, HEAD_W), F32),
                        pltpu.VMEM((nt // GLA_CHUNK, 8, HEAD_W), F32),
                        pltpu.VMEM((nt, HEAD_W), F32)],
        compiler_params=_params(("arbitrary", "arbitrary")),
        name="gla_mixer",
    )(z, z, z, z, zs, fw, fb, on)


def _conv_kernel(zc_ref, zp_ref, zn_ref, dw_ref, dwb_ref, lng_ref, lnb_ref, scw_ref, o_ref,
                 u_sc, m_sc, *, lat_tiles, n_tiles):
    t = pl.program_id(1)
    tl = zc_ref.shape[1]
    w = MIX_W
    first = (t == 0) | (t == lat_tiles)
    last = (t == lat_tiles - 1) | (t == n_tiles - 1)

    def glu(ref, rows):
        return ref[0, rows, 0:w].astype(F32) * _sigmoid(ref[0, rows, w:2 * w].astype(F32))

    def gated(ref, rows):
        return ref[0, rows, 3 * w:4 * w].astype(F32) * ref[0, rows, 4 * w:5 * w].astype(F32)

    halo = slice(0, HALO)
    u_sc[0:HALO, :] = jnp.where(first, 0.0, glu(zp_ref, halo))
    m_sc[0:HALO, :] = jnp.where(first, 0.0, gated(zp_ref, halo))
    u_sc[HALO + tl:, :] = jnp.where(last, 0.0, glu(zn_ref, halo))
    m_sc[HALO + tl:, :] = jnp.where(last, 0.0, gated(zn_ref, halo))
    rc = 32
    for r in range(tl // rc):
        rows = slice(r * rc, (r + 1) * rc)
        u_sc[HALO + r * rc:HALO + (r + 1) * rc, :] = glu(zc_ref, rows)
        m_sc[HALO + r * rc:HALO + (r + 1) * rc, :] = gated(zc_ref, rows)

    pad = CONF_KERNEL // 2
    for r in range(tl // rc):
        r0 = r * rc
        acc = jnp.zeros((rc, w), F32)
        for kk in range(CONF_KERNEL):
            s = HALO + r0 + kk - pad
            acc = acc + dw_ref[kk:kk + 1, :] * u_sc[s:s + rc, :]
        acc = acc + dwb_ref[...]
        mu = jnp.mean(acc, axis=-1, keepdims=True)
        xc = acc - mu
        y = xc * lax.rsqrt(jnp.mean(xc * xc, axis=-1, keepdims=True) + EPS) * lng_ref[...] + lnb_ref[...]
        o_ref[0, r0:r0 + rc, 0:w] = _silu(y).astype(BF16)
        s = HALO + r0
        cv = (scw_ref[0:1, :] * m_sc[s - 1:s - 1 + rc, :] + scw_ref[1:2, :] * m_sc[s:s + rc, :]
              + scw_ref[2:3, :] * m_sc[s + 1:s + 1 + rc, :])
        o_ref[0, r0:r0 + rc, w:2 * w] = (zc_ref[0, r0:r0 + rc, 2 * w:3 * w].astype(F32) * cv).astype(BF16)


def _conv_mixers(z, dw, dwb, lng, lnb, scw, n_lat):
    b, nt, _ = z.shape
    tl = SEQ_BLOCK
    n_tiles = nt // tl
    hb = tl // HALO
    nhb = nt // HALO
    kern = functools.partial(_conv_kernel, lat_tiles=n_lat // tl, n_tiles=n_tiles)
    vec = lambda n: pl.BlockSpec((n, MIX_W), lambda bi, t: (0, 0))
    return pl.pallas_call(
        kern,
        grid=(b, n_tiles),
        in_specs=[pl.BlockSpec((1, tl, ZC_W), lambda bi, t: (bi, t, 0)),
                  pl.BlockSpec((1, HALO, ZC_W), lambda bi, t: (bi, jnp.maximum(t * hb - 1, 0), 0)),
                  pl.BlockSpec((1, HALO, ZC_W), lambda bi, t: (bi, jnp.minimum((t + 1) * hb, nhb - 1), 0)),
                  vec(CONF_KERNEL), vec(1), vec(1), vec(1), vec(SC_KERNEL)],
        out_specs=pl.BlockSpec((1, tl, 2 * MIX_W), lambda bi, t: (bi, t, 0)),
        out_shape=jax.ShapeDtypeStruct((b, nt, 2 * MIX_W), BF16),
        scratch_shapes=[pltpu.VMEM((tl + 2 * HALO, MIX_W), F32),
                        pltpu.VMEM((tl + 2 * HALO, MIX_W), F32)],
        compiler_params=_params(("arbitrary", "arbitrary")),
        name="conv_mixers",
    )(z, z, z, dw, dwb, lng, lnb, scw)


def _mla_proj_kernel(zs_ref, qg_ref, kvg_ref, wq_ref, wkv_ref, cos_ref, sin_ref,
                     q_ref, k_ref, v_ref, *, rc):
    tm = zs_ref.shape[1]
    scale = (HEAD_W + MLA_ROPE) ** -0.5
    lane = lax.broadcasted_iota(jnp.int32, (1, HEAD_W), 1)
    lower = (lane & 31) < 16

    for r in range(tm // rc):
        rows = slice(r * rc, (r + 1) * rc)
        cq = zs_ref[0, rows, 0:MLA_Q_RANK]
        ckv = zs_ref[0, rows, MLA_Q_RANK:MLA_Q_RANK + MLA_KV_RANK]
        krb = zs_ref[0, rows, MLA_Q_RANK + MLA_KV_RANK:ZS_W]
        q = _dot(_rms(cq, qg_ref[...]).astype(BF16), wq_ref[...])
        kv = _dot(_rms(ckv, kvg_ref[...]).astype(BF16), wkv_ref[...])
        cs = cos_ref[rows, :]
        sn = sin_ref[rows, :]

        def rope(tt):
            swapped = jnp.where(lower, pltpu.roll(tt, HEAD_W - 16, 1), pltpu.roll(tt, 16, 1))
            return tt * cs + swapped * sn

        krr = rope(krb).astype(BF16)
        for h in range(HEADS):
            c0 = 2 * HEAD_W * h
            q_ref[0, rows, c0:c0 + HEAD_W] = (q[:, c0:c0 + HEAD_W] * scale).astype(BF16)
            q_ref[0, rows, c0 + HEAD_W:c0 + 2 * HEAD_W] = (
                rope(q[:, c0 + HEAD_W:c0 + 2 * HEAD_W]) * scale).astype(BF16)
            k_ref[0, rows, c0:c0 + HEAD_W] = kv[:, c0:c0 + HEAD_W].astype(BF16)
            k_ref[0, rows, c0 + HEAD_W:c0 + 2 * HEAD_W] = krr
            v_ref[0, rows, HEAD_W * h:HEAD_W * (h + 1)] = kv[:, c0 + HEAD_W:c0 + 2 * HEAD_W].astype(BF16)


def _mla_proj(zs, qg, kvg, wq, wkv, cos_t, sin_t):
    b, nt, _ = zs.shape
    tm = nt // 4
    kern = functools.partial(_mla_proj_kernel, rc=tm // 4)
    full = lambda a: pl.BlockSpec(a.shape, lambda bi, t: (0,) * a.ndim)
    kw = 2 * HEAD_W * HEADS
    return pl.pallas_call(
        kern,
        grid=(b, nt // tm),
        in_specs=[pl.BlockSpec((1, tm, ZS_W), lambda bi, t: (bi, t, 0)),
                  full(qg), full(kvg), full(wq), full(wkv),
                  pl.BlockSpec((tm, HEAD_W), lambda bi, t: (t, 0)),
                  pl.BlockSpec((tm, HEAD_W), lambda bi, t: (t, 0))],
        out_specs=[pl.BlockSpec((1, tm, kw), lambda bi, t: (bi, t, 0)),
                   pl.BlockSpec((1, tm, kw), lambda bi, t: (bi, t, 0)),
                   pl.BlockSpec((1, tm, MIX_W), lambda bi, t: (bi, t, 0))],
        out_shape=[jax.ShapeDtypeStruct((b, nt, kw), BF16),
                   jax.ShapeDtypeStruct((b, nt, kw), BF16),
                   jax.ShapeDtypeStruct((b, nt, MIX_W), BF16)],
        compiler_params=_params(("arbitrary", "arbitrary")),
        name="mla_proj",
    )(zs, qg, kvg, wq, wkv, cos_t, sin_t)


def _attn_kernel(q_ref, k_ref, v_ref, o_ref, *, n_lat):
    qi = pl.program_id(2)
    q = q_ref[0]

    def attend(k, v):
        s = _dot_nt(q, k)
        p = jnp.exp(s - jnp.max(s, axis=-1, keepdims=True))
        denom = jnp.sum(p, axis=-1, keepdims=True)
        o_ref[0] = (_dot(p.astype(BF16), v) / denom).astype(BF16)

    @pl.when(qi < n_lat // Q_BLOCK)
    def _():
        attend(k_ref[0], v_ref[0])

    @pl.when(qi >= n_lat // Q_BLOCK)
    def _():
        attend(k_ref[0, n_lat:, :], v_ref[0, n_lat:, :])


def _attention(q, k, v, n_lat, need_ctx):
    b, nt, _ = q.shape
    nq = (nt if need_ctx else n_lat) // Q_BLOCK
    kern = functools.partial(_attn_kernel, n_lat=n_lat)
    return pl.pallas_call(
        kern,
        grid=(b, HEADS, nq),
        in_specs=[pl.BlockSpec((1, Q_BLOCK, 2 * HEAD_W), lambda bi, h, qi: (bi, qi, h)),
                  pl.BlockSpec((1, nt, 2 * HEAD_W), lambda bi, h, qi: (bi, 0, h)),
                  pl.BlockSpec((1, nt, HEAD_W), lambda bi, h, qi: (bi, 0, h))],
        out_specs=pl.BlockSpec((1, Q_BLOCK, HEAD_W), lambda bi, h, qi: (bi, qi, h)),
        out_shape=jax.ShapeDtypeStruct((b, nq * Q_BLOCK, MIX_W), BF16),
        compiler_params=_params(("arbitrary", "arbitrary", "arbitrary")),
        name="mla_attention",
    )(q, k, v)


def _wout_kernel(gla_ref, cs_ref, mla_ref, x_ref, ml_ref, mc_ref, g_ref, w_ref, xo_ref, h_ref,
                 *, n_lat, rc):
    t = pl.program_id(1)
    tm = x_ref.shape[1]
    w = MIX_W
    for r in range(tm // rc):
        rows = slice(r * rc, (r + 1) * rc)
        o = (_dot(gla_ref[0, rows, :], w_ref[0:w, :]) + _dot(cs_ref[0, rows, :], w_ref[w:3 * w, :])
             + _dot(mla_ref[0, rows, :], w_ref[3 * w:4 * w, :]))
        rowid = t * tm + r * rc + lax.broadcasted_iota(jnp.int32, (rc, 1), 0)
        is_ctx = rowid >= n_lat
        g1 = jnp.where(is_ctx, mc_ref[0, 2:3, :], ml_ref[0, 2:3, :])
        sh = jnp.where(is_ctx, mc_ref[0, 3:4, :], ml_ref[0, 3:4, :])
        sc = jnp.where(is_ctx, mc_ref[0, 4:5, :], ml_ref[0, 4:5, :])
        xn = x_ref[0, rows, :] + g1 * o
        xo_ref[0, rows, :] = xn
        h_ref[0, rows, :] = (_rms(xn, g_ref[...]) * (1.0 + sc) + sh).astype(BF16)


def _wout(gla, cs, mla, xx, modl, g, w, n_lat, n_rows):
    b, _, d = xx.shape
    tm = n_rows // 8
    kern = functools.partial(_wout_kernel, n_lat=n_lat, rc=tm // 2)
    tile = lambda wd: pl.BlockSpec((1, tm, wd), lambda bi, t: (bi, t, 0))
    return pl.pallas_call(
        kern,
        grid=(b, n_rows // tm),
        in_specs=[tile(MIX_W), tile(2 * MIX_W), tile(MIX_W), tile(d),
                  pl.BlockSpec((1, 6, d), lambda bi, t: (bi, 0, 0)),
                  pl.BlockSpec((1, 6, d), lambda bi, t: (b, 0, 0)),
                  pl.BlockSpec((1, d), lambda bi, t: (0, 0)),
                  pl.BlockSpec(w.shape, lambda bi, t: (0, 0))],
        out_specs=[tile(d), tile(d)],
        out_shape=[jax.ShapeDtypeStruct((b, n_rows, d), F32),
                   jax.ShapeDtypeStruct((b, n_rows, d), BF16)],
        compiler_params=_params(("arbitrary", "arbitrary")),
        name="w_out_norm2",
    )(gla, cs, mla, xx, modl, modl, g, w)


def _ffn_kernel(x_ref, h_ref, ml_ref, mc_ref, w1_ref, w3_ref, w2_ref, fg_ref, o_ref,
                *, n_lat, rc, final):
    t = pl.program_id(1)
    j = pl.program_id(2)
    tm = x_ref.shape[1]
    for r in range(tm // rc):
        rows = slice(r * rc, (r + 1) * rc)
        hh = h_ref[0, rows, :]
        u = (_silu(_dot(hh, w1_ref[...])) * _dot(hh, w3_ref[...])).astype(BF16)
        y = _dot(u, w2_ref[...])

        @pl.when(j == 0)
        def _():
            o_ref[0, rows, :] = y

        @pl.when(j != 0)
        def _():
            o_ref[0, rows, :] += y

    @pl.when(j == pl.num_programs(2) - 1)
    def _():
        for r in range(tm // rc):
            rows = slice(r * rc, (r + 1) * rc)
            rowid = t * tm + r * rc + lax.broadcasted_iota(jnp.int32, (rc, 1), 0)
            g2 = jnp.where(rowid >= n_lat, mc_ref[0, 5:6, :], ml_ref[0, 5:6, :])
            xn = x_ref[0, rows, :] + g2 * o_ref[0, rows, :]
            o_ref[0, rows, :] = _rms(xn, fg_ref[...]) if final else xn


def _ffn(xn, h2, modl, w1, w3, w2, fg, n_lat, final):
    b, n_rows, d = xn.shape
    tm = n_rows // 8
    tf = 512
    dff = w1.shape[1]
    kern = functools.partial(_ffn_kernel, n_lat=n_lat, rc=tm // 2, final=final)
    return pl.pallas_call(
        kern,
        grid=(b, n_rows // tm, dff // tf),
        in_specs=[pl.BlockSpec((1, tm, d), lambda bi, t, j: (bi, t, 0)),
                  pl.BlockSpec((1, tm, d), lambda bi, t, j: (bi, t, 0)),
                  pl.BlockSpec((1, 6, d), lambda bi, t, j: (bi, 0, 0)),
                  pl.BlockSpec((1, 6, d), lambda bi, t, j: (b, 0, 0)),
                  pl.BlockSpec((d, tf), lambda bi, t, j: (0, j)),
                  pl.BlockSpec((d, tf), lambda bi, t, j: (0, j)),
                  pl.BlockSpec((tf, d), lambda bi, t, j: (j, 0)),
                  pl.BlockSpec((1, d), lambda bi, t, j: (0, 0))],
        out_specs=pl.BlockSpec((1, tm, d), lambda bi, t, j: (bi, t, 0)),
        out_shape=jax.ShapeDtypeStruct((b, n_rows, d), F32),
        compiler_params=_params(("arbitrary", "arbitrary", "arbitrary")),
        name="swiglu_ffn",
    )(xn, h2, modl, modl, w1, w3, w2, fg)


def _rope_tables(n_lat, n_ctx):
    rows = n_lat // GRID_W
    rowp = jnp.repeat(jnp.arange(rows, dtype=F32), GRID_W)
    colp = jnp.tile(jnp.arange(GRID_W, dtype=F32), rows)
    inv = ROPE_BASE ** (-jnp.arange(ROPE_FREQS, dtype=F32) * 2.0 / (2 * ROPE_FREQS))
    ar, ac = rowp[:, None] * inv, colp[:, None] * inv
    zeros = jnp.zeros((n_lat, HEAD_W - MLA_ROPE), F32)
    cos_l = jnp.concatenate([jnp.cos(ar), jnp.cos(ar), jnp.cos(ac), jnp.cos(ac), zeros], axis=1)
    sin_l = jnp.concatenate([-jnp.sin(ar), jnp.sin(ar), -jnp.sin(ac), jnp.sin(ac), zeros], axis=1)
    cos_c = jnp.concatenate([jnp.ones((n_ctx, MLA_ROPE), F32), jnp.zeros((n_ctx, HEAD_W - MLA_ROPE), F32)], axis=1)
    return jnp.concatenate([cos_l, cos_c], axis=0), jnp.concatenate([sin_l, jnp.zeros_like(cos_c)], axis=0)


def kernel(x, c, ctx, c_ctx, norm1_g, w_mod, b_mod, w_in, gla_fg_up, gla_fg_b, gla_onorm_g, conf_dw, conf_dw_b, conf_ln_g, conf_ln_b, sc_dw, mla_q_norm_g, mla_kv_norm_g, mla_w_uq, mla_w_ukv, w_out, norm2_g, ffn_w1, ffn_w3, ffn_w2, final_norm_g):
    bsz, n_lat, d = x.shape
    n_ctx = ctx.shape[1]
    depth = w_in.shape[0]
    w = MIX_W

    xx = jnp.concatenate([x, ctx], axis=1)
    c3 = jnp.concatenate([c, c_ctx[None, :]], axis=0)
    mod = _modulation(jnp.broadcast_to(c3[:, :, None], (bsz + 1, d, 128)), w_mod, b_mod)
    mod = mod.reshape(depth, MOD_ROWS, 6, d)
    cos_t, sin_t = _rope_tables(n_lat, n_ctx)

    o_gla, o_lr, o_conv, o_mla = 0, 4 * w, 4 * w + 2 * GLA_GATE_RANK, 4 * w + 2 * GLA_GATE_RANK + 5 * w
    w_main = jnp.concatenate([w_in[:, :, o_conv:o_mla], w_in[:, :, o_gla:o_lr]], axis=-1).astype(BF16)
    w_small = jnp.concatenate([w_in[:, :, o_mla:], w_in[:, :, o_lr:o_conv],
                               jnp.zeros((depth, d, ZS_W - (w_in.shape[2] - o_mla) - 2 * GLA_GATE_RANK), F32)],
                              axis=-1).astype(BF16)
    lr0 = MLA_ROPE
    fw = jnp.zeros((depth, 2, HEAD_W, HEADS * HEAD_W), F32)
    fw = fw.at[:, 0, lr0:lr0 + GLA_GATE_RANK].set(gla_fg_up[:, 0])
    fw = fw.at[:, 1, lr0 + GLA_GATE_RANK:lr0 + 2 * GLA_GATE_RANK].set(gla_fg_up[:, 1])
    fw = fw.reshape(depth, 2, HEAD_W, HEADS, HEAD_W).transpose(0, 1, 3, 2, 4)
    fb = gla_fg_b.reshape(depth, 2, HEADS, 1, HEAD_W)
    wq = mla_w_uq.reshape(depth, MLA_Q_RANK, HEADS, HEAD_W + MLA_ROPE)
    wq = jnp.pad(wq, ((0, 0), (0, 0), (0, 0), (0, HEAD_W - MLA_ROPE))).reshape(depth, MLA_Q_RANK, -1).astype(BF16)
    wkv = mla_w_ukv.astype(BF16)
    wo = w_out.astype(BF16)
    w1, w3, w2 = ffn_w1.astype(BF16), ffn_w3.astype(BF16), ffn_w2.astype(BF16)
    row = lambda a: a.reshape(1, -1)

    for i in range(depth):
        last = i == depth - 1
        z, zs = _win(xx, mod[i], row(norm1_g[i]), w_main[i], w_small[i], n_lat)
        gla = _gla(z, zs, fw[i], fb[i], row(gla_onorm_g[i]), n_lat)
        cs = _conv_mixers(z, conf_dw[i], row(conf_dw_b[i]), row(conf_ln_g[i]), row(conf_ln_b[i]), sc_dw[i], n_lat)
        qq, kk, vv = _mla_proj(zs, row(mla_q_norm_g[i]), row(mla_kv_norm_g[i]), wq[i], wkv[i], cos_t, sin_t)
        mla = _attention(qq, kk, vv, n_lat, not last)
        n_rows = n_lat if last else n_lat + n_ctx
        xn, h2 = _wout(gla, cs, mla, xx, mod[i], row(norm2_g[i]), wo[i], n_lat, n_rows)
        xx = _ffn(xn, h2, mod[i], w1[i], w3[i], w2[i], row(final_norm_g), n_lat, last)
    return xx
```

```python
import functools
import math

import jax
import jax.numpy as jnp
from jax import lax
from jax.experimental import pallas as pl
from jax.experimental.pallas import tpu as pltpu

F32 = jnp.float32
BF16 = jnp.bfloat16

EPS = 1e-6
GRID_W = 64
HEADS = 4
HEAD_W = 128
GLA_GATE_RANK = 16
GLA_GATE_NORM = 16.0
GLA_CHUNK = 64
GLA_DIAG = 16
GLA_SAFE_DECAY = 60.0
CONF_KERNEL = 31
SC_KERNEL = 3
MIX_W = 512
MLA_ROPE = 64
MLA_Q_RANK = 384
MLA_KV_RANK = 128
ROPE_FREQS = 16
ROPE_BASE = 10000.0
Q_BLOCK = 256
SEQ_BLOCK = 256
HALO = 16
SUBLANES = 8
ZC_W = 5 * MIX_W
ZG_W = 4 * MIX_W
MOD_ROWS = 8
ZS_W = 640
VMEM_LIMIT_V7X = 56 * 1024 * 1024


def _dot(a, b):
    return jnp.dot(a, b, preferred_element_type=F32)


def _dot_nt(a, b):
    return lax.dot_general(a, b, (((1,), (1,)), ((), ())), preferred_element_type=F32)


def _dot_tn(a, b):
    return lax.dot_general(a, b, (((0,), (0,)), ((), ())), preferred_element_type=F32)


def _split3(x):
    h1 = x.astype(BF16)
    r1 = x - h1.astype(F32)
    h2 = r1.astype(BF16)
    h3 = (r1 - h2.astype(F32)).astype(BF16)
    return h1, h2, h3


def _sigmoid(x):
    return 1.0 / (1.0 + jnp.exp(-x))


def _silu(x):
    return x * _sigmoid(x)


def _rms(x, g):
    return x * lax.rsqrt(jnp.mean(x * x, axis=-1, keepdims=True) + EPS) * g


def _params(sem):
    return pltpu.CompilerParams(dimension_semantics=sem, vmem_limit_bytes=VMEM_LIMIT_V7X)


def _mod_kernel(c_ref, w_ref, b_ref, o_ref):
    tn = w_ref.shape[2]
    rows = []
    nrow = c_ref.shape[0]
    for m in range(nrow):
        am = _silu(c_ref[m])
        cols = [jnp.sum(w_ref[0, :, cb * 128:(cb + 1) * 128] * am, axis=0, keepdims=True)
                for cb in range(tn // 128)]
        rows.append(jnp.concatenate(cols, axis=1) + b_ref[0])
    rows.append(jnp.zeros((MOD_ROWS - nrow, tn), F32))
    o_ref[0] = jnp.concatenate(rows, axis=0)


def _modulation(cb, w_mod, b_mod):
    depth, d, n6 = w_mod.shape
    tn = 1024
    return pl.pallas_call(
        _mod_kernel,
        grid=(depth, n6 // tn),
        in_specs=[pl.BlockSpec(cb.shape, lambda i, j: (0, 0, 0)),
                  pl.BlockSpec((1, d, tn), lambda i, j: (i, 0, j)),
                  pl.BlockSpec((1, 1, tn), lambda i, j: (i, 0, j))],
        out_specs=pl.BlockSpec((1, MOD_ROWS, tn), lambda i, j: (i, 0, j)),
        out_shape=jax.ShapeDtypeStruct((depth, MOD_ROWS, n6), F32),
        compiler_params=_params(("arbitrary", "arbitrary")),
        name="modulation",
    )(cb, w_mod, b_mod.reshape(depth, 1, n6))


def _mod_specs(layer, b, d):
    lat = lambda bi, *_: (layer, bi, 0, 0)
    ctx = lambda *_: (layer, b, 0, 0)
    return (pl.BlockSpec((None, 1, 6, d), lat), pl.BlockSpec((None, 1, 6, d), ctx))


def _win_kernel(x_ref, ml_ref, mc_ref, g_ref, w_ref, ws_ref, z_ref, zs_ref, h_sc, *, n_lat, rc):
    t = pl.program_id(1)
    j = pl.program_id(2)
    tm = x_ref.shape[1]

    @pl.when(j == 0)
    def _():
        for r in range(tm // rc):
            rows = slice(r * rc, (r + 1) * rc)
            y = _rms(x_ref[0, rows, :], g_ref[...])
            rowid = t * tm + r * rc + lax.broadcasted_iota(jnp.int32, (rc, 1), 0)
            is_ctx = rowid >= n_lat
            sh = jnp.where(is_ctx, mc_ref[0, 0:1, :], ml_ref[0, 0:1, :])
            sc = jnp.where(is_ctx, mc_ref[0, 1:2, :], ml_ref[0, 1:2, :])
            h = (y * (1.0 + sc) + sh).astype(BF16)
            h_sc[rows, :] = h
            zs_ref[0, rows, :] = _dot(h, ws_ref[...])
            z_ref[0, rows, :] = _dot(h, w_ref[...]).astype(BF16)

    @pl.when(j != 0)
    def _():
        for r in range(tm // rc):
            rows = slice(r * rc, (r + 1) * rc)
            z_ref[0, rows, :] = _dot(h_sc[rows, :], w_ref[...]).astype(BF16)


def _win(xx, mod, g, w_main, w_small, n_lat, layer):
    b, nt, d = xx.shape
    tm = nt // 4
    tn = 512
    nw = w_main.shape[2]
    kern = functools.partial(_win_kernel, n_lat=n_lat, rc=tm // 4)
    ml, mc = _mod_specs(layer, b, d)
    return pl.pallas_call(
        kern,
        grid=(b, nt // tm, nw // tn),
        in_specs=[pl.BlockSpec((1, tm, d), lambda bi, t, j: (bi, t, 0)),
                  ml, mc,
                  pl.BlockSpec((1, d), lambda bi, t, j: (0, 0)),
                  pl.BlockSpec((None, d, tn), lambda bi, t, j: (layer, 0, j)),
                  pl.BlockSpec((None, d, ZS_W), lambda bi, t, j: (layer, 0, 0))],
        out_specs=[pl.BlockSpec((1, tm, tn), lambda bi, t, j: (bi, t, j)),
                   pl.BlockSpec((1, tm, ZS_W), lambda bi, t, j: (bi, t, 0))],
        out_shape=[jax.ShapeDtypeStruct((b, nt, nw), BF16),
                   jax.ShapeDtypeStruct((b, nt, ZS_W), F32)],
        scratch_shapes=[pltpu.VMEM((tm, d), BF16)],
        compiler_params=_params(("arbitrary", "arbitrary", "arbitrary")),
        name="norm1_w_in",
    )(xx, mod, mod, g, w_main, w_small)


def _gla_kernel(q_ref, k_ref, v_ref, g_ref, zs_ref, fw_ref, fb_ref, on_ref, o_ref,
                bc_sc, qin_sc, kv_sc, er_sc, oacc_sc, msk_sc, tri_sc, sel_sc, *, n_lat):
    n_tot = q_ref.shape[1]
    blk = SEQ_BLOCK
    cpb = blk // GLA_CHUNK
    nblk = n_tot // blk
    nchunk = n_tot // GLA_CHUNK
    lat_chunks = n_lat // GLA_CHUNK
    ctx_chunks = nchunk - lat_chunks
    ndb = blk // GLA_DIAG
    qscale = HEAD_W ** -0.5
    unroll = 4

    ri = lax.broadcasted_iota(jnp.int32, (blk, blk), 0)
    ci = lax.broadcasted_iota(jnp.int32, (blk, blk), 1)
    same64 = (ri >> 6) == (ci >> 6)
    same16 = (ri >> 4) == (ci >> 4)
    one = lambda m: jnp.where(m, 1.0, 0.0).astype(F32)
    msk_sc[0] = one(same64)
    msk_sc[1] = one((ri >> 5) == (ci >> 5))
    msk_sc[2] = one(same16 & ((ri & 15) >= (ci & 15)))
    msk_sc[3] = one(same16 & ((ri & 15) <= (ci & 15)))
    msk_sc[4] = one(same64 & (ri >= ci))
    msk_sc[5] = one(same64 & (ri <= ci))
    tri_sc[0] = msk_sc[4].astype(BF16)
    tri_sc[1] = msk_sc[5].astype(BF16)
    cj = lax.broadcasted_iota(jnp.int32, (HEAD_W, blk), 1) & (GLA_DIAG - 1)
    for jj in range(GLA_DIAG):
        sel_sc[jj * HEAD_W:(jj + 1) * HEAD_W, :] = one(cj == jj).astype(BF16)

    row = lax.broadcasted_iota(jnp.int32, (blk, 1), 0)
    half0 = (row & 63) < 32
    quart0 = (row & 31) < 16

    def block_rows(i, nb=1):
        start = i * blk if isinstance(i, int) else pl.multiple_of(i * blk, blk)
        return pl.ds(start, nb * blk)

    def paired(fn):
        def body(p, carry):
            fn(2 * p, 2)
            return carry
        lax.fori_loop(0, nblk // 2, body, 0)
        if nblk % 2:
            fn(nblk - 1, 1)

    def prologue(i0, nb):
        for u in range(nb):
            rows = block_rows(i0 + u)
            zh, zl, _ = _split3(zs_ref[0, rows, :])
            zcat = jnp.concatenate([zh, zl, zh], axis=1)
            for d in range(2):
                fh, fl, _ = _split3(fw_ref[d, 0])
                xg = _dot(zcat, jnp.concatenate([fh, fh, fl], axis=0)) + fb_ref[d, 0]
                logd = (jnp.minimum(xg, 0.0) - jnp.log1p(jnp.exp(-jnp.abs(xg)))) * (1.0 / GLA_GATE_NORM)
                c3 = _dot(tri_sc[d], jnp.concatenate(_split3(logd), axis=1))
                bc_sc[d, rows, :] = c3[:, 0:HEAD_W] + c3[:, HEAD_W:2 * HEAD_W] + c3[:, 2 * HEAD_W:]

    paired(prologue)

    def run_direction(d):
        rev = d == 1

        def load_block(i):
            rows = block_rows(i)
            b = bc_sc[d, rows, :]
            q = q_ref[0, rows, :].astype(F32) * qscale
            k = k_ref[0, rows, :].astype(F32)
            v = v_ref[0, rows, :]
            b4 = b.reshape(cpb, GLA_CHUNK, HEAD_W)
            blast = b4[:, 0:1, :] if rev else b4[:, GLA_CHUNK - 1:GLA_CHUNK, :]
            kst = (k.reshape(cpb, GLA_CHUNK, HEAD_W) * jnp.exp(blast - b4)).astype(BF16)
            v4 = v.reshape(cpb, GLA_CHUNK, HEAD_W)
            ebl = jnp.exp(blast)
            for c in range(cpb):
                kv_sc[i * cpb + c] = _dot_tn(v4[c], kst[c])
                er_sc[i * cpb + c] = jnp.broadcast_to(ebl[c], (SUBLANES, HEAD_W))
            qin = (q * jnp.exp(b)).astype(BF16)
            qin_sc[rows, :] = qin
            return rows, b, q, k, v, qin

        def store_block(rows, scores, v):
            o_blk = _dot(scores.astype(BF16), v)
            if rev:
                oacc_sc[rows, :] += o_blk
            else:
                oacc_sc[rows, :] = o_blk

        def intra_plain(i):
            rows, b, q, k, v, qin = load_block(i)
            s = _dot_nt(qin, (k * jnp.exp(-b)).astype(BF16))
            store_block(rows, s * msk_sc[5 if rev else 4], v)

        def intra_robust(i):
            rows, b, q, k, v, _ = load_block(i)
            b4 = b.reshape(cpb, GLA_CHUNK, HEAD_W)
            r1 = b4[:, 32:33, :] if rev else b4[:, 31:32, :]
            r1 = jnp.broadcast_to(r1, b4.shape).reshape(blk, HEAD_W)
            qsel, ksel = (half0, ~half0) if rev else (~half0, half0)
            qa = q * jnp.where(qsel, jnp.exp(jnp.minimum(b - r1, 0.0)), 0.0)
            ka = k * jnp.where(ksel, jnp.exp(jnp.minimum(r1 - b, 0.0)), 0.0)
            s1 = _dot_nt(qa.astype(BF16), ka.astype(BF16))
            b8 = b.reshape(blk // 32, 32, HEAD_W)
            r2 = b8[:, 16:17, :] if rev else b8[:, 15:16, :]
            r2 = jnp.broadcast_to(r2, b8.shape).reshape(blk, HEAD_W)
            qsel, ksel = (quart0, ~quart0) if rev else (~quart0, quart0)
            qb = q * jnp.where(qsel, jnp.exp(jnp.minimum(b - r2, 0.0)), 0.0)
            kb = k * jnp.where(ksel, jnp.exp(jnp.minimum(r2 - b, 0.0)), 0.0)
            s2 = _dot_nt(qb.astype(BF16), kb.astype(BF16))
            q16 = q.reshape(ndb, GLA_DIAG, HEAD_W)
            k16 = k.reshape(ndb, GLA_DIAG, HEAD_W)
            b16 = b.reshape(ndb, GLA_DIAG, HEAD_W)
            half = GLA_DIAG // 2
            terms = []
            for jj in range(GLA_DIAG):
                bj = b16[:, jj:jj + 1, :]
                kj = k16[:, jj:jj + 1, :]
                need = ((True, jj >= half) if rev else (jj < half, True))
                parts = []
                for hsel, needed in zip((slice(0, half), slice(half, GLA_DIAG)), need):
                    if needed:
                        e = jnp.exp(jnp.minimum(b16[:, hsel, :] - bj, 0.0))
                        parts.append(q16[:, hsel, :] * kj * e)
                    else:
                        parts.append(jnp.zeros((ndb, half, HEAD_W), F32))
                terms.append(jnp.concatenate(parts, axis=1).reshape(blk, HEAD_W).astype(BF16))
            sdiag = _dot(jnp.concatenate(terms, axis=1), sel_sc[...])
            store_block(rows, s1 * msk_sc[0] + s2 * msk_sc[1] + sdiag * msk_sc[3 if rev else 2], v)

        def intra(i0, nb):
            plain_ok = jnp.min(bc_sc[d, block_rows(i0, nb), :]) >= -GLA_SAFE_DECAY

            @pl.when(plain_ok)
            def _():
                for u in range(nb):
                    intra_plain(i0 + u)

            @pl.when(jnp.logical_not(plain_ok))
            def _():
                for u in range(nb):
                    intra_robust(i0 + u)

        paired(intra)

        def inter(m, st):
            for u in range(unroll):
                n = m * unroll + u
                if rev:
                    cid = jnp.where(n < ctx_chunks, lat_chunks + (ctx_chunks - 1 - n),
                                    lat_chunks - 1 - (n - ctx_chunks))
                else:
                    cid = jnp.where(n < ctx_chunks, lat_chunks + n, n - ctx_chunks)
                rows = pl.ds(pl.multiple_of(cid * GLA_CHUNK, GLA_CHUNK), GLA_CHUNK)
                oacc_sc[rows, :] += _dot_nt(qin_sc[rows, :], st.astype(BF16))
                st = st * er_sc[cid][0:1, :] + kv_sc[cid]
            return st

        lax.fori_loop(0, nchunk // unroll, inter, jnp.zeros((HEAD_W, HEAD_W), F32))

    run_direction(0)
    run_direction(1)

    def finish(i, carry):
        rows = block_rows(i)
        y = _rms(oacc_sc[rows, :], on_ref[...])
        o_ref[0, rows, :] = (y * _silu(g_ref[0, rows, :].astype(F32))).astype(BF16)
        return carry

    lax.fori_loop(0, nblk, finish, 0)


def _gla(z, zs, fw, fb, on, n_lat, layer):
    b, nt, _ = z.shape
    cb0 = ZC_W // HEAD_W
    seq = lambda off: pl.BlockSpec((1, nt, HEAD_W), lambda bi, h: (bi, 0, cb0 + off * HEADS + h))
    kern = functools.partial(_gla_kernel, n_lat=n_lat)
    return pl.pallas_call(
        kern,
        grid=(b, HEADS),
        in_specs=[seq(0), seq(1), seq(2), seq(3),
                  pl.BlockSpec((1, nt, HEAD_W), lambda bi, h: (bi, 0, ZS_W // HEAD_W - 1)),
                  pl.BlockSpec((None, 2, 1, HEAD_W, HEAD_W), lambda bi, h: (layer, 0, h, 0, 0)),
                  pl.BlockSpec((2, 1, 1, HEAD_W), lambda bi, h: (0, h, 0, 0)),
                  pl.BlockSpec((1, HEAD_W), lambda bi, h: (0, 0))],
        out_specs=pl.BlockSpec((1, nt, HEAD_W), lambda bi, h: (bi, 0, h)),
        out_shape=jax.ShapeDtypeStruct((b, nt, MIX_W), BF16),
        scratch_shapes=[pltpu.VMEM((2, nt, HEAD_W), F32),
                        pltpu.VMEM((nt, HEAD_W), BF16),
                        pltpu.VMEM((nt // GLA_CHUNK, HEAD_W, HEAD_W), F32),
                        pltpu.VMEM((nt // GLA_CHUNK, SUBLANES, HEAD_W), F32),
                        pltpu.VMEM((nt, HEAD_W), F32),
                        pltpu.VMEM((6, SEQ_BLOCK, SEQ_BLOCK), F32),
                        pltpu.VMEM((2, SEQ_BLOCK, SEQ_BLOCK), BF16),
                        pltpu.VMEM((GLA_DIAG * HEAD_W, SEQ_BLOCK), BF16)],
        compiler_params=_params(("arbitrary", "arbitrary")),
        name="gla_mixer",
    )(z, z, z, z, zs, fw, fb, on)


def _conv_kernel(zc_ref, zp_ref, zn_ref, dw_ref, dwb_ref, lng_ref, lnb_ref, scw_ref, o_ref,
                 u_sc, m_sc, *, lat_tiles, n_tiles):
    t = pl.program_id(1)
    tl = zc_ref.shape[1]
    w = MIX_W
    first = (t == 0) | (t == lat_tiles)
    last = (t == lat_tiles - 1) | (t == n_tiles - 1)

    def glu(ref, rows):
        return ref[0, rows, 0:w].astype(F32) * _sigmoid(ref[0, rows, w:2 * w].astype(F32))

    def gated(ref, rows):
        return ref[0, rows, 3 * w:4 * w].astype(F32) * ref[0, rows, 4 * w:5 * w].astype(F32)

    halo = slice(0, HALO)
    u_sc[0, 0:HALO, :] = jnp.where(first, 0.0, glu(zp_ref, halo))
    m_sc[0:HALO, :] = jnp.where(first, 0.0, gated(zp_ref, halo))
    u_sc[0, HALO + tl:, :] = jnp.where(last, 0.0, glu(zn_ref, halo))
    m_sc[HALO + tl:, :] = jnp.where(last, 0.0, gated(zn_ref, halo))
    rc = 32
    for r in range(tl // rc):
        rows = slice(r * rc, (r + 1) * rc)
        u_sc[0, HALO + r * rc:HALO + (r + 1) * rc, :] = glu(zc_ref, rows)
        m_sc[HALO + r * rc:HALO + (r + 1) * rc, :] = gated(zc_ref, rows)
    span = tl + 2 * HALO - SUBLANES
    for s in range(1, SUBLANES):
        u_sc[s, 0:span, :] = u_sc[0, s:s + span, :]

    pad = CONF_KERNEL // 2
    for r in range(tl // rc):
        r0 = r * rc
        acc = jnp.zeros((rc, w), F32)
        for kk in range(CONF_KERNEL):
            off = HALO + r0 + kk - pad
            base = off - off % SUBLANES
            acc = acc + dw_ref[kk:kk + 1, :] * u_sc[off % SUBLANES, base:base + rc, :]
        acc = acc + dwb_ref[...]
        mu = jnp.mean(acc, axis=-1, keepdims=True)
        xc = acc - mu
        y = xc * lax.rsqrt(jnp.mean(xc * xc, axis=-1, keepdims=True) + EPS) * lng_ref[...] + lnb_ref[...]
        o_ref[0, r0:r0 + rc, 0:w] = _silu(y).astype(BF16)
        s = HALO + r0
        cv = (scw_ref[0:1, :] * m_sc[s - 1:s - 1 + rc, :] + scw_ref[1:2, :] * m_sc[s:s + rc, :]
              + scw_ref[2:3, :] * m_sc[s + 1:s + 1 + rc, :])
        o_ref[0, r0:r0 + rc, w:2 * w] = (zc_ref[0, r0:r0 + rc, 2 * w:3 * w].astype(F32) * cv).astype(BF16)


def _conv_mixers(z, dw, dwb, lng, lnb, scw, n_lat):
    b, nt, _ = z.shape
    tl = SEQ_BLOCK
    n_tiles = nt // tl
    hb = tl // HALO
    nhb = nt // HALO
    kern = functools.partial(_conv_kernel, lat_tiles=n_lat // tl, n_tiles=n_tiles)
    vec = lambda n: pl.BlockSpec((n, MIX_W), lambda bi, t: (0, 0))
    return pl.pallas_call(
        kern,
        grid=(b, n_tiles),
        in_specs=[pl.BlockSpec((1, tl, ZC_W), lambda bi, t: (bi, t, 0)),
                  pl.BlockSpec((1, HALO, ZC_W), lambda bi, t: (bi, jnp.maximum(t * hb - 1, 0), 0)),
                  pl.BlockSpec((1, HALO, ZC_W), lambda bi, t: (bi, jnp.minimum((t + 1) * hb, nhb - 1), 0)),
                  vec(CONF_KERNEL), vec(1), vec(1), vec(1), vec(SC_KERNEL)],
        out_specs=pl.BlockSpec((1, tl, 2 * MIX_W), lambda bi, t: (bi, t, 0)),
        out_shape=jax.ShapeDtypeStruct((b, nt, 2 * MIX_W), BF16),
        scratch_shapes=[pltpu.VMEM((SUBLANES, tl + 2 * HALO, MIX_W), F32),
                        pltpu.VMEM((tl + 2 * HALO, MIX_W), F32)],
        compiler_params=_params(("arbitrary", "arbitrary")),
        name="conv_mixers",
    )(z, z, z, dw, dwb, lng, lnb, scw)


def _mla_proj_kernel(zs_ref, qg_ref, kvg_ref, wq_ref, wkv_ref, cos_ref, sin_ref,
                     q_ref, k_ref, v_ref, *, rc):
    tm = zs_ref.shape[1]
    scale = (HEAD_W + MLA_ROPE) ** -0.5 * math.log2(math.e)
    lane = lax.broadcasted_iota(jnp.int32, (1, HEAD_W), 1)
    lower = (lane & 31) < 16

    for r in range(tm // rc):
        rows = slice(r * rc, (r + 1) * rc)
        cq = zs_ref[0, rows, 0:MLA_Q_RANK]
        ckv = zs_ref[0, rows, MLA_Q_RANK:MLA_Q_RANK + MLA_KV_RANK]
        krb = zs_ref[0, rows, MLA_Q_RANK + MLA_KV_RANK:ZS_W]
        q = _dot(_rms(cq, qg_ref[...]).astype(BF16), wq_ref[...])
        kv = _dot(_rms(ckv, kvg_ref[...]).astype(BF16), wkv_ref[...])
        cs = cos_ref[rows, :]
        sn = sin_ref[rows, :]

        def rope(tt):
            swapped = jnp.where(lower, pltpu.roll(tt, HEAD_W - 16, 1), pltpu.roll(tt, 16, 1))
            return tt * cs + swapped * sn

        krr = rope(krb).astype(BF16)
        for h in range(HEADS):
            c0 = 2 * HEAD_W * h
            q_ref[0, rows, c0:c0 + HEAD_W] = (q[:, c0:c0 + HEAD_W] * scale).astype(BF16)
            q_ref[0, rows, c0 + HEAD_W:c0 + 2 * HEAD_W] = (
                rope(q[:, c0 + HEAD_W:c0 + 2 * HEAD_W]) * scale).astype(BF16)
            k_ref[0, rows, c0:c0 + HEAD_W] = kv[:, c0:c0 + HEAD_W].astype(BF16)
            k_ref[0, rows, c0 + HEAD_W:c0 + 2 * HEAD_W] = krr
            v_ref[0, rows, HEAD_W * h:HEAD_W * (h + 1)] = kv[:, c0 + HEAD_W:c0 + 2 * HEAD_W].astype(BF16)


def _mla_proj(zs, qg, kvg, wq, wkv, cos_t, sin_t, layer):
    b, nt, _ = zs.shape
    tm = nt // 4
    kern = functools.partial(_mla_proj_kernel, rc=tm // 4)
    full = lambda a: pl.BlockSpec(a.shape, lambda bi, t: (0,) * a.ndim)
    stacked = lambda a: pl.BlockSpec((None,) + a.shape[1:], lambda bi, t: (layer,) + (0,) * (a.ndim - 1))
    kw = 2 * HEAD_W * HEADS
    return pl.pallas_call(
        kern,
        grid=(b, nt // tm),
        in_specs=[pl.BlockSpec((1, tm, ZS_W), lambda bi, t: (bi, t, 0)),
                  full(qg), full(kvg), stacked(wq), stacked(wkv),
                  pl.BlockSpec((tm, HEAD_W), lambda bi, t: (t, 0)),
                  pl.BlockSpec((tm, HEAD_W), lambda bi, t: (t, 0))],
        out_specs=[pl.BlockSpec((1, tm, kw), lambda bi, t: (bi, t, 0)),
                   pl.BlockSpec((1, tm, kw), lambda bi, t: (bi, t, 0)),
                   pl.BlockSpec((1, tm, MIX_W), lambda bi, t: (bi, t, 0))],
        out_shape=[jax.ShapeDtypeStruct((b, nt, kw), BF16),
                   jax.ShapeDtypeStruct((b, nt, kw), BF16),
                   jax.ShapeDtypeStruct((b, nt, MIX_W), BF16)],
        compiler_params=_params(("arbitrary", "arbitrary")),
        name="mla_proj",
    )(zs, qg, kvg, wq, wkv, cos_t, sin_t)


def _attn_kernel(q_ref, k_ref, v_ref, o_ref, *, n_lat):
    qi = pl.program_id(2)
    q = q_ref[0]

    def attend(k, v):
        s = _dot_nt(q, k)
        p = jnp.exp2(s - jnp.max(s, axis=-1, keepdims=True))
        denom = jnp.sum(p, axis=-1, keepdims=True)
        o_ref[0] = (_dot(p.astype(BF16), v) / denom).astype(BF16)

    @pl.when(qi < n_lat // Q_BLOCK)
    def _():
        attend(k_ref[0], v_ref[0])

    @pl.when(qi >= n_lat // Q_BLOCK)
    def _():
        attend(k_ref[0, n_lat:, :], v_ref[0, n_lat:, :])


def _attention(q, k, v, n_lat, need_ctx):
    b, nt, _ = q.shape
    nq = (nt if need_ctx else n_lat) // Q_BLOCK
    kern = functools.partial(_attn_kernel, n_lat=n_lat)
    return pl.pallas_call(
        kern,
        grid=(b, HEADS, nq),
        in_specs=[pl.BlockSpec((1, Q_BLOCK, 2 * HEAD_W), lambda bi, h, qi: (bi, qi, h)),
                  pl.BlockSpec((1, nt, 2 * HEAD_W), lambda bi, h, qi: (bi, 0, h)),
                  pl.BlockSpec((1, nt, HEAD_W), lambda bi, h, qi: (bi, 0, h))],
        out_specs=pl.BlockSpec((1, Q_BLOCK, HEAD_W), lambda bi, h, qi: (bi, qi, h)),
        out_shape=jax.ShapeDtypeStruct((b, nq * Q_BLOCK, MIX_W), BF16),
        compiler_params=_params(("arbitrary", "arbitrary", "arbitrary")),
        name="mla_attention",
    )(q, k, v)


def _wout_kernel(gla_ref, cs_ref, mla_ref, x_ref, ml_ref, mc_ref, g_ref, w_ref, xo_ref, h_ref,
                 *, n_lat, rc):
    t = pl.program_id(1)
    tm = x_ref.shape[1]
    for r in range(tm // rc):
        rows = slice(r * rc, (r + 1) * rc)
        mix = jnp.concatenate([gla_ref[0, rows, :], cs_ref[0, rows, :], mla_ref[0, rows, :]], axis=1)
        o = _dot(mix, w_ref[...])
        rowid = t * tm + r * rc + lax.broadcasted_iota(jnp.int32, (rc, 1), 0)
        is_ctx = rowid >= n_lat
        g1 = jnp.where(is_ctx, mc_ref[0, 2:3, :], ml_ref[0, 2:3, :])
        sh = jnp.where(is_ctx, mc_ref[0, 3:4, :], ml_ref[0, 3:4, :])
        sc = jnp.where(is_ctx, mc_ref[0, 4:5, :], ml_ref[0, 4:5, :])
        xn = x_ref[0, rows, :] + g1 * o
        xo_ref[0, rows, :] = xn
        h_ref[0, rows, :] = (_rms(xn, g_ref[...]) * (1.0 + sc) + sh).astype(BF16)


def _wout(gla, cs, mla, xx, mod, g, w, n_lat, n_rows, layer):
    b, _, d = xx.shape
    tm = n_rows // 8
    kern = functools.partial(_wout_kernel, n_lat=n_lat, rc=tm // 2)
    tile = lambda wd: pl.BlockSpec((1, tm, wd), lambda bi, t: (bi, t, 0))
    ml, mc = _mod_specs(layer, b, d)
    return pl.pallas_call(
        kern,
        grid=(b, n_rows // tm),
        in_specs=[tile(MIX_W), tile(2 * MIX_W), tile(MIX_W), tile(d), ml, mc,
                  pl.BlockSpec((1, d), lambda bi, t: (0, 0)),
                  pl.BlockSpec((None,) + w.shape[1:], lambda bi, t: (layer, 0, 0))],
        out_specs=[tile(d), tile(d)],
        out_shape=[jax.ShapeDtypeStruct((b, n_rows, d), F32),
                   jax.ShapeDtypeStruct((b, n_rows, d), BF16)],
        compiler_params=_params(("arbitrary", "arbitrary")),
        name="w_out_norm2",
    )(gla, cs, mla, xx, mod, mod, g, w)


def _ffn_up_kernel(h_ref, w1_ref, w3_ref, u_ref, *, rc):
    tm = h_ref.shape[1]
    for r in range(tm // rc):
        rows = slice(r * rc, (r + 1) * rc)
        hh = h_ref[0, rows, :]
        u_ref[0, rows, :] = (_silu(_dot(hh, w1_ref[...])) * _dot(hh, w3_ref[...])).astype(BF16)


def _ffn_up(h2, w1, w3, layer):
    b, n_rows, d = h2.shape
    tm = n_rows // 4
    tf = 512
    dff = w1.shape[2]
    kern = functools.partial(_ffn_up_kernel, rc=tm // 4)
    wspec = pl.BlockSpec((None, d, tf), lambda bi, t, j: (layer, 0, j))
    return pl.pallas_call(
        kern,
        grid=(b, n_rows // tm, dff // tf),
        in_specs=[pl.BlockSpec((1, tm, d), lambda bi, t, j: (bi, t, 0)), wspec, wspec],
        out_specs=pl.BlockSpec((1, tm, tf), lambda bi, t, j: (bi, t, j)),
        out_shape=jax.ShapeDtypeStruct((b, n_rows, dff), BF16),
        compiler_params=_params(("arbitrary", "arbitrary", "arbitrary")),
        name="ffn_up",
    )(h2, w1, w3)


def _ffn_down_kernel(u_ref, x_ref, ml_ref, mc_ref, w2_ref, fg_ref, o_ref, *, n_lat, rc, final):
    t = pl.program_id(1)
    j = pl.program_id(2)
    tm = u_ref.shape[1]
    tn = w2_ref.shape[1]
    cols = pl.ds(pl.multiple_of(j * tn, tn), tn)
    for r in range(tm // rc):
        rows = slice(r * rc, (r + 1) * rc)
        rowid = t * tm + r * rc + lax.broadcasted_iota(jnp.int32, (rc, 1), 0)
        g2 = jnp.where(rowid >= n_lat, mc_ref[0, 5:6, cols], ml_ref[0, 5:6, cols])
        o_ref[0, rows, cols] = x_ref[0, rows, cols] + g2 * _dot(u_ref[0, rows, :], w2_ref[...])

    if final:
        @pl.when(j == pl.num_programs(2) - 1)
        def _():
            for r in range(tm // rc):
                rows = slice(r * rc, (r + 1) * rc)
                o_ref[0, rows, :] = _rms(o_ref[0, rows, :], fg_ref[...])


def _ffn_down(u, xn, mod, w2, fg, n_lat, final, layer):
    b, n_rows, d = xn.shape
    tm = n_rows // 8
    tn = 512
    dff = w2.shape[1]
    kern = functools.partial(_ffn_down_kernel, n_lat=n_lat, rc=tm // 2, final=final)
    ml, mc = _mod_specs(layer, b, d)
    return pl.pallas_call(
        kern,
        grid=(b, n_rows // tm, d // tn),
        in_specs=[pl.BlockSpec((1, tm, dff), lambda bi, t, j: (bi, t, 0)),
                  pl.BlockSpec((1, tm, d), lambda bi, t, j: (bi, t, 0)),
                  ml, mc,
                  pl.BlockSpec((None, dff, tn), lambda bi, t, j: (layer, 0, j)),
                  pl.BlockSpec((1, d), lambda bi, t, j: (0, 0))],
        out_specs=pl.BlockSpec((1, tm, d), lambda bi, t, j: (bi, t, 0)),
        out_shape=jax.ShapeDtypeStruct((b, n_rows, d), F32),
        compiler_params=_params(("arbitrary", "arbitrary", "arbitrary")),
        name="ffn_down",
    )(u, xn, mod, mod, w2, fg)


def _rope_tables(n_lat, n_ctx):
    rows = n_lat // GRID_W
    rowp = jnp.repeat(jnp.arange(rows, dtype=F32), GRID_W)
    colp = jnp.tile(jnp.arange(GRID_W, dtype=F32), rows)
    inv = ROPE_BASE ** (-jnp.arange(ROPE_FREQS, dtype=F32) * 2.0 / (2 * ROPE_FREQS))
    ar, ac = rowp[:, None] * inv, colp[:, None] * inv
    zeros = jnp.zeros((n_lat, HEAD_W - MLA_ROPE), F32)
    cos_l = jnp.concatenate([jnp.cos(ar), jnp.cos(ar), jnp.cos(ac), jnp.cos(ac), zeros], axis=1)
    sin_l = jnp.concatenate([-jnp.sin(ar), jnp.sin(ar), -jnp.sin(ac), jnp.sin(ac), zeros], axis=1)
    cos_c = jnp.concatenate([jnp.ones((n_ctx, MLA_ROPE), F32), jnp.zeros((n_ctx, HEAD_W - MLA_ROPE), F32)], axis=1)
    return jnp.concatenate([cos_l, cos_c], axis=0), jnp.concatenate([sin_l, jnp.zeros_like(cos_c)], axis=0)


def kernel(x, c, ctx, c_ctx, norm1_g, w_mod, b_mod, w_in, gla_fg_up, gla_fg_b, gla_onorm_g, conf_dw, conf_dw_b, conf_ln_g, conf_ln_b, sc_dw, mla_q_norm_g, mla_kv_norm_g, mla_w_uq, mla_w_ukv, w_out, norm2_g, ffn_w1, ffn_w3, ffn_w2, final_norm_g):
    bsz, n_lat, d = x.shape
    n_ctx = ctx.shape[1]
    depth = w_in.shape[0]
    w = MIX_W

    xx = jnp.concatenate([x, ctx], axis=1)
    c3 = jnp.concatenate([c, c_ctx[None, :]], axis=0)
    mod = _modulation(jnp.broadcast_to(c3[:, :, None], (bsz + 1, d, 128)), w_mod, b_mod)
    mod = mod.reshape(depth, MOD_ROWS, 6, d)
    cos_t, sin_t = _rope_tables(n_lat, n_ctx)

    o_gla, o_lr, o_conv, o_mla = 0, 4 * w, 4 * w + 2 * GLA_GATE_RANK, 4 * w + 2 * GLA_GATE_RANK + 5 * w
    w_main = jnp.concatenate([w_in[:, :, o_conv:o_mla], w_in[:, :, o_gla:o_lr]], axis=-1).astype(BF16)
    w_small = jnp.concatenate([w_in[:, :, o_mla:], w_in[:, :, o_lr:o_conv],
                               jnp.zeros((depth, d, ZS_W - (w_in.shape[2] - o_mla) - 2 * GLA_GATE_RANK), F32)],
                              axis=-1).astype(BF16)
    lr0 = MLA_ROPE
    fw = jnp.zeros((depth, 2, HEAD_W, HEADS * HEAD_W), F32)
    fw = fw.at[:, 0, lr0:lr0 + GLA_GATE_RANK].set(gla_fg_up[:, 0])
    fw = fw.at[:, 1, lr0 + GLA_GATE_RANK:lr0 + 2 * GLA_GATE_RANK].set(gla_fg_up[:, 1])
    fw = fw.reshape(depth, 2, HEAD_W, HEADS, HEAD_W).transpose(0, 1, 3, 2, 4)
    fb = gla_fg_b.reshape(depth, 2, HEADS, 1, HEAD_W)
    wq = mla_w_uq.reshape(depth, MLA_Q_RANK, HEADS, HEAD_W + MLA_ROPE)
    wq = jnp.pad(wq, ((0, 0), (0, 0), (0, 0), (0, HEAD_W - MLA_ROPE))).reshape(depth, MLA_Q_RANK, -1).astype(BF16)
    wkv = mla_w_ukv.astype(BF16)
    wo = w_out.astype(BF16)
    w1, w3, w2 = ffn_w1.astype(BF16), ffn_w3.astype(BF16), ffn_w2.astype(BF16)
    row = lambda a: a.reshape(1, -1)

    for i in range(depth):
        last = i == depth - 1
        z, zs = _win(xx, mod, row(norm1_g[i]), w_main, w_small, n_lat, i)
        gla = _gla(z, zs, fw, fb[i], row(gla_onorm_g[i]), n_lat, i)
        cs = _conv_mixers(z, conf_dw[i], row(conf_dw_b[i]), row(conf_ln_g[i]), row(conf_ln_b[i]), sc_dw[i], n_lat)
        qq, kk, vv = _mla_proj(zs, row(mla_q_norm_g[i]), row(mla_kv_norm_g[i]), wq, wkv, cos_t, sin_t, i)
        mla = _attention(qq, kk, vv, n_lat, not last)
        n_rows = n_lat if last else n_lat + n_ctx
        xn, h2 = _wout(gla, cs, mla, xx, mod, row(norm2_g[i]), wo, n_lat, n_rows, i)
        u = _ffn_up(h2, w1, w3, i)
        xx = _ffn_down(u, xn, mod, w2, row(final_norm_g), n_lat, last, i)
    return xx
```

```python
import functools
import math

import jax
import jax.numpy as jnp
from jax import lax
from jax.experimental import pallas as pl
from jax.experimental.pallas import tpu as pltpu

F32 = jnp.float32
BF16 = jnp.bfloat16

EPS = 1e-6
GRID_W = 64
HEADS = 4
HEAD_W = 128
GLA_GATE_RANK = 16
GLA_GATE_NORM = 16.0
GLA_CHUNK = 64
GLA_DIAG = 16
GLA_SAFE_DECAY = 60.0
CONF_KERNEL = 31
SC_KERNEL = 3
MIX_W = 512
MLA_ROPE = 64
MLA_Q_RANK = 384
MLA_KV_RANK = 128
ROPE_FREQS = 16
ROPE_BASE = 10000.0
Q_BLOCK = 1024
Q_SUB = 256
SEQ_BLOCK = 256
HALO = 16
SUBLANES = 8
ZC_W = 5 * MIX_W
ZG_W = 4 * MIX_W
MOD_ROWS = 8
ZS_W = 640
VMEM_LIMIT_V7X = 56 * 1024 * 1024


def _dot(a, b):
    return jnp.dot(a, b, preferred_element_type=F32)


def _dot_nt(a, b):
    return lax.dot_general(a, b, (((1,), (1,)), ((), ())), preferred_element_type=F32)


def _dot_tn(a, b):
    return lax.dot_general(a, b, (((0,), (0,)), ((), ())), preferred_element_type=F32)


def _split3(x):
    h1 = x.astype(BF16)
    r1 = x - h1.astype(F32)
    h2 = r1.astype(BF16)
    h3 = (r1 - h2.astype(F32)).astype(BF16)
    return h1, h2, h3


def _sigmoid(x):
    return 1.0 / (1.0 + jnp.exp(-x))


def _silu(x):
    return x * _sigmoid(x)


def _rms(x, g):
    return x * lax.rsqrt(jnp.mean(x * x, axis=-1, keepdims=True) + EPS) * g


def _params(sem):
    return pltpu.CompilerParams(dimension_semantics=sem, vmem_limit_bytes=VMEM_LIMIT_V7X)


def _mod_kernel(c_ref, w_ref, b_ref, o_ref, act_sc):
    tn = w_ref.shape[2]
    rows = []
    nrow = c_ref.shape[0]

    @pl.when((pl.program_id(0) == 0) & (pl.program_id(1) == 0))
    def _():
        for m in range(nrow):
            act_sc[m] = _silu(c_ref[m])

    kc = 8 * SUBLANES
    ncb = tn // 128
    acc = [[jnp.zeros((SUBLANES, 128), F32) for _ in range(ncb)] for _ in range(nrow)]
    for k0 in range(0, w_ref.shape[1], kc):
        wk = w_ref[0, k0:k0 + kc, :]
        for m in range(nrow):
            am = act_sc[m, k0:k0 + kc, :]
            for cb in range(ncb):
                part = wk[:, cb * 128:(cb + 1) * 128] * am
                acc[m][cb] = acc[m][cb] + jnp.sum(part.reshape(kc // SUBLANES, SUBLANES, 128), axis=0)
    for m in range(nrow):
        out_m = jnp.concatenate([jnp.sum(a, axis=0, keepdims=True) for a in acc[m]], axis=1)
        rows.append(out_m + b_ref[0])
    rows.append(jnp.zeros((MOD_ROWS - nrow, tn), F32))
    o_ref[0] = jnp.concatenate(rows, axis=0)


def _modulation(cb, w_mod, b_mod):
    depth, d, n6 = w_mod.shape
    tn = 1024
    return pl.pallas_call(
        _mod_kernel,
        grid=(depth, n6 // tn),
        in_specs=[pl.BlockSpec(cb.shape, lambda i, j: (0, 0, 0)),
                  pl.BlockSpec((1, d, tn), lambda i, j: (i, 0, j)),
                  pl.BlockSpec((1, 1, tn), lambda i, j: (i, 0, j))],
        out_specs=pl.BlockSpec((1, MOD_ROWS, tn), lambda i, j: (i, 0, j)),
        out_shape=jax.ShapeDtypeStruct((depth, MOD_ROWS, n6), F32),
        scratch_shapes=[pltpu.VMEM(cb.shape, F32)],
        compiler_params=_params(("arbitrary", "arbitrary")),
        name="modulation",
    )(cb, w_mod, b_mod.reshape(depth, 1, n6))


def _mod_specs(layer, b, d):
    lat = lambda bi, *_: (layer, bi, 0, 0)
    ctx = lambda *_: (layer, b, 0, 0)
    return (pl.BlockSpec((None, 1, 6, d), lat), pl.BlockSpec((None, 1, 6, d), ctx))


def _win_kernel(x_ref, ml_ref, mc_ref, g_ref, w_ref, ws_ref, z_ref, zs_ref, h_sc, *, n_lat, rc):
    t = pl.program_id(1)
    j = pl.program_id(2)
    tm = x_ref.shape[1]

    @pl.when(j == 0)
    def _():
        for r in range(tm // rc):
            rows = slice(r * rc, (r + 1) * rc)
            y = _rms(x_ref[0, rows, :], g_ref[...])
            rowid = t * tm + r * rc + lax.broadcasted_iota(jnp.int32, (rc, 1), 0)
            is_ctx = rowid >= n_lat
            sh = jnp.where(is_ctx, mc_ref[0, 0:1, :], ml_ref[0, 0:1, :])
            sc = jnp.where(is_ctx, mc_ref[0, 1:2, :], ml_ref[0, 1:2, :])
            h = (y * (1.0 + sc) + sh).astype(BF16)
            h_sc[rows, :] = h
            zs_ref[0, rows, :] = _dot(h, ws_ref[...])
            z_ref[0, rows, :] = _dot(h, w_ref[...]).astype(BF16)

    @pl.when(j != 0)
    def _():
        for r in range(tm // rc):
            rows = slice(r * rc, (r + 1) * rc)
            z_ref[0, rows, :] = _dot(h_sc[rows, :], w_ref[...]).astype(BF16)


def _win(xx, mod, g, w_main, w_small, n_lat, layer):
    b, nt, d = xx.shape
    tm = nt // 4
    tn = 512
    nw = w_main.shape[2]
    kern = functools.partial(_win_kernel, n_lat=n_lat, rc=tm // 4)
    ml, mc = _mod_specs(layer, b, d)
    return pl.pallas_call(
        kern,
        grid=(b, nt // tm, nw // tn),
        in_specs=[pl.BlockSpec((1, tm, d), lambda bi, t, j: (bi, t, 0)),
                  ml, mc,
                  pl.BlockSpec((1, d), lambda bi, t, j: (0, 0)),
                  pl.BlockSpec((None, d, tn), lambda bi, t, j: (layer, 0, j)),
                  pl.BlockSpec((None, d, ZS_W), lambda bi, t, j: (layer, 0, 0))],
        out_specs=[pl.BlockSpec((1, tm, tn), lambda bi, t, j: (bi, t, j)),
                   pl.BlockSpec((1, tm, ZS_W), lambda bi, t, j: (bi, t, 0))],
        out_shape=[jax.ShapeDtypeStruct((b, nt, nw), BF16),
                   jax.ShapeDtypeStruct((b, nt, ZS_W), F32)],
        scratch_shapes=[pltpu.VMEM((tm, d), BF16)],
        compiler_params=_params(("arbitrary", "arbitrary", "arbitrary")),
        name="norm1_w_in",
    )(xx, mod, mod, g, w_main, w_small)


def _gla_kernel(q_ref, k_ref, v_ref, g_ref, zs_ref, fw_ref, fb_ref, on_ref, o_ref,
                bc_sc, qin_sc, kv_sc, er_sc, oacc_sc, msk_sc, tri_sc, sel_sc, *, n_lat):
    n_tot = q_ref.shape[1]
    blk = SEQ_BLOCK
    cpb = blk // GLA_CHUNK
    nblk = n_tot // blk
    nchunk = n_tot // GLA_CHUNK
    lat_chunks = n_lat // GLA_CHUNK
    ctx_chunks = nchunk - lat_chunks
    ndb = blk // GLA_DIAG
    qscale = HEAD_W ** -0.5
    unroll = 4

    ri = lax.broadcasted_iota(jnp.int32, (blk, blk), 0)
    ci = lax.broadcasted_iota(jnp.int32, (blk, blk), 1)
    same64 = (ri >> 6) == (ci >> 6)
    same16 = (ri >> 4) == (ci >> 4)
    one = lambda m: jnp.where(m, 1.0, 0.0).astype(F32)
    msk_sc[0] = one(same64)
    msk_sc[1] = one((ri >> 5) == (ci >> 5))
    msk_sc[2] = one(same16 & ((ri & 15) >= (ci & 15)))
    msk_sc[3] = one(same16 & ((ri & 15) <= (ci & 15)))
    msk_sc[4] = one(same64 & (ri >= ci))
    msk_sc[5] = one(same64 & (ri <= ci))
    tri_sc[0] = msk_sc[4].astype(BF16)
    tri_sc[1] = msk_sc[5].astype(BF16)
    cj = lax.broadcasted_iota(jnp.int32, (HEAD_W, blk), 1) & (GLA_DIAG - 1)
    for jj in range(GLA_DIAG):
        sel_sc[jj * HEAD_W:(jj + 1) * HEAD_W, :] = one(cj == jj).astype(BF16)

    row = lax.broadcasted_iota(jnp.int32, (blk, 1), 0)
    half0 = (row & 63) < 32
    quart0 = (row & 31) < 16

    def block_rows(i, nb=1):
        start = i * blk if isinstance(i, int) else pl.multiple_of(i * blk, blk)
        return pl.ds(start, nb * blk)

    def paired(fn):
        def body(p, carry):
            fn(2 * p, 2)
            return carry
        lax.fori_loop(0, nblk // 2, body, 0)
        if nblk % 2:
            fn(nblk - 1, 1)

    fsplit = [_split3(fw_ref[d, 0]) for d in range(2)]
    fcat = jnp.concatenate([jnp.concatenate([fh, fh, fl], axis=0) for fh, fl, _ in fsplit], axis=1)

    def prologue(i0, nb):
        for u in range(nb):
            rows = block_rows(i0 + u)
            zh, zl, _ = _split3(zs_ref[0, rows, :])
            xg2 = _dot(jnp.concatenate([zh, zl, zh], axis=1), fcat)
            for d in range(2):
                xg = xg2[:, d * HEAD_W:(d + 1) * HEAD_W] + fb_ref[d, 0]
                logd = (jnp.minimum(xg, 0.0) - jnp.log1p(jnp.exp(-jnp.abs(xg)))) * (1.0 / GLA_GATE_NORM)
                c3 = _dot(tri_sc[d], jnp.concatenate(_split3(logd), axis=1))
                bc_sc[d, rows, :] = c3[:, 0:HEAD_W] + c3[:, HEAD_W:2 * HEAD_W] + c3[:, 2 * HEAD_W:]

    paired(prologue)

    def run_direction(d):
        rev = d == 1

        def load_block(i):
            rows = block_rows(i)
            b = bc_sc[d, rows, :]
            q = q_ref[0, rows, :].astype(F32) * qscale
            k = k_ref[0, rows, :].astype(F32)
            v = v_ref[0, rows, :]
            b4 = b.reshape(cpb, GLA_CHUNK, HEAD_W)
            blast = b4[:, 0:1, :] if rev else b4[:, GLA_CHUNK - 1:GLA_CHUNK, :]
            kst = (k.reshape(cpb, GLA_CHUNK, HEAD_W) * jnp.exp(blast - b4)).astype(BF16)
            v4 = v.reshape(cpb, GLA_CHUNK, HEAD_W)
            ebl = jnp.exp(blast)
            for c in range(cpb):
                kv_sc[i * cpb + c] = _dot_tn(v4[c], kst[c])
                er_sc[i * cpb + c] = jnp.broadcast_to(ebl[c], (SUBLANES, HEAD_W))
            qin = (q * jnp.exp(b)).astype(BF16)
            qin_sc[rows, :] = qin
            return rows, b, q, k, v, qin

        def store_block(rows, scores, v):
            o_blk = _dot(scores.astype(BF16), v)
            if rev:
                oacc_sc[rows, :] += o_blk
            else:
                oacc_sc[rows, :] = o_blk

        def intra_plain(i):
            rows, b, q, k, v, qin = load_block(i)
            s = _dot_nt(qin, (k * jnp.exp(-b)).astype(BF16))
            store_block(rows, s * msk_sc[5 if rev else 4], v)

        def intra_robust(i):
            rows, b, q, k, v, _ = load_block(i)
            b4 = b.reshape(cpb, GLA_CHUNK, HEAD_W)
            r1 = b4[:, 32:33, :] if rev else b4[:, 31:32, :]
            r1 = jnp.broadcast_to(r1, b4.shape).reshape(blk, HEAD_W)
            qsel, ksel = (half0, ~half0) if rev else (~half0, half0)
            qa = q * jnp.where(qsel, jnp.exp(jnp.minimum(b - r1, 0.0)), 0.0)
            ka = k * jnp.where(ksel, jnp.exp(jnp.minimum(r1 - b, 0.0)), 0.0)
            s1 = _dot_nt(qa.astype(BF16), ka.astype(BF16))
            b8 = b.reshape(blk // 32, 32, HEAD_W)
            r2 = b8[:, 16:17, :] if rev else b8[:, 15:16, :]
            r2 = jnp.broadcast_to(r2, b8.shape).reshape(blk, HEAD_W)
            qsel, ksel = (quart0, ~quart0) if rev else (~quart0, quart0)
            qb = q * jnp.where(qsel, jnp.exp(jnp.minimum(b - r2, 0.0)), 0.0)
            kb = k * jnp.where(ksel, jnp.exp(jnp.minimum(r2 - b, 0.0)), 0.0)
            s2 = _dot_nt(qb.astype(BF16), kb.astype(BF16))
            q16 = q.reshape(ndb, GLA_DIAG, HEAD_W)
            k16 = k.reshape(ndb, GLA_DIAG, HEAD_W)
            b16 = b.reshape(ndb, GLA_DIAG, HEAD_W)
            half = GLA_DIAG // 2
            terms = []
            for jj in range(GLA_DIAG):
                bj = b16[:, jj:jj + 1, :]
                kj = k16[:, jj:jj + 1, :]
                need = ((True, jj >= half) if rev else (jj < half, True))
                parts = []
                for hsel, needed in zip((slice(0, half), slice(half, GLA_DIAG)), need):
                    if needed:
                        e = jnp.exp(jnp.minimum(b16[:, hsel, :] - bj, 0.0))
                        parts.append(q16[:, hsel, :] * kj * e)
                    else:
                        parts.append(jnp.zeros((ndb, half, HEAD_W), F32))
                terms.append(jnp.concatenate(parts, axis=1).reshape(blk, HEAD_W).astype(BF16))
            sdiag = _dot(jnp.concatenate(terms, axis=1), sel_sc[...])
            store_block(rows, s1 * msk_sc[0] + s2 * msk_sc[1] + sdiag * msk_sc[3 if rev else 2], v)

        def intra(i0, nb):
            plain_ok = jnp.min(bc_sc[d, block_rows(i0, nb), :]) >= -GLA_SAFE_DECAY

            @pl.when(plain_ok)
            def _():
                for u in range(nb):
                    intra_plain(i0 + u)

            @pl.when(jnp.logical_not(plain_ok))
            def _():
                for u in range(nb):
                    intra_robust(i0 + u)

        paired(intra)

        def inter(m, st):
            for u in range(unroll):
                n = m * unroll + u
                if rev:
                    cid = jnp.where(n < ctx_chunks, lat_chunks + (ctx_chunks - 1 - n),
                                    lat_chunks - 1 - (n - ctx_chunks))
                else:
                    cid = jnp.where(n < ctx_chunks, lat_chunks + n, n - ctx_chunks)
                rows = pl.ds(pl.multiple_of(cid * GLA_CHUNK, GLA_CHUNK), GLA_CHUNK)
                oacc_sc[rows, :] += _dot_nt(qin_sc[rows, :], st.astype(BF16))
                st = st * er_sc[cid][0:1, :] + kv_sc[cid]
            return st

        lax.fori_loop(0, nchunk // unroll, inter, jnp.zeros((HEAD_W, HEAD_W), F32))

    run_direction(0)
    run_direction(1)

    def finish(i, carry):
        rows = block_rows(i)
        y = _rms(oacc_sc[rows, :], on_ref[...])
        o_ref[0, rows, :] = (y * _silu(g_ref[0, rows, :].astype(F32))).astype(BF16)
        return carry

    lax.fori_loop(0, nblk, finish, 0)


def _gla(z, zs, fw, fb, on, n_lat, layer):
    b, nt, _ = z.shape
    cb0 = ZC_W // HEAD_W
    seq = lambda off: pl.BlockSpec((1, nt, HEAD_W), lambda bi, h: (bi, 0, cb0 + off * HEADS + h))
    kern = functools.partial(_gla_kernel, n_lat=n_lat)
    return pl.pallas_call(
        kern,
        grid=(b, HEADS),
        in_specs=[seq(0), seq(1), seq(2), seq(3),
                  pl.BlockSpec((1, nt, HEAD_W), lambda bi, h: (bi, 0, ZS_W // HEAD_W - 1)),
                  pl.BlockSpec((None, 2, 1, HEAD_W, HEAD_W), lambda bi, h: (layer, 0, h, 0, 0)),
                  pl.BlockSpec((2, 1, 1, HEAD_W), lambda bi, h: (0, h, 0, 0)),
                  pl.BlockSpec((1, HEAD_W), lambda bi, h: (0, 0))],
        out_specs=pl.BlockSpec((1, nt, HEAD_W), lambda bi, h: (bi, 0, h)),
        out_shape=jax.ShapeDtypeStruct((b, nt, MIX_W), BF16),
        scratch_shapes=[pltpu.VMEM((2, nt, HEAD_W), F32),
                        pltpu.VMEM((nt, HEAD_W), BF16),
                        pltpu.VMEM((nt // GLA_CHUNK, HEAD_W, HEAD_W), F32),
                        pltpu.VMEM((nt // GLA_CHUNK, SUBLANES, HEAD_W), F32),
                        pltpu.VMEM((nt, HEAD_W), F32),
                        pltpu.VMEM((6, SEQ_BLOCK, SEQ_BLOCK), F32),
                        pltpu.VMEM((2, SEQ_BLOCK, SEQ_BLOCK), BF16),
                        pltpu.VMEM((GLA_DIAG * HEAD_W, SEQ_BLOCK), BF16)],
        compiler_params=_params(("arbitrary", "arbitrary")),
        name="gla_mixer",
    )(z, z, z, z, zs, fw, fb, on)


def _conv_kernel(zc_ref, zp_ref, zn_ref, dw_ref, dwb_ref, lng_ref, lnb_ref, scw_ref, o_ref,
                 u_sc, m_sc, *, lat_tiles, n_tiles):
    t = pl.program_id(1)
    tl = zc_ref.shape[1]
    w = MIX_W
    first = (t == 0) | (t == lat_tiles)
    last = (t == lat_tiles - 1) | (t == n_tiles - 1)

    def glu(ref, rows):
        return ref[0, rows, 0:w].astype(F32) * _sigmoid(ref[0, rows, w:2 * w].astype(F32))

    def gated(ref, rows):
        return ref[0, rows, 3 * w:4 * w].astype(F32) * ref[0, rows, 4 * w:5 * w].astype(F32)

    halo = slice(0, HALO)
    u_sc[0, 0:HALO, :] = jnp.where(first, 0.0, glu(zp_ref, halo))
    m_sc[0:HALO, :] = jnp.where(first, 0.0, gated(zp_ref, halo))
    u_sc[0, HALO + tl:, :] = jnp.where(last, 0.0, glu(zn_ref, halo))
    m_sc[HALO + tl:, :] = jnp.where(last, 0.0, gated(zn_ref, halo))
    rc = 32
    for r in range(tl // rc):
        rows = slice(r * rc, (r + 1) * rc)
        u_sc[0, HALO + r * rc:HALO + (r + 1) * rc, :] = glu(zc_ref, rows)
        m_sc[HALO + r * rc:HALO + (r + 1) * rc, :] = gated(zc_ref, rows)
    span = tl + 2 * HALO - SUBLANES
    for s in range(1, SUBLANES):
        u_sc[s, 0:span, :] = u_sc[0, s:s + span, :]

    pad = CONF_KERNEL // 2
    for r in range(tl // rc):
        r0 = r * rc
        acc = jnp.zeros((rc, w), F32)
        for kk in range(CONF_KERNEL):
            off = HALO + r0 + kk - pad
            base = off - off % SUBLANES
            acc = acc + dw_ref[kk:kk + 1, :] * u_sc[off % SUBLANES, base:base + rc, :]
        acc = acc + dwb_ref[...]
        mu = jnp.mean(acc, axis=-1, keepdims=True)
        xc = acc - mu
        y = xc * lax.rsqrt(jnp.mean(xc * xc, axis=-1, keepdims=True) + EPS) * lng_ref[...] + lnb_ref[...]
        o_ref[0, r0:r0 + rc, 0:w] = _silu(y).astype(BF16)
        s = HALO + r0
        cv = (scw_ref[0:1, :] * m_sc[s - 1:s - 1 + rc, :] + scw_ref[1:2, :] * m_sc[s:s + rc, :]
              + scw_ref[2:3, :] * m_sc[s + 1:s + 1 + rc, :])
        o_ref[0, r0:r0 + rc, w:2 * w] = (zc_ref[0, r0:r0 + rc, 2 * w:3 * w].astype(F32) * cv).astype(BF16)


def _conv_mixers(z, dw, dwb, lng, lnb, scw, n_lat):
    b, nt, _ = z.shape
    tl = SEQ_BLOCK
    n_tiles = nt // tl
    hb = tl // HALO
    nhb = nt // HALO
    kern = functools.partial(_conv_kernel, lat_tiles=n_lat // tl, n_tiles=n_tiles)
    vec = lambda n: pl.BlockSpec((n, MIX_W), lambda bi, t: (0, 0))
    return pl.pallas_call(
        kern,
        grid=(b, n_tiles),
        in_specs=[pl.BlockSpec((1, tl, ZC_W), lambda bi, t: (bi, t, 0)),
                  pl.BlockSpec((1, HALO, ZC_W), lambda bi, t: (bi, jnp.maximum(t * hb - 1, 0), 0)),
                  pl.BlockSpec((1, HALO, ZC_W), lambda bi, t: (bi, jnp.minimum((t + 1) * hb, nhb - 1), 0)),
                  vec(CONF_KERNEL), vec(1), vec(1), vec(1), vec(SC_KERNEL)],
        out_specs=pl.BlockSpec((1, tl, 2 * MIX_W), lambda bi, t: (bi, t, 0)),
        out_shape=jax.ShapeDtypeStruct((b, nt, 2 * MIX_W), BF16),
        scratch_shapes=[pltpu.VMEM((SUBLANES, tl + 2 * HALO, MIX_W), F32),
                        pltpu.VMEM((tl + 2 * HALO, MIX_W), F32)],
        compiler_params=_params(("arbitrary", "arbitrary")),
        name="conv_mixers",
    )(z, z, z, dw, dwb, lng, lnb, scw)


def _mla_proj_kernel(zs_ref, qg_ref, kvg_ref, wq_ref, wkv_ref, cos_ref, sin_ref,
                     q_ref, k_ref, v_ref, *, rc):
    tm = zs_ref.shape[1]
    scale = (HEAD_W + MLA_ROPE) ** -0.5 * math.log2(math.e)
    lane = lax.broadcasted_iota(jnp.int32, (1, HEAD_W), 1)
    lower = (lane & 31) < 16

    for r in range(tm // rc):
        rows = slice(r * rc, (r + 1) * rc)
        cq = zs_ref[0, rows, 0:MLA_Q_RANK]
        ckv = zs_ref[0, rows, MLA_Q_RANK:MLA_Q_RANK + MLA_KV_RANK]
        krb = zs_ref[0, rows, MLA_Q_RANK + MLA_KV_RANK:ZS_W]
        q = _dot(_rms(cq, qg_ref[...]).astype(BF16), wq_ref[...])
        kv = _dot(_rms(ckv, kvg_ref[...]).astype(BF16), wkv_ref[...])
        cs = cos_ref[rows, :]
        sn = sin_ref[rows, :]

        def rope(tt):
            swapped = jnp.where(lower, pltpu.roll(tt, HEAD_W - 16, 1), pltpu.roll(tt, 16, 1))
            return tt * cs + swapped * sn

        krr = rope(krb).astype(BF16)
        for h in range(HEADS):
            c0 = 2 * HEAD_W * h
            q_ref[0, rows, c0:c0 + HEAD_W] = (q[:, c0:c0 + HEAD_W] * scale).astype(BF16)
            q_ref[0, rows, c0 + HEAD_W:c0 + 2 * HEAD_W] = (
                rope(q[:, c0 + HEAD_W:c0 + 2 * HEAD_W]) * scale).astype(BF16)
            k_ref[0, rows, c0:c0 + HEAD_W] = kv[:, c0:c0 + HEAD_W].astype(BF16)
            k_ref[0, rows, c0 + HEAD_W:c0 + 2 * HEAD_W] = krr
            v_ref[0, rows, HEAD_W * h:HEAD_W * (h + 1)] = kv[:, c0 + HEAD_W:c0 + 2 * HEAD_W].astype(BF16)


def _mla_proj(zs, qg, kvg, wq, wkv, cos_t, sin_t, layer):
    b, nt, _ = zs.shape
    tm = nt // 4
    kern = functools.partial(_mla_proj_kernel, rc=tm // 4)
    full = lambda a: pl.BlockSpec(a.shape, lambda bi, t: (0,) * a.ndim)
    stacked = lambda a: pl.BlockSpec((None,) + a.shape[1:], lambda bi, t: (layer,) + (0,) * (a.ndim - 1))
    kw = 2 * HEAD_W * HEADS
    return pl.pallas_call(
        kern,
        grid=(b, nt // tm),
        in_specs=[pl.BlockSpec((1, tm, ZS_W), lambda bi, t: (bi, t, 0)),
                  full(qg), full(kvg), stacked(wq), stacked(wkv),
                  pl.BlockSpec((tm, HEAD_W), lambda bi, t: (t, 0)),
                  pl.BlockSpec((tm, HEAD_W), lambda bi, t: (t, 0))],
        out_specs=[pl.BlockSpec((1, tm, kw), lambda bi, t: (bi, t, 0)),
                   pl.BlockSpec((1, tm, kw), lambda bi, t: (bi, t, 0)),
                   pl.BlockSpec((1, tm, MIX_W), lambda bi, t: (bi, t, 0))],
        out_shape=[jax.ShapeDtypeStruct((b, nt, kw), BF16),
                   jax.ShapeDtypeStruct((b, nt, kw), BF16),
                   jax.ShapeDtypeStruct((b, nt, MIX_W), BF16)],
        compiler_params=_params(("arbitrary", "arbitrary")),
        name="mla_proj",
    )(zs, qg, kvg, wq, wkv, cos_t, sin_t)


def _attn_kernel(q_ref, k_ref, v_ref, o_ref, *, sub):
    for r0 in range(0, q_ref.shape[1], sub):
        rows = slice(r0, r0 + sub)
        s = _dot_nt(q_ref[0, rows, :], k_ref[0])
        p = jnp.exp2(s - jnp.max(s, axis=-1, keepdims=True))
        denom = jnp.sum(p, axis=-1, keepdims=True)
        o_ref[0, rows, :] = (_dot(p.astype(BF16), v_ref[0]) / denom).astype(BF16)


def _attention(q, k, v, n_lat, need_ctx):
    b, nt, _ = q.shape
    kern = functools.partial(_attn_kernel, sub=Q_SUB)
    out = pl.pallas_call(
        kern,
        grid=(b, HEADS, n_lat // Q_BLOCK),
        in_specs=[pl.BlockSpec((1, Q_BLOCK, 2 * HEAD_W), lambda bi, h, qi: (bi, qi, h)),
                  pl.BlockSpec((1, nt, 2 * HEAD_W), lambda bi, h, qi: (bi, 0, h)),
                  pl.BlockSpec((1, nt, HEAD_W), lambda bi, h, qi: (bi, 0, h))],
        out_specs=pl.BlockSpec((1, Q_BLOCK, HEAD_W), lambda bi, h, qi: (bi, qi, h)),
        out_shape=jax.ShapeDtypeStruct((b, n_lat, MIX_W), BF16),
        compiler_params=_params(("arbitrary", "arbitrary", "arbitrary")),
        name="mla_attention",
    )(q, k, v)
    if not need_ctx:
        return out
    n_ctx = nt - n_lat
    cblk = n_lat // n_ctx
    ctx_spec = lambda wd: pl.BlockSpec((1, n_ctx, wd), lambda bi, h: (bi, cblk, h))
    out_ctx = pl.pallas_call(
        kern,
        grid=(b, HEADS),
        in_specs=[ctx_spec(2 * HEAD_W), ctx_spec(2 * HEAD_W), ctx_spec(HEAD_W)],
        out_specs=pl.BlockSpec((1, n_ctx, HEAD_W), lambda bi, h: (bi, 0, h)),
        out_shape=jax.ShapeDtypeStruct((b, n_ctx, MIX_W), BF16),
        compiler_params=_params(("arbitrary", "arbitrary")),
        name="mla_attention_ctx",
    )(q, k, v)
    return jnp.concatenate([out, out_ctx], axis=1)


def _wout_kernel(gla_ref, cs_ref, mla_ref, x_ref, ml_ref, mc_ref, g_ref, w_ref, xo_ref, h_ref,
                 *, n_lat, rc):
    t = pl.program_id(1)
    tm = x_ref.shape[1]
    for r in range(tm // rc):
        rows = slice(r * rc, (r + 1) * rc)
        mix = jnp.concatenate([gla_ref[0, rows, :], cs_ref[0, rows, :], mla_ref[0, rows, :]], axis=1)
        o = _dot(mix, w_ref[...])
        rowid = t * tm + r * rc + lax.broadcasted_iota(jnp.int32, (rc, 1), 0)
        is_ctx = rowid >= n_lat
        g1 = jnp.where(is_ctx, mc_ref[0, 2:3, :], ml_ref[0, 2:3, :])
        sh = jnp.where(is_ctx, mc_ref[0, 3:4, :], ml_ref[0, 3:4, :])
        sc = jnp.where(is_ctx, mc_ref[0, 4:5, :], ml_ref[0, 4:5, :])
        xn = x_ref[0, rows, :] + g1 * o
        xo_ref[0, rows, :] = xn
        h_ref[0, rows, :] = (_rms(xn, g_ref[...]) * (1.0 + sc) + sh).astype(BF16)


def _wout(gla, cs, mla, xx, mod, g, w, n_lat, n_rows, layer):
    b, _, d = xx.shape
    tm = n_rows // 8
    kern = functools.partial(_wout_kernel, n_lat=n_lat, rc=tm // 2)
    tile = lambda wd: pl.BlockSpec((1, tm, wd), lambda bi, t: (bi, t, 0))
    ml, mc = _mod_specs(layer, b, d)
    return pl.pallas_call(
        kern,
        grid=(b, n_rows // tm),
        in_specs=[tile(MIX_W), tile(2 * MIX_W), tile(MIX_W), tile(d), ml, mc,
                  pl.BlockSpec((1, d), lambda bi, t: (0, 0)),
                  pl.BlockSpec((None,) + w.shape[1:], lambda bi, t: (layer, 0, 0))],
        out_specs=[tile(d), tile(d)],
        out_shape=[jax.ShapeDtypeStruct((b, n_rows, d), F32),
                   jax.ShapeDtypeStruct((b, n_rows, d), BF16)],
        compiler_params=_params(("arbitrary", "arbitrary")),
        name="w_out_norm2",
    )(gla, cs, mla, xx, mod, mod, g, w)


def _ffn_up_kernel(h_ref, w1_ref, w3_ref, u_ref, *, rc):
    tm = h_ref.shape[1]
    w1 = w1_ref[...].astype(BF16)
    w3 = w3_ref[...].astype(BF16)
    for r in range(tm // rc):
        rows = slice(r * rc, (r + 1) * rc)
        hh = h_ref[0, rows, :]
        u_ref[0, rows, :] = (_silu(_dot(hh, w1)) * _dot(hh, w3)).astype(BF16)


def _ffn_up(h2, w1, w3, layer):
    b, n_rows, d = h2.shape
    tm = n_rows // 4
    tf = 512
    dff = w1.shape[2]
    kern = functools.partial(_ffn_up_kernel, rc=tm // 4)
    wspec = pl.BlockSpec((None, d, tf), lambda bi, t, j: (layer, 0, j))
    return pl.pallas_call(
        kern,
        grid=(b, n_rows // tm, dff // tf),
        in_specs=[pl.BlockSpec((1, tm, d), lambda bi, t, j: (bi, t, 0)), wspec, wspec],
        out_specs=pl.BlockSpec((1, tm, tf), lambda bi, t, j: (bi, t, j)),
        out_shape=jax.ShapeDtypeStruct((b, n_rows, dff), BF16),
        compiler_params=_params(("arbitrary", "arbitrary", "arbitrary")),
        name="ffn_up",
    )(h2, w1, w3)


def _ffn_down_kernel(u_ref, x_ref, ml_ref, mc_ref, w2_ref, fg_ref, o_ref, *, n_lat, rc, final):
    t = pl.program_id(1)
    j = pl.program_id(2)
    tm = u_ref.shape[1]
    tn = w2_ref.shape[1]
    cols = pl.ds(pl.multiple_of(j * tn, tn), tn)
    xo_cols = cols if final else slice(None)
    for r in range(tm // rc):
        rows = slice(r * rc, (r + 1) * rc)
        rowid = t * tm + r * rc + lax.broadcasted_iota(jnp.int32, (rc, 1), 0)
        g2 = jnp.where(rowid >= n_lat, mc_ref[0, 5:6, cols], ml_ref[0, 5:6, cols])
        o_ref[0, rows, xo_cols] = x_ref[0, rows, xo_cols] + g2 * _dot(u_ref[0, rows, :], w2_ref[...])

    if final:
        @pl.when(j == pl.num_programs(2) - 1)
        def _():
            for r in range(tm // rc):
                rows = slice(r * rc, (r + 1) * rc)
                o_ref[0, rows, :] = _rms(o_ref[0, rows, :], fg_ref[...])


def _ffn_down(u, xn, mod, w2, fg, n_lat, final, layer):
    b, n_rows, d = xn.shape
    tn = 512
    dff = w2.shape[1]
    if final:
        tm = n_rows // 8
        xo_spec = pl.BlockSpec((1, tm, d), lambda bi, t, j: (bi, t, 0))
    else:
        tm = n_rows // 4
        xo_spec = pl.BlockSpec((1, tm, tn), lambda bi, t, j: (bi, t, j))
    kern = functools.partial(_ffn_down_kernel, n_lat=n_lat, rc=tm // (2 if final else 4), final=final)
    ml, mc = _mod_specs(layer, b, d)
    return pl.pallas_call(
        kern,
        grid=(b, n_rows // tm, d // tn),
        in_specs=[pl.BlockSpec((1, tm, dff), lambda bi, t, j: (bi, t, 0)),
                  xo_spec, ml, mc,
                  pl.BlockSpec((None, dff, tn), lambda bi, t, j: (layer, 0, j)),
                  pl.BlockSpec((1, d), lambda bi, t, j: (0, 0))],
        out_specs=xo_spec,
        out_shape=jax.ShapeDtypeStruct((b, n_rows, d), F32),
        compiler_params=_params(("arbitrary", "arbitrary", "arbitrary")),
        name="ffn_down",
    )(u, xn, mod, mod, w2, fg)


def _rope_tables(n_lat, n_ctx):
    rows = n_lat // GRID_W
    rowp = jnp.repeat(jnp.arange(rows, dtype=F32), GRID_W)
    colp = jnp.tile(jnp.arange(GRID_W, dtype=F32), rows)
    inv = ROPE_BASE ** (-jnp.arange(ROPE_FREQS, dtype=F32) * 2.0 / (2 * ROPE_FREQS))
    ar, ac = rowp[:, None] * inv, colp[:, None] * inv
    zeros = jnp.zeros((n_lat, HEAD_W - MLA_ROPE), F32)
    cos_l = jnp.concatenate([jnp.cos(ar), jnp.cos(ar), jnp.cos(ac), jnp.cos(ac), zeros], axis=1)
    sin_l = jnp.concatenate([-jnp.sin(ar), jnp.sin(ar), -jnp.sin(ac), jnp.sin(ac), zeros], axis=1)
    cos_c = jnp.concatenate([jnp.ones((n_ctx, MLA_ROPE), F32), jnp.zeros((n_ctx, HEAD_W - MLA_ROPE), F32)], axis=1)
    return jnp.concatenate([cos_l, cos_c], axis=0), jnp.concatenate([sin_l, jnp.zeros_like(cos_c)], axis=0)


def kernel(x, c, ctx, c_ctx, norm1_g, w_mod, b_mod, w_in, gla_fg_up, gla_fg_b, gla_onorm_g, conf_dw, conf_dw_b, conf_ln_g, conf_ln_b, sc_dw, mla_q_norm_g, mla_kv_norm_g, mla_w_uq, mla_w_ukv, w_out, norm2_g, ffn_w1, ffn_w3, ffn_w2, final_norm_g):
    bsz, n_lat, d = x.shape
    n_ctx = ctx.shape[1]
    depth = w_in.shape[0]
    w = MIX_W

    xx = jnp.concatenate([x, ctx], axis=1)
    c3 = jnp.concatenate([c, c_ctx[None, :]], axis=0)
    mod = _modulation(jnp.broadcast_to(c3[:, :, None], (bsz + 1, d, 128)), w_mod, b_mod)
    mod = mod.reshape(depth, MOD_ROWS, 6, d)
    cos_t, sin_t = _rope_tables(n_lat, n_ctx)

    o_gla, o_lr, o_conv, o_mla = 0, 4 * w, 4 * w + 2 * GLA_GATE_RANK, 4 * w + 2 * GLA_GATE_RANK + 5 * w
    w_main = jnp.concatenate([w_in[:, :, o_conv:o_mla], w_in[:, :, o_gla:o_lr]], axis=-1).astype(BF16)
    w_small = jnp.concatenate([w_in[:, :, o_mla:], w_in[:, :, o_lr:o_conv],
                               jnp.zeros((depth, d, ZS_W - (w_in.shape[2] - o_mla) - 2 * GLA_GATE_RANK), F32)],
                              axis=-1).astype(BF16)
    lr0 = MLA_ROPE
    fw = jnp.stack([jnp.pad(gla_fg_up[:, dd], ((0, 0), (lr0 + dd * GLA_GATE_RANK,
                                                        HEAD_W - lr0 - (dd + 1) * GLA_GATE_RANK), (0, 0)))
                    for dd in range(2)], axis=1)
    fw = fw.reshape(depth, 2, HEAD_W, HEADS, HEAD_W).transpose(0, 1, 3, 2, 4)
    fb = gla_fg_b.reshape(depth, 2, HEADS, 1, HEAD_W)
    wq = mla_w_uq.reshape(depth, MLA_Q_RANK, HEADS, HEAD_W + MLA_ROPE)
    wq = jnp.pad(wq, ((0, 0), (0, 0), (0, 0), (0, HEAD_W - MLA_ROPE))).reshape(depth, MLA_Q_RANK, -1).astype(BF16)
    wkv = mla_w_ukv.astype(BF16)
    wo = w_out.astype(BF16)
    w2 = ffn_w2.astype(BF16)
    row = lambda a: a.reshape(1, -1)

    for i in range(depth):
        last = i == depth - 1
        z, zs = _win(xx, mod, row(norm1_g[i]), w_main, w_small, n_lat, i)
        gla = _gla(z, zs, fw, fb[i], row(gla_onorm_g[i]), n_lat, i)
        cs = _conv_mixers(z, conf_dw[i], row(conf_dw_b[i]), row(conf_ln_g[i]), row(conf_ln_b[i]), sc_dw[i], n_lat)
        qq, kk, vv = _mla_proj(zs, row(mla_q_norm_g[i]), row(mla_kv_norm_g[i]), wq, wkv, cos_t, sin_t, i)
        mla = _attention(qq, kk, vv, n_lat, not last)
        n_rows = n_lat if last else n_lat + n_ctx
        xn, h2 = _wout(gla, cs, mla, xx, mod, row(norm2_g[i]), wo, n_lat, n_rows, i)
        u = _ffn_up(h2, ffn_w1, ffn_w3, i)
        xx = _ffn_down(u, xn, mod, w2, row(final_norm_g), n_lat, last, i)
    return xx
```

```python
import functools
import math

import jax
import jax.numpy as jnp
from jax import lax
from jax.experimental import pallas as pl
from jax.experimental.pallas import tpu as pltpu

F32 = jnp.float32
BF16 = jnp.bfloat16

EPS = 1e-6
GRID_W = 64
HEADS = 4
HEAD_W = 128
GLA_GATE_RANK = 16
GLA_GATE_NORM = 16.0
GLA_CHUNK = 64
GLA_DIAG = 16
GLA_SAFE_DECAY = 60.0
CONF_KERNEL = 31
SC_KERNEL = 3
MIX_W = 512
MLA_ROPE = 64
MLA_Q_RANK = 384
MLA_KV_RANK = 128
ROPE_FREQS = 16
ROPE_BASE = 10000.0
Q_BLOCK = 1024
Q_SUB = 256
SEQ_BLOCK = 256
HALO = 16
SUBLANES = 8
ZC_W = 5 * MIX_W
ZG_W = 4 * MIX_W
MOD_ROWS = 8
ZS_W = 640
VMEM_LIMIT_V7X = 56 * 1024 * 1024


def _dot(a, b):
    return jnp.dot(a, b, preferred_element_type=F32)


def _dot_nt(a, b):
    return lax.dot_general(a, b, (((1,), (1,)), ((), ())), preferred_element_type=F32)


def _dot_tn(a, b):
    return lax.dot_general(a, b, (((0,), (0,)), ((), ())), preferred_element_type=F32)


def _split3(x):
    h1 = x.astype(BF16)
    r1 = x - h1.astype(F32)
    h2 = r1.astype(BF16)
    h3 = (r1 - h2.astype(F32)).astype(BF16)
    return h1, h2, h3


def _sigmoid(x):
    return 1.0 / (1.0 + jnp.exp(-x))


def _silu(x):
    return x * _sigmoid(x)


def _rms(x, g):
    return x * lax.rsqrt(jnp.mean(x * x, axis=-1, keepdims=True) + EPS) * g


def _params(sem):
    return pltpu.CompilerParams(dimension_semantics=sem, vmem_limit_bytes=VMEM_LIMIT_V7X)


def _mod_kernel(c_ref, w_ref, b_ref, o_ref, act_sc):
    tn = w_ref.shape[2]
    rows = []
    nrow = c_ref.shape[0]

    @pl.when((pl.program_id(0) == 0) & (pl.program_id(1) == 0))
    def _():
        for m in range(nrow):
            act_sc[m] = _silu(c_ref[m])

    kc = 8 * SUBLANES
    ncb = tn // 128
    acc = [[jnp.zeros((SUBLANES, 128), F32) for _ in range(ncb)] for _ in range(nrow)]
    for k0 in range(0, w_ref.shape[1], kc):
        wk = w_ref[0, k0:k0 + kc, :]
        for m in range(nrow):
            am = act_sc[m, k0:k0 + kc, :]
            for cb in range(ncb):
                part = wk[:, cb * 128:(cb + 1) * 128] * am
                acc[m][cb] = acc[m][cb] + jnp.sum(part.reshape(kc // SUBLANES, SUBLANES, 128), axis=0)
    for m in range(nrow):
        out_m = jnp.concatenate([jnp.sum(a, axis=0, keepdims=True) for a in acc[m]], axis=1)
        rows.append(out_m + b_ref[0])
    rows.append(jnp.zeros((MOD_ROWS - nrow, tn), F32))
    o_ref[0] = jnp.concatenate(rows, axis=0)


def _modulation(cb, w_mod, b_mod):
    depth, d, n6 = w_mod.shape
    tn = 1024
    return pl.pallas_call(
        _mod_kernel,
        grid=(depth, n6 // tn),
        in_specs=[pl.BlockSpec(cb.shape, lambda i, j: (0, 0, 0)),
                  pl.BlockSpec((1, d, tn), lambda i, j: (i, 0, j)),
                  pl.BlockSpec((1, 1, tn), lambda i, j: (i, 0, j))],
        out_specs=pl.BlockSpec((1, MOD_ROWS, tn), lambda i, j: (i, 0, j)),
        out_shape=jax.ShapeDtypeStruct((depth, MOD_ROWS, n6), F32),
        scratch_shapes=[pltpu.VMEM(cb.shape, F32)],
        compiler_params=_params(("arbitrary", "arbitrary")),
        name="modulation",
    )(cb, w_mod, b_mod.reshape(depth, 1, n6))


def _mod_specs(layer, b, d):
    lat = lambda bi, *_: (layer, bi, 0, 0)
    ctx = lambda *_: (layer, b, 0, 0)
    return (pl.BlockSpec((None, 1, 6, d), lat), pl.BlockSpec((None, 1, 6, d), ctx))


def _win_kernel(x_ref, ml_ref, mc_ref, g_ref, w_ref, ws_ref, z_ref, zs_ref, *, n_lat, rc, nc):
    t = pl.program_id(1)
    tm = x_ref.shape[1]
    nw = w_ref.shape[1]
    for r in range(tm // rc):
        rows = slice(r * rc, (r + 1) * rc)
        y = _rms(x_ref[0, rows, :], g_ref[...])
        rowid = t * tm + r * rc + lax.broadcasted_iota(jnp.int32, (rc, 1), 0)
        is_ctx = rowid >= n_lat
        sh = jnp.where(is_ctx, mc_ref[0, 0:1, :], ml_ref[0, 0:1, :])
        sc = jnp.where(is_ctx, mc_ref[0, 1:2, :], ml_ref[0, 1:2, :])
        h = (y * (1.0 + sc) + sh).astype(BF16)
        zs_ref[0, rows, :] = _dot(h, ws_ref[...])
        for n0 in range(0, nw, nc):
            z_ref[0, rows, n0:n0 + nc] = _dot(h, w_ref[:, n0:n0 + nc]).astype(BF16)


def _win(xx, mod, g, w_main, w_small, n_lat, layer):
    b, nt, d = xx.shape
    tm = nt // 8
    nw = w_main.shape[2]
    kern = functools.partial(_win_kernel, n_lat=n_lat, rc=tm // 2, nc=nw // 3)
    ml, mc = _mod_specs(layer, b, d)
    resident = lambda wd: pl.BlockSpec((None, d, wd), lambda bi, t: (layer, 0, 0),
                                       pipeline_mode=pl.Buffered(1))
    return pl.pallas_call(
        kern,
        grid=(b, nt // tm),
        in_specs=[pl.BlockSpec((1, tm, d), lambda bi, t: (bi, t, 0)),
                  ml, mc,
                  pl.BlockSpec((1, d), lambda bi, t: (0, 0)),
                  resident(nw), resident(ZS_W)],
        out_specs=[pl.BlockSpec((1, tm, nw), lambda bi, t: (bi, t, 0)),
                   pl.BlockSpec((1, tm, ZS_W), lambda bi, t: (bi, t, 0))],
        out_shape=[jax.ShapeDtypeStruct((b, nt, nw), BF16),
                   jax.ShapeDtypeStruct((b, nt, ZS_W), F32)],
        compiler_params=_params(("arbitrary", "arbitrary")),
        name="norm1_w_in",
    )(xx, mod, mod, g, w_main, w_small)


def _gla_kernel(q_ref, k_ref, v_ref, g_ref, zs_ref, fw_ref, fb_ref, on_ref, o_ref,
                bc_sc, qin_sc, kv_sc, er_sc, oacc_sc, msk_sc, tri_sc, sel_sc, *, n_lat):
    n_tot = q_ref.shape[1]
    blk = SEQ_BLOCK
    cpb = blk // GLA_CHUNK
    nblk = n_tot // blk
    nchunk = n_tot // GLA_CHUNK
    lat_chunks = n_lat // GLA_CHUNK
    ctx_chunks = nchunk - lat_chunks
    ndb = blk // GLA_DIAG
    qscale = HEAD_W ** -0.5
    unroll = 4
    group = 4

    ri = lax.broadcasted_iota(jnp.int32, (blk, blk), 0)
    ci = lax.broadcasted_iota(jnp.int32, (blk, blk), 1)
    same64 = (ri >> 6) == (ci >> 6)
    same16 = (ri >> 4) == (ci >> 4)
    one = lambda m: jnp.where(m, 1.0, 0.0).astype(F32)
    msk_sc[0] = one(same64)
    msk_sc[1] = one((ri >> 5) == (ci >> 5))
    msk_sc[2] = one(same16 & ((ri & 15) >= (ci & 15)))
    msk_sc[3] = one(same16 & ((ri & 15) <= (ci & 15)))
    msk_sc[4] = one(same64 & (ri >= ci))
    msk_sc[5] = one(same64 & (ri <= ci))
    tri_sc[0] = msk_sc[4].astype(BF16)
    tri_sc[1] = msk_sc[5].astype(BF16)
    cj = lax.broadcasted_iota(jnp.int32, (HEAD_W, blk), 1) & (GLA_DIAG - 1)
    for jj in range(GLA_DIAG):
        sel_sc[jj * HEAD_W:(jj + 1) * HEAD_W, :] = one(cj == jj).astype(BF16)

    row = lax.broadcasted_iota(jnp.int32, (blk, 1), 0)
    half0 = (row & 63) < 32
    quart0 = (row & 31) < 16

    def block_rows(i, nb=1):
        start = i * blk if isinstance(i, int) else pl.multiple_of(i * blk, blk)
        return pl.ds(start, nb * blk)

    def paired(fn):
        def body(p, carry):
            fn(group * p, group)
            return carry
        lax.fori_loop(0, nblk // group, body, 0)
        if nblk % group:
            fn(nblk - nblk % group, nblk % group)

    fsplit = [_split3(fw_ref[d, 0]) for d in range(2)]
    fcat = jnp.concatenate([jnp.concatenate([fh, fh, fl], axis=0) for fh, fl, _ in fsplit], axis=1)

    def prologue(i0, nb):
        for u in range(nb):
            rows = block_rows(i0 + u)
            zh, zl, _ = _split3(zs_ref[0, rows, :])
            xg2 = _dot(jnp.concatenate([zh, zl, zh], axis=1), fcat)
            for d in range(2):
                xg = xg2[:, d * HEAD_W:(d + 1) * HEAD_W] + fb_ref[d, 0]
                logd = (jnp.minimum(xg, 0.0) - jnp.log1p(jnp.exp(-jnp.abs(xg)))) * (1.0 / GLA_GATE_NORM)
                c3 = _dot(tri_sc[d], jnp.concatenate(_split3(logd), axis=1))
                bc_sc[d, rows, :] = c3[:, 0:HEAD_W] + c3[:, HEAD_W:2 * HEAD_W] + c3[:, 2 * HEAD_W:]

    paired(prologue)

    def run_direction(d):
        rev = d == 1

        def load_block(i):
            rows = block_rows(i)
            b = bc_sc[d, rows, :]
            q = q_ref[0, rows, :].astype(F32) * qscale
            k = k_ref[0, rows, :].astype(F32)
            v = v_ref[0, rows, :]
            b4 = b.reshape(cpb, GLA_CHUNK, HEAD_W)
            blast = b4[:, 0:1, :] if rev else b4[:, GLA_CHUNK - 1:GLA_CHUNK, :]
            kst = (k.reshape(cpb, GLA_CHUNK, HEAD_W) * jnp.exp(blast - b4)).astype(BF16)
            v4 = v.reshape(cpb, GLA_CHUNK, HEAD_W)
            ebl = jnp.exp(blast)
            for c in range(cpb):
                kv_sc[i * cpb + c] = _dot_tn(v4[c], kst[c])
                er_sc[i * cpb + c] = jnp.broadcast_to(ebl[c], (SUBLANES, HEAD_W))
            qin = (q * jnp.exp(b)).astype(BF16)
            qin_sc[rows, :] = qin
            return rows, b, q, k, v, qin

        def store_block(rows, scores, v):
            o_blk = _dot(scores.astype(BF16), v)
            if rev:
                oacc_sc[rows, :] += o_blk
            else:
                oacc_sc[rows, :] = o_blk

        def intra_plain(i):
            rows, b, q, k, v, qin = load_block(i)
            s = _dot_nt(qin, (k * jnp.exp(-b)).astype(BF16))
            store_block(rows, s * msk_sc[5 if rev else 4], v)

        def intra_robust(i):
            rows, b, q, k, v, _ = load_block(i)
            b4 = b.reshape(cpb, GLA_CHUNK, HEAD_W)
            r1 = b4[:, 32:33, :] if rev else b4[:, 31:32, :]
            r1 = jnp.broadcast_to(r1, b4.shape).reshape(blk, HEAD_W)
            qsel, ksel = (half0, ~half0) if rev else (~half0, half0)
            qa = q * jnp.where(qsel, jnp.exp(jnp.minimum(b - r1, 0.0)), 0.0)
            ka = k * jnp.where(ksel, jnp.exp(jnp.minimum(r1 - b, 0.0)), 0.0)
            s1 = _dot_nt(qa.astype(BF16), ka.astype(BF16))
            b8 = b.reshape(blk // 32, 32, HEAD_W)
            r2 = b8[:, 16:17, :] if rev else b8[:, 15:16, :]
            r2 = jnp.broadcast_to(r2, b8.shape).reshape(blk, HEAD_W)
            qsel, ksel = (quart0, ~quart0) if rev else (~quart0, quart0)
            qb = q * jnp.where(qsel, jnp.exp(jnp.minimum(b - r2, 0.0)), 0.0)
            kb = k * jnp.where(ksel, jnp.exp(jnp.minimum(r2 - b, 0.0)), 0.0)
            s2 = _dot_nt(qb.astype(BF16), kb.astype(BF16))
            q16 = q.reshape(ndb, GLA_DIAG, HEAD_W)
            k16 = k.reshape(ndb, GLA_DIAG, HEAD_W)
            b16 = b.reshape(ndb, GLA_DIAG, HEAD_W)
            half = GLA_DIAG // 2
            terms = []
            for jj in range(GLA_DIAG):
                bj = b16[:, jj:jj + 1, :]
                kj = k16[:, jj:jj + 1, :]
                need = ((True, jj >= half) if rev else (jj < half, True))
                parts = []
                for hsel, needed in zip((slice(0, half), slice(half, GLA_DIAG)), need):
                    if needed:
                        e = jnp.exp(jnp.minimum(b16[:, hsel, :] - bj, 0.0))
                        parts.append(q16[:, hsel, :] * kj * e)
                    else:
                        parts.append(jnp.zeros((ndb, half, HEAD_W), F32))
                terms.append(jnp.concatenate(parts, axis=1).reshape(blk, HEAD_W).astype(BF16))
            sdiag = _dot(jnp.concatenate(terms, axis=1), sel_sc[...])
            store_block(rows, s1 * msk_sc[0] + s2 * msk_sc[1] + sdiag * msk_sc[3 if rev else 2], v)

        def intra(i0, nb):
            plain_ok = jnp.min(bc_sc[d, block_rows(i0, nb), :]) >= -GLA_SAFE_DECAY

            @pl.when(plain_ok)
            def _():
                for u in range(nb):
                    intra_plain(i0 + u)

            @pl.when(jnp.logical_not(plain_ok))
            def _():
                for u in range(nb):
                    intra_robust(i0 + u)

        paired(intra)

        def inter(m, st):
            for u in range(unroll):
                n = m * unroll + u
                if rev:
                    cid = jnp.where(n < ctx_chunks, lat_chunks + (ctx_chunks - 1 - n),
                                    lat_chunks - 1 - (n - ctx_chunks))
                else:
                    cid = jnp.where(n < ctx_chunks, lat_chunks + n, n - ctx_chunks)
                rows = pl.ds(pl.multiple_of(cid * GLA_CHUNK, GLA_CHUNK), GLA_CHUNK)
                oacc_sc[rows, :] += _dot_nt(qin_sc[rows, :], st.astype(BF16))
                st = st * er_sc[cid][0:1, :] + kv_sc[cid]
            return st

        lax.fori_loop(0, nchunk // unroll, inter, jnp.zeros((HEAD_W, HEAD_W), F32))

    run_direction(0)
    run_direction(1)

    def finish(i, carry):
        rows = block_rows(i)
        y = _rms(oacc_sc[rows, :], on_ref[...])
        o_ref[0, rows, :] = (y * _silu(g_ref[0, rows, :].astype(F32))).astype(BF16)
        return carry

    lax.fori_loop(0, nblk, finish, 0)


def _gla(z, zs, fw, fb, on, n_lat, layer):
    b, nt, _ = z.shape
    cb0 = ZC_W // HEAD_W
    seq = lambda off: pl.BlockSpec((1, nt, HEAD_W), lambda bi, h: (bi, 0, cb0 + off * HEADS + h))
    kern = functools.partial(_gla_kernel, n_lat=n_lat)
    return pl.pallas_call(
        kern,
        grid=(b, HEADS),
        in_specs=[seq(0), seq(1), seq(2), seq(3),
                  pl.BlockSpec((1, nt, HEAD_W), lambda bi, h: (bi, 0, ZS_W // HEAD_W - 1)),
                  pl.BlockSpec((None, 2, 1, HEAD_W, HEAD_W), lambda bi, h: (layer, 0, h, 0, 0)),
                  pl.BlockSpec((2, 1, 1, HEAD_W), lambda bi, h: (0, h, 0, 0)),
                  pl.BlockSpec((1, HEAD_W), lambda bi, h: (0, 0))],
        out_specs=pl.BlockSpec((1, nt, HEAD_W), lambda bi, h: (bi, 0, h)),
        out_shape=jax.ShapeDtypeStruct((b, nt, MIX_W), BF16),
        scratch_shapes=[pltpu.VMEM((2, nt, HEAD_W), F32),
                        pltpu.VMEM((nt, HEAD_W), BF16),
                        pltpu.VMEM((nt // GLA_CHUNK, HEAD_W, HEAD_W), F32),
                        pltpu.VMEM((nt // GLA_CHUNK, SUBLANES, HEAD_W), F32),
                        pltpu.VMEM((nt, HEAD_W), F32),
                        pltpu.VMEM((6, SEQ_BLOCK, SEQ_BLOCK), F32),
                        pltpu.VMEM((2, SEQ_BLOCK, SEQ_BLOCK), BF16),
                        pltpu.VMEM((GLA_DIAG * HEAD_W, SEQ_BLOCK), BF16)],
        compiler_params=_params(("arbitrary", "arbitrary")),
        name="gla_mixer",
    )(z, z, z, z, zs, fw, fb, on)


def _conv_kernel(zc_ref, zp_ref, zn_ref, dw_ref, dwb_ref, lng_ref, lnb_ref, scw_ref, o_ref,
                 u_sc, m_sc, *, lat_tiles, n_tiles):
    t = pl.program_id(1)
    tl = zc_ref.shape[1]
    w = MIX_W
    first = (t == 0) | (t == lat_tiles)
    last = (t == lat_tiles - 1) | (t == n_tiles - 1)

    def glu(ref, rows):
        return ref[0, rows, 0:w].astype(F32) * _sigmoid(ref[0, rows, w:2 * w].astype(F32))

    def gated(ref, rows):
        return ref[0, rows, 3 * w:4 * w].astype(F32) * ref[0, rows, 4 * w:5 * w].astype(F32)

    halo = slice(0, HALO)
    u_sc[0, 0:HALO, :] = jnp.where(first, 0.0, glu(zp_ref, halo))
    m_sc[0:HALO, :] = jnp.where(first, 0.0, gated(zp_ref, halo))
    u_sc[0, HALO + tl:, :] = jnp.where(last, 0.0, glu(zn_ref, halo))
    m_sc[HALO + tl:, :] = jnp.where(last, 0.0, gated(zn_ref, halo))
    rc = 32
    for r in range(tl // rc):
        rows = slice(r * rc, (r + 1) * rc)
        u_sc[0, HALO + r * rc:HALO + (r + 1) * rc, :] = glu(zc_ref, rows)
        m_sc[HALO + r * rc:HALO + (r + 1) * rc, :] = gated(zc_ref, rows)
    span = tl + 2 * HALO - SUBLANES
    for s in range(1, SUBLANES):
        u_sc[s, 0:span, :] = u_sc[0, s:s + span, :]

    pad = CONF_KERNEL // 2
    for r in range(tl // rc):
        r0 = r * rc
        acc = jnp.zeros((rc, w), F32)
        for kk in range(CONF_KERNEL):
            off = HALO + r0 + kk - pad
            base = off - off % SUBLANES
            acc = acc + dw_ref[kk:kk + 1, :] * u_sc[off % SUBLANES, base:base + rc, :]
        acc = acc + dwb_ref[...]
        mu = jnp.mean(acc, axis=-1, keepdims=True)
        xc = acc - mu
        y = xc * lax.rsqrt(jnp.mean(xc * xc, axis=-1, keepdims=True) + EPS) * lng_ref[...] + lnb_ref[...]
        o_ref[0, r0:r0 + rc, 0:w] = _silu(y).astype(BF16)
        s = HALO + r0
        cv = (scw_ref[0:1, :] * m_sc[s - 1:s - 1 + rc, :] + scw_ref[1:2, :] * m_sc[s:s + rc, :]
              + scw_ref[2:3, :] * m_sc[s + 1:s + 1 + rc, :])
        o_ref[0, r0:r0 + rc, w:2 * w] = (zc_ref[0, r0:r0 + rc, 2 * w:3 * w].astype(F32) * cv).astype(BF16)


def _conv_mixers(z, dw, dwb, lng, lnb, scw, n_lat):
    b, nt, _ = z.shape
    tl = SEQ_BLOCK
    n_tiles = nt // tl
    hb = tl // HALO
    nhb = nt // HALO
    kern = functools.partial(_conv_kernel, lat_tiles=n_lat // tl, n_tiles=n_tiles)
    vec = lambda n: pl.BlockSpec((n, MIX_W), lambda bi, t: (0, 0))
    return pl.pallas_call(
        kern,
        grid=(b, n_tiles),
        in_specs=[pl.BlockSpec((1, tl, ZC_W), lambda bi, t: (bi, t, 0)),
                  pl.BlockSpec((1, HALO, ZC_W), lambda bi, t: (bi, jnp.maximum(t * hb - 1, 0), 0)),
                  pl.BlockSpec((1, HALO, ZC_W), lambda bi, t: (bi, jnp.minimum((t + 1) * hb, nhb - 1), 0)),
                  vec(CONF_KERNEL), vec(1), vec(1), vec(1), vec(SC_KERNEL)],
        out_specs=pl.BlockSpec((1, tl, 2 * MIX_W), lambda bi, t: (bi, t, 0)),
        out_shape=jax.ShapeDtypeStruct((b, nt, 2 * MIX_W), BF16),
        scratch_shapes=[pltpu.VMEM((SUBLANES, tl + 2 * HALO, MIX_W), F32),
                        pltpu.VMEM((tl + 2 * HALO, MIX_W), F32)],
        compiler_params=_params(("arbitrary", "arbitrary")),
        name="conv_mixers",
    )(z, z, z, dw, dwb, lng, lnb, scw)


def _mla_proj_kernel(zs_ref, qg_ref, kvg_ref, wq_ref, wkv_ref, cos_ref, sin_ref,
                     q_ref, k_ref, v_ref, *, rc):
    tm = zs_ref.shape[1]
    scale = (HEAD_W + MLA_ROPE) ** -0.5 * math.log2(math.e)
    lane = lax.broadcasted_iota(jnp.int32, (1, HEAD_W), 1)
    lower = (lane & 31) < 16

    for r in range(tm // rc):
        rows = slice(r * rc, (r + 1) * rc)
        cq = zs_ref[0, rows, 0:MLA_Q_RANK]
        ckv = zs_ref[0, rows, MLA_Q_RANK:MLA_Q_RANK + MLA_KV_RANK]
        krb = zs_ref[0, rows, MLA_Q_RANK + MLA_KV_RANK:ZS_W]
        q = _dot(_rms(cq, qg_ref[...]).astype(BF16), wq_ref[...])
        kv = _dot(_rms(ckv, kvg_ref[...]).astype(BF16), wkv_ref[...])
        cs = cos_ref[rows, :]
        sn = sin_ref[rows, :]

        def rope(tt):
            swapped = jnp.where(lower, pltpu.roll(tt, HEAD_W - 16, 1), pltpu.roll(tt, 16, 1))
            return tt * cs + swapped * sn

        krr = rope(krb).astype(BF16)
        for h in range(HEADS):
            c0 = 2 * HEAD_W * h
            q_ref[0, rows, c0:c0 + HEAD_W] = (q[:, c0:c0 + HEAD_W] * scale).astype(BF16)
            q_ref[0, rows, c0 + HEAD_W:c0 + 2 * HEAD_W] = (
                rope(q[:, c0 + HEAD_W:c0 + 2 * HEAD_W]) * scale).astype(BF16)
            k_ref[0, rows, c0:c0 + HEAD_W] = kv[:, c0:c0 + HEAD_W].astype(BF16)
            k_ref[0, rows, c0 + HEAD_W:c0 + 2 * HEAD_W] = krr
            v_ref[0, rows, HEAD_W * h:HEAD_W * (h + 1)] = kv[:, c0 + HEAD_W:c0 + 2 * HEAD_W].astype(BF16)


def _mla_proj(zs, qg, kvg, wq, wkv, cos_t, sin_t, layer):
    b, nt, _ = zs.shape
    tm = nt // 4
    kern = functools.partial(_mla_proj_kernel, rc=tm // 4)
    full = lambda a: pl.BlockSpec(a.shape, lambda bi, t: (0,) * a.ndim)
    stacked = lambda a: pl.BlockSpec((None,) + a.shape[1:], lambda bi, t: (layer,) + (0,) * (a.ndim - 1))
    kw = 2 * HEAD_W * HEADS
    return pl.pallas_call(
        kern,
        grid=(b, nt // tm),
        in_specs=[pl.BlockSpec((1, tm, ZS_W), lambda bi, t: (bi, t, 0)),
                  full(qg), full(kvg), stacked(wq), stacked(wkv),
                  pl.BlockSpec((tm, HEAD_W), lambda bi, t: (t, 0)),
                  pl.BlockSpec((tm, HEAD_W), lambda bi, t: (t, 0))],
        out_specs=[pl.BlockSpec((1, tm, kw), lambda bi, t: (bi, t, 0)),
                   pl.BlockSpec((1, tm, kw), lambda bi, t: (bi, t, 0)),
                   pl.BlockSpec((1, tm, MIX_W), lambda bi, t: (bi, t, 0))],
        out_shape=[jax.ShapeDtypeStruct((b, nt, kw), BF16),
                   jax.ShapeDtypeStruct((b, nt, kw), BF16),
                   jax.ShapeDtypeStruct((b, nt, MIX_W), BF16)],
        compiler_params=_params(("arbitrary", "arbitrary")),
        name="mla_proj",
    )(zs, qg, kvg, wq, wkv, cos_t, sin_t)


def _attn_kernel(q_ref, k_ref, v_ref, o_ref, *, sub):
    for r0 in range(0, q_ref.shape[1], sub):
        rows = slice(r0, r0 + sub)
        s = _dot_nt(q_ref[0, rows, :], k_ref[0])
        p = jnp.exp2(s - jnp.max(s, axis=-1, keepdims=True))
        denom = jnp.sum(p, axis=-1, keepdims=True)
        o_ref[0, rows, :] = (_dot(p.astype(BF16), v_ref[0]) / denom).astype(BF16)


def _attention(q, k, v, n_lat, need_ctx):
    b, nt, _ = q.shape
    kern = functools.partial(_attn_kernel, sub=Q_SUB)
    out = pl.pallas_call(
        kern,
        grid=(b, HEADS, n_lat // Q_BLOCK),
        in_specs=[pl.BlockSpec((1, Q_BLOCK, 2 * HEAD_W), lambda bi, h, qi: (bi, qi, h)),
                  pl.BlockSpec((1, nt, 2 * HEAD_W), lambda bi, h, qi: (bi, 0, h)),
                  pl.BlockSpec((1, nt, HEAD_W), lambda bi, h, qi: (bi, 0, h))],
        out_specs=pl.BlockSpec((1, Q_BLOCK, HEAD_W), lambda bi, h, qi: (bi, qi, h)),
        out_shape=jax.ShapeDtypeStruct((b, n_lat, MIX_W), BF16),
        compiler_params=_params(("arbitrary", "arbitrary", "arbitrary")),
        name="mla_attention",
    )(q, k, v)
    if not need_ctx:
        return out
    n_ctx = nt - n_lat
    cblk = n_lat // n_ctx
    ctx_spec = lambda wd: pl.BlockSpec((1, n_ctx, wd), lambda bi, h: (bi, cblk, h))
    out_ctx = pl.pallas_call(
        kern,
        grid=(b, HEADS),
        in_specs=[ctx_spec(2 * HEAD_W), ctx_spec(2 * HEAD_W), ctx_spec(HEAD_W)],
        out_specs=pl.BlockSpec((1, n_ctx, HEAD_W), lambda bi, h: (bi, 0, h)),
        out_shape=jax.ShapeDtypeStruct((b, n_ctx, MIX_W), BF16),
        compiler_params=_params(("arbitrary", "arbitrary")),
        name="mla_attention_ctx",
    )(q, k, v)
    return jnp.concatenate([out, out_ctx], axis=1)


def _wout_kernel(gla_ref, cs_ref, mla_ref, x_ref, ml_ref, mc_ref, g_ref, w_ref, xo_ref, h_ref,
                 *, n_lat, rc):
    t = pl.program_id(1)
    tm = x_ref.shape[1]
    for r in range(tm // rc):
        rows = slice(r * rc, (r + 1) * rc)
        mix = jnp.concatenate([gla_ref[0, rows, :], cs_ref[0, rows, :], mla_ref[0, rows, :]], axis=1)
        o = _dot(mix, w_ref[...])
        rowid = t * tm + r * rc + lax.broadcasted_iota(jnp.int32, (rc, 1), 0)
        is_ctx = rowid >= n_lat
        g1 = jnp.where(is_ctx, mc_ref[0, 2:3, :], ml_ref[0, 2:3, :])
        sh = jnp.where(is_ctx, mc_ref[0, 3:4, :], ml_ref[0, 3:4, :])
        sc = jnp.where(is_ctx, mc_ref[0, 4:5, :], ml_ref[0, 4:5, :])
        xn = x_ref[0, rows, :] + g1 * o
        xo_ref[0, rows, :] = xn
        h_ref[0, rows, :] = (_rms(xn, g_ref[...]) * (1.0 + sc) + sh).astype(BF16)


def _wout(gla, cs, mla, xx, mod, g, w, n_lat, n_rows, layer):
    b, _, d = xx.shape
    tm = n_rows // 8
    kern = functools.partial(_wout_kernel, n_lat=n_lat, rc=tm // 2)
    tile = lambda wd: pl.BlockSpec((1, tm, wd), lambda bi, t: (bi, t, 0))
    ml, mc = _mod_specs(layer, b, d)
    return pl.pallas_call(
        kern,
        grid=(b, n_rows // tm),
        in_specs=[tile(MIX_W), tile(2 * MIX_W), tile(MIX_W), tile(d), ml, mc,
                  pl.BlockSpec((1, d), lambda bi, t: (0, 0)),
                  pl.BlockSpec((None,) + w.shape[1:], lambda bi, t: (layer, 0, 0))],
        out_specs=[tile(d), tile(d)],
        out_shape=[jax.ShapeDtypeStruct((b, n_rows, d), F32),
                   jax.ShapeDtypeStruct((b, n_rows, d), BF16)],
        compiler_params=_params(("arbitrary", "arbitrary")),
        name="w_out_norm2",
    )(gla, cs, mla, xx, mod, mod, g, w)


def _ffn_up_kernel(h_ref, w1_ref, w3_ref, u_ref, *, rc):
    tm = h_ref.shape[1]
    w1 = w1_ref[...].astype(BF16)
    w3 = w3_ref[...].astype(BF16)
    for r in range(tm // rc):
        rows = slice(r * rc, (r + 1) * rc)
        hh = h_ref[0, rows, :]
        u_ref[0, rows, :] = (_silu(_dot(hh, w1)) * _dot(hh, w3)).astype(BF16)


def _ffn_up(h2, w1, w3, layer):
    b, n_rows, d = h2.shape
    tm = n_rows // 4
    tf = 512
    dff = w1.shape[2]
    kern = functools.partial(_ffn_up_kernel, rc=tm // 4)
    wspec = pl.BlockSpec((None, d, tf), lambda bi, t, j: (layer, 0, j))
    return pl.pallas_call(
        kern,
        grid=(b, n_rows // tm, dff // tf),
        in_specs=[pl.BlockSpec((1, tm, d), lambda bi, t, j: (bi, t, 0)), wspec, wspec],
        out_specs=pl.BlockSpec((1, tm, tf), lambda bi, t, j: (bi, t, j)),
        out_shape=jax.ShapeDtypeStruct((b, n_rows, dff), BF16),
        compiler_params=_params(("arbitrary", "arbitrary", "arbitrary")),
        name="ffn_up",
    )(h2, w1, w3)


def _ffn_down_kernel(u_ref, x_ref, ml_ref, mc_ref, w2_ref, fg_ref, o_ref, *, n_lat, rc, final):
    t = pl.program_id(1)
    j = pl.program_id(2)
    tm = u_ref.shape[1]
    tn = w2_ref.shape[1]
    cols = pl.ds(pl.multiple_of(j * tn, tn), tn)
    xo_cols = cols if final else slice(None)
    for r in range(tm // rc):
        rows = slice(r * rc, (r + 1) * rc)
        rowid = t * tm + r * rc + lax.broadcasted_iota(jnp.int32, (rc, 1), 0)
        g2 = jnp.where(rowid >= n_lat, mc_ref[0, 5:6, cols], ml_ref[0, 5:6, cols])
        o_ref[0, rows, xo_cols] = x_ref[0, rows, xo_cols] + g2 * _dot(u_ref[0, rows, :], w2_ref[...])

    if final:
        @pl.when(j == pl.num_programs(2) - 1)
        def _():
            for r in range(tm // rc):
                rows = slice(r * rc, (r + 1) * rc)
                o_ref[0, rows, :] = _rms(o_ref[0, rows, :], fg_ref[...])


def _ffn_down(u, xn, mod, w2, fg, n_lat, final, layer):
    b, n_rows, d = xn.shape
    tn = 512
    dff = w2.shape[1]
    if final:
        tm = n_rows // 8
        xo_spec = pl.BlockSpec((1, tm, d), lambda bi, t, j: (bi, t, 0))
    else:
        tm = n_rows // 4
        xo_spec = pl.BlockSpec((1, tm, tn), lambda bi, t, j: (bi, t, j))
    kern = functools.partial(_ffn_down_kernel, n_lat=n_lat, rc=tm // (2 if final else 4), final=final)
    ml, mc = _mod_specs(layer, b, d)
    return pl.pallas_call(
        kern,
        grid=(b, n_rows // tm, d // tn),
        in_specs=[pl.BlockSpec((1, tm, dff), lambda bi, t, j: (bi, t, 0)),
                  xo_spec, ml, mc,
                  pl.BlockSpec((None, dff, tn), lambda bi, t, j: (layer, 0, j)),
                  pl.BlockSpec((1, d), lambda bi, t, j: (0, 0))],
        out_specs=xo_spec,
        out_shape=jax.ShapeDtypeStruct((b, n_rows, d), F32),
        compiler_params=_params(("arbitrary", "arbitrary", "arbitrary")),
        name="ffn_down",
    )(u, xn, mod, mod, w2, fg)


def _rope_tables(n_lat, n_ctx):
    rows = n_lat // GRID_W
    rowp = jnp.repeat(jnp.arange(rows, dtype=F32), GRID_W)
    colp = jnp.tile(jnp.arange(GRID_W, dtype=F32), rows)
    inv = ROPE_BASE ** (-jnp.arange(ROPE_FREQS, dtype=F32) * 2.0 / (2 * ROPE_FREQS))
    ar, ac = rowp[:, None] * inv, colp[:, None] * inv
    zeros = jnp.zeros((n_lat, HEAD_W - MLA_ROPE), F32)
    cos_l = jnp.concatenate([jnp.cos(ar), jnp.cos(ar), jnp.cos(ac), jnp.cos(ac), zeros], axis=1)
    sin_l = jnp.concatenate([-jnp.sin(ar), jnp.sin(ar), -jnp.sin(ac), jnp.sin(ac), zeros], axis=1)
    cos_c = jnp.concatenate([jnp.ones((n_ctx, MLA_ROPE), F32), jnp.zeros((n_ctx, HEAD_W - MLA_ROPE), F32)], axis=1)
    return jnp.concatenate([cos_l, cos_c], axis=0), jnp.concatenate([sin_l, jnp.zeros_like(cos_c)], axis=0)


def kernel(x, c, ctx, c_ctx, norm1_g, w_mod, b_mod, w_in, gla_fg_up, gla_fg_b, gla_onorm_g, conf_dw, conf_dw_b, conf_ln_g, conf_ln_b, sc_dw, mla_q_norm_g, mla_kv_norm_g, mla_w_uq, mla_w_ukv, w_out, norm2_g, ffn_w1, ffn_w3, ffn_w2, final_norm_g):
    bsz, n_lat, d = x.shape
    n_ctx = ctx.shape[1]
    depth = w_in.shape[0]
    w = MIX_W

    xx = jnp.concatenate([x, ctx], axis=1)
    c3 = jnp.concatenate([c, c_ctx[None, :]], axis=0)
    mod = _modulation(jnp.broadcast_to(c3[:, :, None], (bsz + 1, d, 128)), w_mod, b_mod)
    mod = mod.reshape(depth, MOD_ROWS, 6, d)
    cos_t, sin_t = _rope_tables(n_lat, n_ctx)

    o_gla, o_lr, o_conv, o_mla = 0, 4 * w, 4 * w + 2 * GLA_GATE_RANK, 4 * w + 2 * GLA_GATE_RANK + 5 * w
    w_main = jnp.concatenate([w_in[:, :, o_conv:o_mla], w_in[:, :, o_gla:o_lr]], axis=-1).astype(BF16)
    w_small = jnp.concatenate([w_in[:, :, o_mla:], w_in[:, :, o_lr:o_conv],
                               jnp.zeros((depth, d, ZS_W - (w_in.shape[2] - o_mla) - 2 * GLA_GATE_RANK), F32)],
                              axis=-1).astype(BF16)
    lr0 = MLA_ROPE
    fw = jnp.stack([jnp.pad(gla_fg_up[:, dd], ((0, 0), (lr0 + dd * GLA_GATE_RANK,
                                                        HEAD_W - lr0 - (dd + 1) * GLA_GATE_RANK), (0, 0)))
                    for dd in range(2)], axis=1)
    fw = fw.reshape(depth, 2, HEAD_W, HEADS, HEAD_W).transpose(0, 1, 3, 2, 4)
    fb = gla_fg_b.reshape(depth, 2, HEADS, 1, HEAD_W)
    wq = mla_w_uq.reshape(depth, MLA_Q_RANK, HEADS, HEAD_W + MLA_ROPE)
    wq = jnp.pad(wq, ((0, 0), (0, 0), (0, 0), (0, HEAD_W - MLA_ROPE))).reshape(depth, MLA_Q_RANK, -1).astype(BF16)
    wkv = mla_w_ukv.astype(BF16)
    wo = w_out.astype(BF16)
    w2 = ffn_w2.astype(BF16)
    row = lambda a: a.reshape(1, -1)

    for i in range(depth):
        last = i == depth - 1
        z, zs = _win(xx, mod, row(norm1_g[i]), w_main, w_small, n_lat, i)
        gla = _gla(z, zs, fw, fb[i], row(gla_onorm_g[i]), n_lat, i)
        cs = _conv_mixers(z, conf_dw[i], row(conf_dw_b[i]), row(conf_ln_g[i]), row(conf_ln_b[i]), sc_dw[i], n_lat)
        qq, kk, vv = _mla_proj(zs, row(mla_q_norm_g[i]), row(mla_kv_norm_g[i]), wq, wkv, cos_t, sin_t, i)
        mla = _attention(qq, kk, vv, n_lat, not last)
        n_rows = n_lat if last else n_lat + n_ctx
        xn, h2 = _wout(gla, cs, mla, xx, mod, row(norm2_g[i]), wo, n_lat, n_rows, i)
        u = _ffn_up(h2, ffn_w1, ffn_w3, i)
        xx = _ffn_down(u, xn, mod, w2, row(final_norm_g), n_lat, last, i)
    return xx
```

```python
import functools
import math

import jax
import jax.numpy as jnp
from jax import lax
from jax.experimental import pallas as pl
from jax.experimental.pallas import tpu as pltpu

F32 = jnp.float32
BF16 = jnp.bfloat16

EPS = 1e-6
GRID_W = 64
HEADS = 4
HEAD_W = 128
GLA_GATE_RANK = 16
GLA_GATE_NORM = 16.0
GLA_CHUNK = 64
GLA_DIAG = 16
GLA_SAFE_DECAY = 60.0
CONF_KERNEL = 31
SC_KERNEL = 3
MIX_W = 512
MLA_ROPE = 64
MLA_Q_RANK = 384
MLA_KV_RANK = 128
ROPE_FREQS = 16
ROPE_BASE = 10000.0
Q_BLOCK = 1024
Q_SUB = 256
SEQ_BLOCK = 256
HALO = 16
SUBLANES = 8
ZC_W = 5 * MIX_W
ZG_W = 4 * MIX_W
MOD_ROWS = 8
ZS_W = 640
VMEM_LIMIT_V7X = 56 * 1024 * 1024


def _dot(a, b):
    return jnp.dot(a, b, preferred_element_type=F32)


def _dot_nt(a, b):
    return lax.dot_general(a, b, (((1,), (1,)), ((), ())), preferred_element_type=F32)


def _dot_tn(a, b):
    return lax.dot_general(a, b, (((0,), (0,)), ((), ())), preferred_element_type=F32)


def _split3(x):
    h1 = x.astype(BF16)
    r1 = x - h1.astype(F32)
    h2 = r1.astype(BF16)
    h3 = (r1 - h2.astype(F32)).astype(BF16)
    return h1, h2, h3


def _sigmoid(x):
    return 1.0 / (1.0 + jnp.exp(-x))


def _silu(x):
    return x * _sigmoid(x)


def _rms(x, g):
    return x * lax.rsqrt(jnp.mean(x * x, axis=-1, keepdims=True) + EPS) * g


def _params(sem):
    return pltpu.CompilerParams(dimension_semantics=sem, vmem_limit_bytes=VMEM_LIMIT_V7X)


def _mod_kernel(c_ref, w_ref, b_ref, o_ref, act_sc):
    tn = w_ref.shape[2]
    rows = []
    nrow = c_ref.shape[0]

    @pl.when((pl.program_id(0) == 0) & (pl.program_id(1) == 0))
    def _():
        for m in range(nrow):
            act_sc[m] = _silu(c_ref[m])

    kc = 8 * SUBLANES
    ncb = tn // 128
    acc = [[jnp.zeros((SUBLANES, 128), F32) for _ in range(ncb)] for _ in range(nrow)]
    for k0 in range(0, w_ref.shape[1], kc):
        wk = w_ref[0, k0:k0 + kc, :]
        for m in range(nrow):
            am = act_sc[m, k0:k0 + kc, :]
            for cb in range(ncb):
                part = wk[:, cb * 128:(cb + 1) * 128] * am
                acc[m][cb] = acc[m][cb] + jnp.sum(part.reshape(kc // SUBLANES, SUBLANES, 128), axis=0)
    for m in range(nrow):
        out_m = jnp.concatenate([jnp.sum(a, axis=0, keepdims=True) for a in acc[m]], axis=1)
        rows.append(out_m + b_ref[0])
    rows.append(jnp.zeros((MOD_ROWS - nrow, tn), F32))
    o_ref[0] = jnp.concatenate(rows, axis=0)


def _modulation(cb, w_mod, b_mod):
    depth, d, n6 = w_mod.shape
    tn = 1024
    return pl.pallas_call(
        _mod_kernel,
        grid=(depth, n6 // tn),
        in_specs=[pl.BlockSpec(cb.shape, lambda i, j: (0, 0, 0)),
                  pl.BlockSpec((1, d, tn), lambda i, j: (i, 0, j)),
                  pl.BlockSpec((1, 1, tn), lambda i, j: (i, 0, j))],
        out_specs=pl.BlockSpec((1, MOD_ROWS, tn), lambda i, j: (i, 0, j)),
        out_shape=jax.ShapeDtypeStruct((depth, MOD_ROWS, n6), F32),
        scratch_shapes=[pltpu.VMEM(cb.shape, F32)],
        compiler_params=_params(("arbitrary", "arbitrary")),
        name="modulation",
    )(cb, w_mod, b_mod.reshape(depth, 1, n6))


def _mod_specs(layer, b, d):
    lat = lambda bi, *_: (layer, bi, 0, 0)
    ctx = lambda *_: (layer, b, 0, 0)
    return (pl.BlockSpec((None, 1, 6, d), lat), pl.BlockSpec((None, 1, 6, d), ctx))


def _win_kernel(x_ref, ml_ref, mc_ref, g_ref, w_ref, ws_ref, qg_ref, kvg_ref, wq_ref, wkv_ref,
                cos_ref, sin_ref, z_ref, zl_ref, q_ref, k_ref, v_ref, *, n_lat, rc, nc):
    t = pl.program_id(1)
    tm = x_ref.shape[1]
    nw = w_ref.shape[1]
    scale = (HEAD_W + MLA_ROPE) ** -0.5 * math.log2(math.e)
    lane = lax.broadcasted_iota(jnp.int32, (1, HEAD_W), 1)
    lower = (lane & 31) < 16
    for r in range(tm // rc):
        rows = slice(r * rc, (r + 1) * rc)
        y = _rms(x_ref[0, rows, :], g_ref[...])
        rowid = t * tm + r * rc + lax.broadcasted_iota(jnp.int32, (rc, 1), 0)
        is_ctx = rowid >= n_lat
        sh = jnp.where(is_ctx, mc_ref[0, 0:1, :], ml_ref[0, 0:1, :])
        sc = jnp.where(is_ctx, mc_ref[0, 1:2, :], ml_ref[0, 1:2, :])
        h = (y * (1.0 + sc) + sh).astype(BF16)
        zs = _dot(h, ws_ref[...])
        for n0 in range(0, nw, nc):
            z_ref[0, rows, n0:n0 + nc] = _dot(h, w_ref[:, n0:n0 + nc]).astype(BF16)

        krb = zs[:, MLA_Q_RANK + MLA_KV_RANK:ZS_W]
        zl_ref[0, rows, :] = krb
        q = _dot(_rms(zs[:, 0:MLA_Q_RANK], qg_ref[...]).astype(BF16), wq_ref[...])
        kv = _dot(_rms(zs[:, MLA_Q_RANK:MLA_Q_RANK + MLA_KV_RANK], kvg_ref[...]).astype(BF16), wkv_ref[...])
        cs = cos_ref[rows, :]
        sn = sin_ref[rows, :]

        def rope(tt):
            swapped = jnp.where(lower, pltpu.roll(tt, HEAD_W - 16, 1), pltpu.roll(tt, 16, 1))
            return tt * cs + swapped * sn

        krr = rope(krb).astype(BF16)
        for hd in range(HEADS):
            c0 = 2 * HEAD_W * hd
            q_ref[0, rows, c0:c0 + HEAD_W] = (q[:, c0:c0 + HEAD_W] * scale).astype(BF16)
            q_ref[0, rows, c0 + HEAD_W:c0 + 2 * HEAD_W] = (
                rope(q[:, c0 + HEAD_W:c0 + 2 * HEAD_W]) * scale).astype(BF16)
            k_ref[0, rows, c0:c0 + HEAD_W] = kv[:, c0:c0 + HEAD_W].astype(BF16)
            k_ref[0, rows, c0 + HEAD_W:c0 + 2 * HEAD_W] = krr
            v_ref[0, rows, HEAD_W * hd:HEAD_W * (hd + 1)] = kv[:, c0 + HEAD_W:c0 + 2 * HEAD_W].astype(BF16)


def _win(xx, mod, g, w_main, w_small, qg, kvg, wq, wkv, cos_t, sin_t, n_lat, layer):
    b, nt, d = xx.shape
    tm = nt // 8
    nw = w_main.shape[2]
    kw = 2 * HEAD_W * HEADS
    kern = functools.partial(_win_kernel, n_lat=n_lat, rc=tm // 2, nc=nw // 3)
    ml, mc = _mod_specs(layer, b, d)
    resident = lambda a: pl.BlockSpec((None,) + a.shape[1:], lambda bi, t: (layer, 0, 0),
                                      pipeline_mode=pl.Buffered(1))
    vec = lambda a: pl.BlockSpec(a.shape, lambda bi, t: (0, 0))
    tile = lambda wd: pl.BlockSpec((1, tm, wd), lambda bi, t: (bi, t, 0))
    return pl.pallas_call(
        kern,
        grid=(b, nt // tm),
        in_specs=[tile(d), ml, mc, vec(g), resident(w_main), resident(w_small),
                  vec(qg), vec(kvg), resident(wq), resident(wkv),
                  pl.BlockSpec((tm, HEAD_W), lambda bi, t: (t, 0)),
                  pl.BlockSpec((tm, HEAD_W), lambda bi, t: (t, 0))],
        out_specs=[tile(nw), tile(HEAD_W), tile(kw), tile(kw), tile(MIX_W)],
        out_shape=[jax.ShapeDtypeStruct((b, nt, nw), BF16),
                   jax.ShapeDtypeStruct((b, nt, HEAD_W), F32),
                   jax.ShapeDtypeStruct((b, nt, kw), BF16),
                   jax.ShapeDtypeStruct((b, nt, kw), BF16),
                   jax.ShapeDtypeStruct((b, nt, MIX_W), BF16)],
        compiler_params=_params(("arbitrary", "arbitrary")),
        name="norm1_w_in",
    )(xx, mod, mod, g, w_main, w_small, qg, kvg, wq, wkv, cos_t, sin_t)


def _gla_kernel(q_ref, k_ref, v_ref, g_ref, zs_ref, fw_ref, fb_ref, on_ref, o_ref,
                bc_sc, qin_sc, kv_sc, er_sc, oacc_sc, oint_sc, msk_sc, tri_sc, sel_sc, *, n_lat):
    n_tot = q_ref.shape[1]
    blk = SEQ_BLOCK
    cpb = blk // GLA_CHUNK
    nblk = n_tot // blk
    nchunk = n_tot // GLA_CHUNK
    lat_chunks = n_lat // GLA_CHUNK
    ctx_chunks = nchunk - lat_chunks
    ndb = blk // GLA_DIAG
    qscale = HEAD_W ** -0.5
    unroll = max(u for u in range(1, 18) if nchunk % u == 0)
    group_pro, group_intra = 8, 4

    ri = lax.broadcasted_iota(jnp.int32, (blk, blk), 0)
    ci = lax.broadcasted_iota(jnp.int32, (blk, blk), 1)
    same64 = (ri >> 6) == (ci >> 6)
    same16 = (ri >> 4) == (ci >> 4)
    one = lambda m: jnp.where(m, 1.0, 0.0).astype(F32)
    msk_sc[0] = one(same64)
    msk_sc[1] = one((ri >> 5) == (ci >> 5))
    msk_sc[2] = one(same16 & ((ri & 15) >= (ci & 15)))
    msk_sc[3] = one(same16 & ((ri & 15) <= (ci & 15)))
    msk_sc[4] = one(same64 & (ri >= ci))
    msk_sc[5] = one(same64 & (ri <= ci))
    tri_sc[0] = msk_sc[4].astype(BF16)
    tri_sc[1] = msk_sc[5].astype(BF16)
    cj = lax.broadcasted_iota(jnp.int32, (HEAD_W, blk), 1) & (GLA_DIAG - 1)
    for jj in range(GLA_DIAG):
        sel_sc[jj * HEAD_W:(jj + 1) * HEAD_W, :] = one(cj == jj).astype(BF16)

    row = lax.broadcasted_iota(jnp.int32, (blk, 1), 0)
    half0 = (row & 63) < 32
    quart0 = (row & 31) < 16

    def block_rows(i, nb=1):
        start = i * blk if isinstance(i, int) else pl.multiple_of(i * blk, blk)
        return pl.ds(start, nb * blk)

    def paired(fn, group):
        def body(p, carry):
            fn(group * p, group)
            return carry
        lax.fori_loop(0, nblk // group, body, 0)
        if nblk % group:
            fn(nblk - nblk % group, nblk % group)

    fsplit = [_split3(fw_ref[d, 0]) for d in range(2)]
    fcat = jnp.concatenate([jnp.concatenate([fh, fh, fl], axis=0) for fh, fl, _ in fsplit], axis=1)

    def prologue(i0, nb):
        for u in range(nb):
            rows = block_rows(i0 + u)
            zh, zl, _ = _split3(zs_ref[0, rows, :])
            xg2 = _dot(jnp.concatenate([zh, zl, zh], axis=1), fcat)
            for d in range(2):
                xg = xg2[:, d * HEAD_W:(d + 1) * HEAD_W] + fb_ref[d, 0]
                logd = (jnp.minimum(xg, 0.0) - jnp.log1p(jnp.exp(-jnp.abs(xg)))) * (1.0 / GLA_GATE_NORM)
                c3 = _dot(tri_sc[d], jnp.concatenate(_split3(logd), axis=1))
                bc_sc[d, rows, :] = c3[:, 0:HEAD_W] + c3[:, HEAD_W:2 * HEAD_W] + c3[:, 2 * HEAD_W:]

    paired(prologue, group_pro)

    def run_direction(d):
        rev = d == 1

        def load_block(i):
            rows = block_rows(i)
            b = bc_sc[d, rows, :]
            q = q_ref[0, rows, :].astype(F32) * qscale
            k = k_ref[0, rows, :].astype(F32)
            v = v_ref[0, rows, :]
            b4 = b.reshape(cpb, GLA_CHUNK, HEAD_W)
            blast = b4[:, 0:1, :] if rev else b4[:, GLA_CHUNK - 1:GLA_CHUNK, :]
            kst = (k.reshape(cpb, GLA_CHUNK, HEAD_W) * jnp.exp(blast - b4)).astype(BF16)
            v4 = v.reshape(cpb, GLA_CHUNK, HEAD_W)
            ebl = jnp.exp(blast)
            for c in range(cpb):
                kv_sc[i * cpb + c] = _dot_tn(v4[c], kst[c])
                er_sc[i * cpb + c] = jnp.broadcast_to(ebl[c], (SUBLANES, HEAD_W))
            qin = (q * jnp.exp(b)).astype(BF16)
            qin_sc[rows, :] = qin
            return rows, b, q, k, v, qin

        def store_block(rows, scores, v):
            o_blk = _dot(scores.astype(BF16), v)
            if rev:
                oacc_sc[rows, :] += o_blk
            else:
                oacc_sc[rows, :] = o_blk

        def intra_plain(i):
            rows, b, q, k, v, qin = load_block(i)
            s = _dot_nt(qin, (k * jnp.exp(-b)).astype(BF16))
            store_block(rows, s * msk_sc[5 if rev else 4], v)

        def intra_robust(i):
            rows, b, q, k, v, _ = load_block(i)
            b4 = b.reshape(cpb, GLA_CHUNK, HEAD_W)
            r1 = b4[:, 32:33, :] if rev else b4[:, 31:32, :]
            r1 = jnp.broadcast_to(r1, b4.shape).reshape(blk, HEAD_W)
            qsel, ksel = (half0, ~half0) if rev else (~half0, half0)
            qa = q * jnp.where(qsel, jnp.exp(jnp.minimum(b - r1, 0.0)), 0.0)
            ka = k * jnp.where(ksel, jnp.exp(jnp.minimum(r1 - b, 0.0)), 0.0)
            s1 = _dot_nt(qa.astype(BF16), ka.astype(BF16))
            b8 = b.reshape(blk // 32, 32, HEAD_W)
            r2 = b8[:, 16:17, :] if rev else b8[:, 15:16, :]
            r2 = jnp.broadcast_to(r2, b8.shape).reshape(blk, HEAD_W)
            qsel, ksel = (quart0, ~quart0) if rev else (~quart0, quart0)
            qb = q * jnp.where(qsel, jnp.exp(jnp.minimum(b - r2, 0.0)), 0.0)
            kb = k * jnp.where(ksel, jnp.exp(jnp.minimum(r2 - b, 0.0)), 0.0)
            s2 = _dot_nt(qb.astype(BF16), kb.astype(BF16))
            q16 = q.reshape(ndb, GLA_DIAG, HEAD_W)
            k16 = k.reshape(ndb, GLA_DIAG, HEAD_W)
            b16 = b.reshape(ndb, GLA_DIAG, HEAD_W)
            half = GLA_DIAG // 2
            terms = []
            for jj in range(GLA_DIAG):
                bj = b16[:, jj:jj + 1, :]
                kj = k16[:, jj:jj + 1, :]
                need = ((True, jj >= half) if rev else (jj < half, True))
                parts = []
                for hsel, needed in zip((slice(0, half), slice(half, GLA_DIAG)), need):
                    if needed:
                        e = jnp.exp(jnp.minimum(b16[:, hsel, :] - bj, 0.0))
                        parts.append(q16[:, hsel, :] * kj * e)
                    else:
                        parts.append(jnp.zeros((ndb, half, HEAD_W), F32))
                terms.append(jnp.concatenate(parts, axis=1).reshape(blk, HEAD_W).astype(BF16))
            sdiag = _dot(jnp.concatenate(terms, axis=1), sel_sc[...])
            store_block(rows, s1 * msk_sc[0] + s2 * msk_sc[1] + sdiag * msk_sc[3 if rev else 2], v)

        def intra(i0, nb):
            plain_ok = jnp.min(bc_sc[d, block_rows(i0, nb), :]) >= -GLA_SAFE_DECAY

            @pl.when(plain_ok)
            def _():
                for u in range(nb):
                    intra_plain(i0 + u)

            @pl.when(jnp.logical_not(plain_ok))
            def _():
                for u in range(nb):
                    intra_robust(i0 + u)

        paired(intra, group_intra)

        def inter(m, st):
            for u in range(unroll):
                n = m * unroll + u
                if rev:
                    cid = jnp.where(n < ctx_chunks, lat_chunks + (ctx_chunks - 1 - n),
                                    lat_chunks - 1 - (n - ctx_chunks))
                else:
                    cid = jnp.where(n < ctx_chunks, lat_chunks + n, n - ctx_chunks)
                rows = pl.ds(pl.multiple_of(cid * GLA_CHUNK, GLA_CHUNK), GLA_CHUNK)
                oint_sc[d, rows, :] = _dot_nt(qin_sc[rows, :], st.astype(BF16))
                st = st * er_sc[cid][0:1, :] + kv_sc[cid]
            return st

        lax.fori_loop(0, nchunk // unroll, inter, jnp.zeros((HEAD_W, HEAD_W), F32))

    run_direction(0)
    run_direction(1)

    def finish(i, carry):
        rows = block_rows(i)
        y = _rms(oacc_sc[rows, :] + oint_sc[0, rows, :] + oint_sc[1, rows, :], on_ref[...])
        o_ref[0, rows, :] = (y * _silu(g_ref[0, rows, :].astype(F32))).astype(BF16)
        return carry

    lax.fori_loop(0, nblk, finish, 0)


def _gla(z, zs, fw, fb, on, n_lat, layer):
    b, nt, _ = z.shape
    cb0 = ZC_W // HEAD_W
    seq = lambda off: pl.BlockSpec((1, nt, HEAD_W), lambda bi, h: (bi, 0, cb0 + off * HEADS + h))
    kern = functools.partial(_gla_kernel, n_lat=n_lat)
    return pl.pallas_call(
        kern,
        grid=(b, HEADS),
        in_specs=[seq(0), seq(1), seq(2), seq(3),
                  pl.BlockSpec((1, nt, HEAD_W), lambda bi, h: (bi, 0, 0)),
                  pl.BlockSpec((None, 2, 1, HEAD_W, HEAD_W), lambda bi, h: (layer, 0, h, 0, 0)),
                  pl.BlockSpec((2, 1, 1, HEAD_W), lambda bi, h: (0, h, 0, 0)),
                  pl.BlockSpec((1, HEAD_W), lambda bi, h: (0, 0))],
        out_specs=pl.BlockSpec((1, nt, HEAD_W), lambda bi, h: (bi, 0, h)),
        out_shape=jax.ShapeDtypeStruct((b, nt, MIX_W), BF16),
        scratch_shapes=[pltpu.VMEM((2, nt, HEAD_W), F32),
                        pltpu.VMEM((nt, HEAD_W), BF16),
                        pltpu.VMEM((nt // GLA_CHUNK, HEAD_W, HEAD_W), F32),
                        pltpu.VMEM((nt // GLA_CHUNK, SUBLANES, HEAD_W), F32),
                        pltpu.VMEM((nt, HEAD_W), F32),
                        pltpu.VMEM((2, nt, HEAD_W), F32),
                        pltpu.VMEM((6, SEQ_BLOCK, SEQ_BLOCK), F32),
                        pltpu.VMEM((2, SEQ_BLOCK, SEQ_BLOCK), BF16),
                        pltpu.VMEM((GLA_DIAG * HEAD_W, SEQ_BLOCK), BF16)],
        compiler_params=_params(("arbitrary", "arbitrary")),
        name="gla_mixer",
    )(z, z, z, z, zs, fw, fb, on)


def _conv_kernel(zc_ref, zp_ref, zn_ref, dw_ref, dwb_ref, lng_ref, lnb_ref, scw_ref, o_ref,
                 u_sc, m_sc, *, lat_tiles, n_tiles):
    t = pl.program_id(1)
    tl = zc_ref.shape[1]
    w = MIX_W
    first = (t == 0) | (t == lat_tiles)
    last = (t == lat_tiles - 1) | (t == n_tiles - 1)

    def glu(ref, rows):
        return ref[0, rows, 0:w].astype(F32) * _sigmoid(ref[0, rows, w:2 * w].astype(F32))

    def gated(ref, rows):
        return ref[0, rows, 3 * w:4 * w].astype(F32) * ref[0, rows, 4 * w:5 * w].astype(F32)

    halo = slice(0, HALO)
    u_sc[0, 0:HALO, :] = jnp.where(first, 0.0, glu(zp_ref, halo))
    m_sc[0:HALO, :] = jnp.where(first, 0.0, gated(zp_ref, halo))
    u_sc[0, HALO + tl:, :] = jnp.where(last, 0.0, glu(zn_ref, halo))
    m_sc[HALO + tl:, :] = jnp.where(last, 0.0, gated(zn_ref, halo))
    rc = 32
    for r in range(tl // rc):
        rows = slice(r * rc, (r + 1) * rc)
        u_sc[0, HALO + r * rc:HALO + (r + 1) * rc, :] = glu(zc_ref, rows)
        m_sc[HALO + r * rc:HALO + (r + 1) * rc, :] = gated(zc_ref, rows)
    span = tl + 2 * HALO - SUBLANES
    for s in range(1, SUBLANES):
        u_sc[s, 0:span, :] = u_sc[0, s:s + span, :]

    pad = CONF_KERNEL // 2
    for r in range(tl // rc):
        r0 = r * rc
        acc = jnp.zeros((rc, w), F32)
        for kk in range(CONF_KERNEL):
            off = HALO + r0 + kk - pad
            base = off - off % SUBLANES
            acc = acc + dw_ref[kk:kk + 1, :] * u_sc[off % SUBLANES, base:base + rc, :]
        acc = acc + dwb_ref[...]
        mu = jnp.mean(acc, axis=-1, keepdims=True)
        xc = acc - mu
        y = xc * lax.rsqrt(jnp.mean(xc * xc, axis=-1, keepdims=True) + EPS) * lng_ref[...] + lnb_ref[...]
        o_ref[0, r0:r0 + rc, 0:w] = _silu(y).astype(BF16)
        s = HALO + r0
        cv = (scw_ref[0:1, :] * m_sc[s - 1:s - 1 + rc, :] + scw_ref[1:2, :] * m_sc[s:s + rc, :]
              + scw_ref[2:3, :] * m_sc[s + 1:s + 1 + rc, :])
        o_ref[0, r0:r0 + rc, w:2 * w] = (zc_ref[0, r0:r0 + rc, 2 * w:3 * w].astype(F32) * cv).astype(BF16)


def _conv_mixers(z, dw, dwb, lng, lnb, scw, n_lat):
    b, nt, _ = z.shape
    tl = SEQ_BLOCK
    n_tiles = nt // tl
    hb = tl // HALO
    nhb = nt // HALO
    kern = functools.partial(_conv_kernel, lat_tiles=n_lat // tl, n_tiles=n_tiles)
    vec = lambda n: pl.BlockSpec((n, MIX_W), lambda bi, t: (0, 0))
    return pl.pallas_call(
        kern,
        grid=(b, n_tiles),
        in_specs=[pl.BlockSpec((1, tl, ZC_W), lambda bi, t: (bi, t, 0)),
                  pl.BlockSpec((1, HALO, ZC_W), lambda bi, t: (bi, jnp.maximum(t * hb - 1, 0), 0)),
                  pl.BlockSpec((1, HALO, ZC_W), lambda bi, t: (bi, jnp.minimum((t + 1) * hb, nhb - 1), 0)),
                  vec(CONF_KERNEL), vec(1), vec(1), vec(1), vec(SC_KERNEL)],
        out_specs=pl.BlockSpec((1, tl, 2 * MIX_W), lambda bi, t: (bi, t, 0)),
        out_shape=jax.ShapeDtypeStruct((b, nt, 2 * MIX_W), BF16),
        scratch_shapes=[pltpu.VMEM((SUBLANES, tl + 2 * HALO, MIX_W), F32),
                        pltpu.VMEM((tl + 2 * HALO, MIX_W), F32)],
        compiler_params=_params(("arbitrary", "arbitrary")),
        name="conv_mixers",
    )(z, z, z, dw, dwb, lng, lnb, scw)


def _attn_kernel(q_ref, k_ref, v_ref, o_ref, *, sub):
    for r0 in range(0, q_ref.shape[1], sub):
        rows = slice(r0, r0 + sub)
        s = _dot_nt(q_ref[0, rows, :], k_ref[0])
        p = jnp.exp2(s - jnp.max(s, axis=-1, keepdims=True))
        denom = jnp.sum(p, axis=-1, keepdims=True)
        o_ref[0, rows, :] = (_dot(p.astype(BF16), v_ref[0]) / denom).astype(BF16)


def _attention(q, k, v, n_lat, need_ctx):
    b, nt, _ = q.shape
    kern = functools.partial(_attn_kernel, sub=Q_SUB)
    out = pl.pallas_call(
        kern,
        grid=(b, HEADS, n_lat // Q_BLOCK),
        in_specs=[pl.BlockSpec((1, Q_BLOCK, 2 * HEAD_W), lambda bi, h, qi: (bi, qi, h)),
                  pl.BlockSpec((1, nt, 2 * HEAD_W), lambda bi, h, qi: (bi, 0, h)),
                  pl.BlockSpec((1, nt, HEAD_W), lambda bi, h, qi: (bi, 0, h))],
        out_specs=pl.BlockSpec((1, Q_BLOCK, HEAD_W), lambda bi, h, qi: (bi, qi, h)),
        out_shape=jax.ShapeDtypeStruct((b, n_lat, MIX_W), BF16),
        compiler_params=_params(("arbitrary", "arbitrary", "arbitrary")),
        name="mla_attention",
    )(q, k, v)
    if not need_ctx:
        return out
    n_ctx = nt - n_lat
    cblk = n_lat // n_ctx
    ctx_spec = lambda wd: pl.BlockSpec((1, n_ctx, wd), lambda bi, h: (bi, cblk, h))
    out_ctx = pl.pallas_call(
        kern,
        grid=(b, HEADS),
        in_specs=[ctx_spec(2 * HEAD_W), ctx_spec(2 * HEAD_W), ctx_spec(HEAD_W)],
        out_specs=pl.BlockSpec((1, n_ctx, HEAD_W), lambda bi, h: (bi, 0, h)),
        out_shape=jax.ShapeDtypeStruct((b, n_ctx, MIX_W), BF16),
        compiler_params=_params(("arbitrary", "arbitrary")),
        name="mla_attention_ctx",
    )(q, k, v)
    return jnp.concatenate([out, out_ctx], axis=1)


def _wout_kernel(gla_ref, cs_ref, mla_ref, x_ref, ml_ref, mc_ref, g_ref, w_ref, xo_ref, h_ref,
                 *, n_lat, rc):
    t = pl.program_id(1)
    tm = x_ref.shape[1]
    for r in range(tm // rc):
        rows = slice(r * rc, (r + 1) * rc)
        mix = jnp.concatenate([gla_ref[0, rows, :], cs_ref[0, rows, :], mla_ref[0, rows, :]], axis=1)
        o = _dot(mix, w_ref[...])
        rowid = t * tm + r * rc + lax.broadcasted_iota(jnp.int32, (rc, 1), 0)
        is_ctx = rowid >= n_lat
        g1 = jnp.where(is_ctx, mc_ref[0, 2:3, :], ml_ref[0, 2:3, :])
        sh = jnp.where(is_ctx, mc_ref[0, 3:4, :], ml_ref[0, 3:4, :])
        sc = jnp.where(is_ctx, mc_ref[0, 4:5, :], ml_ref[0, 4:5, :])
        xn = x_ref[0, rows, :] + g1 * o
        xo_ref[0, rows, :] = xn
        h_ref[0, rows, :] = (_rms(xn, g_ref[...]) * (1.0 + sc) + sh).astype(BF16)


def _wout(gla, cs, mla, xx, mod, g, w, n_lat, n_rows, layer):
    b, _, d = xx.shape
    tm = n_rows // 8
    kern = functools.partial(_wout_kernel, n_lat=n_lat, rc=tm // 2)
    tile = lambda wd: pl.BlockSpec((1, tm, wd), lambda bi, t: (bi, t, 0))
    ml, mc = _mod_specs(layer, b, d)
    return pl.pallas_call(
        kern,
        grid=(b, n_rows // tm),
        in_specs=[tile(MIX_W), tile(2 * MIX_W), tile(MIX_W), tile(d), ml, mc,
                  pl.BlockSpec((1, d), lambda bi, t: (0, 0)),
                  pl.BlockSpec((None,) + w.shape[1:], lambda bi, t: (layer, 0, 0))],
        out_specs=[tile(d), tile(d)],
        out_shape=[jax.ShapeDtypeStruct((b, n_rows, d), F32),
                   jax.ShapeDtypeStruct((b, n_rows, d), BF16)],
        compiler_params=_params(("arbitrary", "arbitrary")),
        name="w_out_norm2",
    )(gla, cs, mla, xx, mod, mod, g, w)


def _ffn_up_kernel(h_ref, w1_ref, w3_ref, u_ref, *, rc):
    tm = h_ref.shape[1]
    w1 = w1_ref[...].astype(BF16)
    w3 = w3_ref[...].astype(BF16)
    for r in range(tm // rc):
        rows = slice(r * rc, (r + 1) * rc)
        hh = h_ref[0, rows, :]
        u_ref[0, rows, :] = (_silu(_dot(hh, w1)) * _dot(hh, w3)).astype(BF16)


def _ffn_up(h2, w1, w3, layer):
    b, n_rows, d = h2.shape
    tm = n_rows // 4
    tf = 512
    dff = w1.shape[2]
    kern = functools.partial(_ffn_up_kernel, rc=tm // 4)
    wspec = pl.BlockSpec((None, d, tf), lambda bi, t, j: (layer, 0, j))
    return pl.pallas_call(
        kern,
        grid=(b, n_rows // tm, dff // tf),
        in_specs=[pl.BlockSpec((1, tm, d), lambda bi, t, j: (bi, t, 0)), wspec, wspec],
        out_specs=pl.BlockSpec((1, tm, tf), lambda bi, t, j: (bi, t, j)),
        out_shape=jax.ShapeDtypeStruct((b, n_rows, dff), BF16),
        compiler_params=_params(("arbitrary", "arbitrary", "arbitrary")),
        name="ffn_up",
    )(h2, w1, w3)


def _ffn_down_kernel(u_ref, x_ref, ml_ref, mc_ref, w2_ref, fg_ref, o_ref, *, n_lat, rc, final):
    t = pl.program_id(1)
    j = pl.program_id(2)
    tm = u_ref.shape[1]
    tn = w2_ref.shape[1]
    cols = pl.ds(pl.multiple_of(j * tn, tn), tn)
    xo_cols = cols if final else slice(None)
    for r in range(tm // rc):
        rows = slice(r * rc, (r + 1) * rc)
        rowid = t * tm + r * rc + lax.broadcasted_iota(jnp.int32, (rc, 1), 0)
        g2 = jnp.where(rowid >= n_lat, mc_ref[0, 5:6, cols], ml_ref[0, 5:6, cols])
        o_ref[0, rows, xo_cols] = x_ref[0, rows, xo_cols] + g2 * _dot(u_ref[0, rows, :], w2_ref[...])

    if final:
        @pl.when(j == pl.num_programs(2) - 1)
        def _():
            for r in range(tm // rc):
                rows = slice(r * rc, (r + 1) * rc)
                o_ref[0, rows, :] = _rms(o_ref[0, rows, :], fg_ref[...])


def _ffn_down(u, xn, mod, w2, fg, n_lat, final, layer):
    b, n_rows, d = xn.shape
    tn = 512
    dff = w2.shape[1]
    if final:
        tm = n_rows // 8
        xo_spec = pl.BlockSpec((1, tm, d), lambda bi, t, j: (bi, t, 0))
    else:
        tm = n_rows // 4
        xo_spec = pl.BlockSpec((1, tm, tn), lambda bi, t, j: (bi, t, j))
    kern = functools.partial(_ffn_down_kernel, n_lat=n_lat, rc=tm // (2 if final else 4), final=final)
    ml, mc = _mod_specs(layer, b, d)
    return pl.pallas_call(
        kern,
        grid=(b, n_rows // tm, d // tn),
        in_specs=[pl.BlockSpec((1, tm, dff), lambda bi, t, j: (bi, t, 0)),
                  xo_spec, ml, mc,
                  pl.BlockSpec((None, dff, tn), lambda bi, t, j: (layer, 0, j)),
                  pl.BlockSpec((1, d), lambda bi, t, j: (0, 0))],
        out_specs=xo_spec,
        out_shape=jax.ShapeDtypeStruct((b, n_rows, d), F32),
        compiler_params=_params(("arbitrary", "arbitrary", "arbitrary")),
        name="ffn_down",
    )(u, xn, mod, mod, w2, fg)


def _rope_tables(n_lat, n_ctx):
    rows = n_lat // GRID_W
    rowp = jnp.repeat(jnp.arange(rows, dtype=F32), GRID_W)
    colp = jnp.tile(jnp.arange(GRID_W, dtype=F32), rows)
    inv = ROPE_BASE ** (-jnp.arange(ROPE_FREQS, dtype=F32) * 2.0 / (2 * ROPE_FREQS))
    ar, ac = rowp[:, None] * inv, colp[:, None] * inv
    zeros = jnp.zeros((n_lat, HEAD_W - MLA_ROPE), F32)
    cos_l = jnp.concatenate([jnp.cos(ar), jnp.cos(ar), jnp.cos(ac), jnp.cos(ac), zeros], axis=1)
    sin_l = jnp.concatenate([-jnp.sin(ar), jnp.sin(ar), -jnp.sin(ac), jnp.sin(ac), zeros], axis=1)
    cos_c = jnp.concatenate([jnp.ones((n_ctx, MLA_ROPE), F32), jnp.zeros((n_ctx, HEAD_W - MLA_ROPE), F32)], axis=1)
    return jnp.concatenate([cos_l, cos_c], axis=0), jnp.concatenate([sin_l, jnp.zeros_like(cos_c)], axis=0)


def kernel(x, c, ctx, c_ctx, norm1_g, w_mod, b_mod, w_in, gla_fg_up, gla_fg_b, gla_onorm_g, conf_dw, conf_dw_b, conf_ln_g, conf_ln_b, sc_dw, mla_q_norm_g, mla_kv_norm_g, mla_w_uq, mla_w_ukv, w_out, norm2_g, ffn_w1, ffn_w3, ffn_w2, final_norm_g):
    bsz, n_lat, d = x.shape
    n_ctx = ctx.shape[1]
    depth = w_in.shape[0]
    w = MIX_W

    xx = jnp.concatenate([x, ctx], axis=1)
    c3 = jnp.concatenate([c, c_ctx[None, :]], axis=0)
    mod = _modulation(jnp.broadcast_to(c3[:, :, None], (bsz + 1, d, 128)), w_mod, b_mod)
    mod = mod.reshape(depth, MOD_ROWS, 6, d)
    cos_t, sin_t = _rope_tables(n_lat, n_ctx)

    o_gla, o_lr, o_conv, o_mla = 0, 4 * w, 4 * w + 2 * GLA_GATE_RANK, 4 * w + 2 * GLA_GATE_RANK + 5 * w
    w_main = jnp.concatenate([w_in[:, :, o_conv:o_mla], w_in[:, :, o_gla:o_lr]], axis=-1).astype(BF16)
    w_small = jnp.concatenate([w_in[:, :, o_mla:], w_in[:, :, o_lr:o_conv],
                               jnp.zeros((depth, d, ZS_W - (w_in.shape[2] - o_mla) - 2 * GLA_GATE_RANK), F32)],
                              axis=-1).astype(BF16)
    lr0 = MLA_ROPE
    fw = jnp.stack([jnp.pad(gla_fg_up[:, dd], ((0, 0), (lr0 + dd * GLA_GATE_RANK,
                                                        HEAD_W - lr0 - (dd + 1) * GLA_GATE_RANK), (0, 0)))
                    for dd in range(2)], axis=1)
    fw = fw.reshape(depth, 2, HEAD_W, HEADS, HEAD_W).transpose(0, 1, 3, 2, 4)
    fb = gla_fg_b.reshape(depth, 2, HEADS, 1, HEAD_W)
    wq = mla_w_uq.reshape(depth, MLA_Q_RANK, HEADS, HEAD_W + MLA_ROPE)
    wq = jnp.pad(wq, ((0, 0), (0, 0), (0, 0), (0, HEAD_W - MLA_ROPE))).reshape(depth, MLA_Q_RANK, -1).astype(BF16)
    wkv = mla_w_ukv.astype(BF16)
    wo = w_out.astype(BF16)
    w2 = ffn_w2.astype(BF16)
    row = lambda a: a.reshape(1, -1)

    for i in range(depth):
        last = i == depth - 1
        z, zl, qq, kk, vv = _win(xx, mod, row(norm1_g[i]), w_main, w_small, row(mla_q_norm_g[i]),
                                 row(mla_kv_norm_g[i]), wq, wkv, cos_t, sin_t, n_lat, i)
        gla = _gla(z, zl, fw, fb[i], row(gla_onorm_g[i]), n_lat, i)
        cs = _conv_mixers(z, conf_dw[i], row(conf_dw_b[i]), row(conf_ln_g[i]), row(conf_ln_b[i]), sc_dw[i], n_lat)
        mla = _attention(qq, kk, vv, n_lat, not last)
        n_rows = n_lat if last else n_lat + n_ctx
        xn, h2 = _wout(gla, cs, mla, xx, mod, row(norm2_g[i]), wo, n_lat, n_rows, i)
        u = _ffn_up(h2, ffn_w1, ffn_w3, i)
        xx = _ffn_down(u, xn, mod, w2, row(final_norm_g), n_lat, last, i)
    return xx
```

```python
import functools
import math

import jax
import jax.numpy as jnp
import numpy as np
from jax import lax
from jax.experimental import pallas as pl
from jax.experimental.pallas import tpu as pltpu

F32 = jnp.float32
BF16 = jnp.bfloat16

EPS = 1e-6
GRID_W = 64
HEADS = 4
HEAD_W = 128
GLA_GATE_RANK = 16
GLA_GATE_NORM = 16.0
GLA_CHUNK = 64
GLA_DIAG = 16
GLA_SAFE_DECAY = 60.0
CONF_KERNEL = 31
SC_KERNEL = 3
MIX_W = 512
MLA_ROPE = 64
MLA_Q_RANK = 384
MLA_KV_RANK = 128
ROPE_FREQS = 16
ROPE_BASE = 10000.0
Q_BLOCK = 2048
Q_SUB = 256
SEQ_BLOCK = 256
HALO = 16
SUBLANES = 8
ZC_W = 5 * MIX_W
ZG_W = 4 * MIX_W
MOD_ROWS = 8
ZS_W = 640
VMEM_LIMIT_V7X = 56 * 1024 * 1024


def _dot(a, b):
    return jnp.dot(a, b, preferred_element_type=F32)


def _dot_nt(a, b):
    return lax.dot_general(a, b, (((1,), (1,)), ((), ())), preferred_element_type=F32)


def _dot_tn(a, b):
    return lax.dot_general(a, b, (((0,), (0,)), ((), ())), preferred_element_type=F32)


def _split3(x):
    h1 = x.astype(BF16)
    r1 = x - h1.astype(F32)
    h2 = r1.astype(BF16)
    h3 = (r1 - h2.astype(F32)).astype(BF16)
    return h1, h2, h3


def _sigmoid(x):
    return 1.0 / (1.0 + jnp.exp(-x))


def _silu(x):
    return x * _sigmoid(x)


def _rms(x, g):
    return x * lax.rsqrt(jnp.mean(x * x, axis=-1, keepdims=True) + EPS) * g


def _params(sem):
    return pltpu.CompilerParams(dimension_semantics=sem, vmem_limit_bytes=VMEM_LIMIT_V7X)


def _mod_kernel(c_ref, w_ref, b_ref, o_ref, act_sc):
    tn = w_ref.shape[2]
    rows = []
    nrow = c_ref.shape[0]

    @pl.when((pl.program_id(0) == 0) & (pl.program_id(1) == 0))
    def _():
        for m in range(nrow):
            act_sc[m] = _silu(c_ref[m])

    kc = 8 * SUBLANES
    ncb = tn // 128
    acc = [[jnp.zeros((SUBLANES, 128), F32) for _ in range(ncb)] for _ in range(nrow)]
    for k0 in range(0, w_ref.shape[1], kc):
        wk = w_ref[0, k0:k0 + kc, :]
        for m in range(nrow):
            am = act_sc[m, k0:k0 + kc, :]
            for cb in range(ncb):
                part = wk[:, cb * 128:(cb + 1) * 128] * am
                acc[m][cb] = acc[m][cb] + jnp.sum(part.reshape(kc // SUBLANES, SUBLANES, 128), axis=0)
    for m in range(nrow):
        out_m = jnp.concatenate([jnp.sum(a, axis=0, keepdims=True) for a in acc[m]], axis=1)
        rows.append(out_m + b_ref[0])
    rows.append(jnp.zeros((MOD_ROWS - nrow, tn), F32))
    o_ref[0] = jnp.concatenate(rows, axis=0)


def _modulation(cb, w_mod, b_mod):
    depth, d, n6 = w_mod.shape
    tn = 1024
    return pl.pallas_call(
        _mod_kernel,
        grid=(depth, n6 // tn),
        in_specs=[pl.BlockSpec(cb.shape, lambda i, j: (0, 0, 0)),
                  pl.BlockSpec((1, d, tn), lambda i, j: (i, 0, j)),
                  pl.BlockSpec((1, 1, tn), lambda i, j: (i, 0, j))],
        out_specs=pl.BlockSpec((1, MOD_ROWS, tn), lambda i, j: (i, 0, j)),
        out_shape=jax.ShapeDtypeStruct((depth, MOD_ROWS, n6), F32),
        scratch_shapes=[pltpu.VMEM(cb.shape, F32)],
        compiler_params=_params(("arbitrary", "arbitrary")),
        name="modulation",
    )(cb, w_mod, b_mod.reshape(depth, 1, n6))


def _mod_specs(layer, b, d):
    lat = lambda bi, *_: (layer, bi, 0, 0)
    ctx = lambda *_: (layer, b, 0, 0)
    return (pl.BlockSpec((None, 1, 6, d), lat), pl.BlockSpec((None, 1, 6, d), ctx))


def _win_kernel(x_ref, ml_ref, mc_ref, g_ref, w_ref, ws_ref, qg_ref, kvg_ref, wq_ref, wkv_ref,
                cos_ref, sin_ref, z_ref, zl_ref, q_ref, k_ref, v_ref, *, n_lat, rc, nc):
    t = pl.program_id(1)
    tm = x_ref.shape[1]
    nw = w_ref.shape[1]
    scale = (HEAD_W + MLA_ROPE) ** -0.5 * math.log2(math.e)
    lane = lax.broadcasted_iota(jnp.int32, (1, HEAD_W), 1)
    lower = (lane & 31) < 16
    for r in range(tm // rc):
        rows = slice(r * rc, (r + 1) * rc)
        y = _rms(x_ref[0, rows, :], g_ref[...])
        rowid = t * tm + r * rc + lax.broadcasted_iota(jnp.int32, (rc, 1), 0)
        is_ctx = rowid >= n_lat
        sh = jnp.where(is_ctx, mc_ref[0, 0:1, :], ml_ref[0, 0:1, :])
        sc = jnp.where(is_ctx, mc_ref[0, 1:2, :], ml_ref[0, 1:2, :])
        h = (y * (1.0 + sc) + sh).astype(BF16)
        zs = _dot(h, ws_ref[...])
        for n0 in range(0, nw, nc):
            z_ref[0, rows, n0:n0 + nc] = _dot(h, w_ref[:, n0:n0 + nc]).astype(BF16)

        krb = zs[:, MLA_Q_RANK + MLA_KV_RANK:ZS_W]
        zl_ref[0, rows, :] = krb
        q = _dot(_rms(zs[:, 0:MLA_Q_RANK], qg_ref[...]).astype(BF16), wq_ref[...])
        kv = _dot(_rms(zs[:, MLA_Q_RANK:MLA_Q_RANK + MLA_KV_RANK], kvg_ref[...]).astype(BF16), wkv_ref[...])
        cs = cos_ref[rows, :]
        sn = sin_ref[rows, :]

        def rope(tt):
            swapped = jnp.where(lower, pltpu.roll(tt, HEAD_W - 16, 1), pltpu.roll(tt, 16, 1))
            return tt * cs + swapped * sn

        krr = rope(krb).astype(BF16)
        for hd in range(HEADS):
            c0 = 2 * HEAD_W * hd
            q_ref[0, rows, c0:c0 + HEAD_W] = (q[:, c0:c0 + HEAD_W] * scale).astype(BF16)
            q_ref[0, rows, c0 + HEAD_W:c0 + 2 * HEAD_W] = (
                rope(q[:, c0 + HEAD_W:c0 + 2 * HEAD_W]) * scale).astype(BF16)
            k_ref[0, rows, c0:c0 + HEAD_W] = kv[:, c0:c0 + HEAD_W].astype(BF16)
            k_ref[0, rows, c0 + HEAD_W:c0 + 2 * HEAD_W] = krr
            v_ref[0, rows, HEAD_W * hd:HEAD_W * (hd + 1)] = kv[:, c0 + HEAD_W:c0 + 2 * HEAD_W].astype(BF16)


def _win(xx, mod, g, w_main, w_small, qg, kvg, wq, wkv, cos_t, sin_t, n_lat, layer):
    b, nt, d = xx.shape
    tm = nt // 8
    nw = w_main.shape[2]
    kw = 2 * HEAD_W * HEADS
    kern = functools.partial(_win_kernel, n_lat=n_lat, rc=tm // 2, nc=nw // 3)
    ml, mc = _mod_specs(layer, b, d)
    resident = lambda a: pl.BlockSpec((None,) + a.shape[1:], lambda bi, t: (layer, 0, 0),
                                      pipeline_mode=pl.Buffered(1))
    vec = lambda a: pl.BlockSpec(a.shape, lambda bi, t: (0, 0))
    tile = lambda wd: pl.BlockSpec((1, tm, wd), lambda bi, t: (bi, t, 0))
    return pl.pallas_call(
        kern,
        grid=(b, nt // tm),
        in_specs=[tile(d), ml, mc, vec(g), resident(w_main), resident(w_small),
                  vec(qg), vec(kvg), resident(wq), resident(wkv),
                  pl.BlockSpec((tm, HEAD_W), lambda bi, t: (t, 0)),
                  pl.BlockSpec((tm, HEAD_W), lambda bi, t: (t, 0))],
        out_specs=[tile(nw), tile(HEAD_W), tile(kw), tile(kw), tile(MIX_W)],
        out_shape=[jax.ShapeDtypeStruct((b, nt, nw), BF16),
                   jax.ShapeDtypeStruct((b, nt, HEAD_W), F32),
                   jax.ShapeDtypeStruct((b, nt, kw), BF16),
                   jax.ShapeDtypeStruct((b, nt, kw), BF16),
                   jax.ShapeDtypeStruct((b, nt, MIX_W), BF16)],
        compiler_params=_params(("arbitrary", "arbitrary")),
        name="norm1_w_in",
    )(xx, mod, mod, g, w_main, w_small, qg, kvg, wq, wkv, cos_t, sin_t)


def _gla_kernel(q_ref, k_ref, v_ref, g_ref, zs_ref, fw_ref, fb_ref, on_ref, o_ref,
                bc_sc, qin_sc, kv_sc, er_sc, oacc_sc, oint_sc, msk_sc, tri_sc, sel_sc, *, n_lat):
    n_tot = q_ref.shape[1]
    blk = SEQ_BLOCK
    cpb = blk // GLA_CHUNK
    nblk = n_tot // blk
    nchunk = n_tot // GLA_CHUNK
    lat_chunks = n_lat // GLA_CHUNK
    ctx_chunks = nchunk - lat_chunks
    ndb = blk // GLA_DIAG
    qscale = HEAD_W ** -0.5
    unroll = max(u for u in range(1, 18) if nchunk % u == 0)
    group_pro, group_intra = 8, 4

    ri = lax.broadcasted_iota(jnp.int32, (blk, blk), 0)
    ci = lax.broadcasted_iota(jnp.int32, (blk, blk), 1)
    same64 = (ri >> 6) == (ci >> 6)
    same16 = (ri >> 4) == (ci >> 4)
    one = lambda m: jnp.where(m, 1.0, 0.0).astype(F32)
    msk_sc[0] = one(same64)
    msk_sc[1] = one((ri >> 5) == (ci >> 5))
    msk_sc[2] = one(same16 & ((ri & 15) >= (ci & 15)))
    msk_sc[3] = one(same16 & ((ri & 15) <= (ci & 15)))
    msk_sc[4] = one(same64 & (ri >= ci))
    msk_sc[5] = one(same64 & (ri <= ci))
    tri_sc[0] = msk_sc[4].astype(BF16)
    tri_sc[1] = msk_sc[5].astype(BF16)
    cj = lax.broadcasted_iota(jnp.int32, (HEAD_W, blk), 1) & (GLA_DIAG - 1)
    for jj in range(GLA_DIAG):
        sel_sc[jj * HEAD_W:(jj + 1) * HEAD_W, :] = one(cj == jj).astype(BF16)

    row = lax.broadcasted_iota(jnp.int32, (blk, 1), 0)
    half0 = (row & 63) < 32
    quart0 = (row & 31) < 16

    def block_rows(i, nb=1):
        start = i * blk if isinstance(i, int) else pl.multiple_of(i * blk, blk)
        return pl.ds(start, nb * blk)

    def paired(fn, group):
        def body(p, carry):
            fn(group * p, group)
            return carry
        lax.fori_loop(0, nblk // group, body, 0)
        if nblk % group:
            fn(nblk - nblk % group, nblk % group)

    fsplit = [_split3(fw_ref[d, 0]) for d in range(2)]
    fcat = jnp.concatenate([jnp.concatenate([fh, fh, fl], axis=0) for fh, fl, _ in fsplit], axis=1)

    def prologue(i0, nb):
        for u in range(nb):
            rows = block_rows(i0 + u)
            zh, zl, _ = _split3(zs_ref[0, rows, :])
            xg2 = _dot(jnp.concatenate([zh, zl, zh], axis=1), fcat)
            for d in range(2):
                xg = xg2[:, d * HEAD_W:(d + 1) * HEAD_W] + fb_ref[d, 0]
                logd = (jnp.minimum(xg, 0.0) - jnp.log1p(jnp.exp(-jnp.abs(xg)))) * (1.0 / GLA_GATE_NORM)
                c3 = _dot(tri_sc[d], jnp.concatenate(_split3(logd), axis=1))
                bc_sc[d, rows, :] = c3[:, 0:HEAD_W] + c3[:, HEAD_W:2 * HEAD_W] + c3[:, 2 * HEAD_W:]

    paired(prologue, group_pro)

    def run_direction(d):
        rev = d == 1

        def load_block(i):
            rows = block_rows(i)
            b = bc_sc[d, rows, :]
            q = q_ref[0, rows, :].astype(F32) * qscale
            k = k_ref[0, rows, :].astype(F32)
            v = v_ref[0, rows, :]
            b4 = b.reshape(cpb, GLA_CHUNK, HEAD_W)
            blast = b4[:, 0:1, :] if rev else b4[:, GLA_CHUNK - 1:GLA_CHUNK, :]
            kst = (k.reshape(cpb, GLA_CHUNK, HEAD_W) * jnp.exp(blast - b4)).astype(BF16)
            v4 = v.reshape(cpb, GLA_CHUNK, HEAD_W)
            ebl = jnp.exp(blast)
            for c in range(cpb):
                kv_sc[i * cpb + c] = _dot_tn(v4[c], kst[c])
                er_sc[i * cpb + c] = jnp.broadcast_to(ebl[c], (SUBLANES, HEAD_W))
            qin = (q * jnp.exp(b)).astype(BF16)
            qin_sc[rows, :] = qin
            return rows, b, q, k, v, qin

        def store_block(rows, scores, v):
            o_blk = _dot(scores.astype(BF16), v)
            if rev:
                oacc_sc[rows, :] += o_blk
            else:
                oacc_sc[rows, :] = o_blk

        def intra_plain(i):
            rows, b, q, k, v, qin = load_block(i)
            s = _dot_nt(qin, (k * jnp.exp(-b)).astype(BF16))
            store_block(rows, s * msk_sc[5 if rev else 4], v)

        def intra_robust(i):
            rows, b, q, k, v, _ = load_block(i)
            b4 = b.reshape(cpb, GLA_CHUNK, HEAD_W)
            r1 = b4[:, 32:33, :] if rev else b4[:, 31:32, :]
            r1 = jnp.broadcast_to(r1, b4.shape).reshape(blk, HEAD_W)
            qsel, ksel = (half0, ~half0) if rev else (~half0, half0)
            qa = q * jnp.where(qsel, jnp.exp(jnp.minimum(b - r1, 0.0)), 0.0)
            ka = k * jnp.where(ksel, jnp.exp(jnp.minimum(r1 - b, 0.0)), 0.0)
            s1 = _dot_nt(qa.astype(BF16), ka.astype(BF16))
            b8 = b.reshape(blk // 32, 32, HEAD_W)
            r2 = b8[:, 16:17, :] if rev else b8[:, 15:16, :]
            r2 = jnp.broadcast_to(r2, b8.shape).reshape(blk, HEAD_W)
            qsel, ksel = (quart0, ~quart0) if rev else (~quart0, quart0)
            qb = q * jnp.where(qsel, jnp.exp(jnp.minimum(b - r2, 0.0)), 0.0)
            kb = k * jnp.where(ksel, jnp.exp(jnp.minimum(r2 - b, 0.0)), 0.0)
            s2 = _dot_nt(qb.astype(BF16), kb.astype(BF16))
            q16 = q.reshape(ndb, GLA_DIAG, HEAD_W)
            k16 = k.reshape(ndb, GLA_DIAG, HEAD_W)
            b16 = b.reshape(ndb, GLA_DIAG, HEAD_W)
            half = GLA_DIAG // 2
            terms = []
            for jj in range(GLA_DIAG):
                bj = b16[:, jj:jj + 1, :]
                kj = k16[:, jj:jj + 1, :]
                need = ((True, jj >= half) if rev else (jj < half, True))
                parts = []
                for hsel, needed in zip((slice(0, half), slice(half, GLA_DIAG)), need):
                    if needed:
                        e = jnp.exp(jnp.minimum(b16[:, hsel, :] - bj, 0.0))
                        parts.append(q16[:, hsel, :] * kj * e)
                    else:
                        parts.append(jnp.zeros((ndb, half, HEAD_W), F32))
                terms.append(jnp.concatenate(parts, axis=1).reshape(blk, HEAD_W).astype(BF16))
            sdiag = _dot(jnp.concatenate(terms, axis=1), sel_sc[...])
            store_block(rows, s1 * msk_sc[0] + s2 * msk_sc[1] + sdiag * msk_sc[3 if rev else 2], v)

        def intra(i0, nb):
            plain_ok = jnp.min(bc_sc[d, block_rows(i0, nb), :]) >= -GLA_SAFE_DECAY

            @pl.when(plain_ok)
            def _():
                for u in range(nb):
                    intra_plain(i0 + u)

            @pl.when(jnp.logical_not(plain_ok))
            def _():
                for u in range(nb):
                    intra_robust(i0 + u)

        paired(intra, group_intra)

        def inter(m, st):
            for u in range(unroll):
                n = m * unroll + u
                if rev:
                    cid = jnp.where(n < ctx_chunks, lat_chunks + (ctx_chunks - 1 - n),
                                    lat_chunks - 1 - (n - ctx_chunks))
                else:
                    cid = jnp.where(n < ctx_chunks, lat_chunks + n, n - ctx_chunks)
                rows = pl.ds(pl.multiple_of(cid * GLA_CHUNK, GLA_CHUNK), GLA_CHUNK)
                oint_sc[d, rows, :] = _dot_nt(qin_sc[rows, :], st.astype(BF16))
                st = st * er_sc[cid][0:1, :] + kv_sc[cid]
            return st

        lax.fori_loop(0, nchunk // unroll, inter, jnp.zeros((HEAD_W, HEAD_W), F32))

    run_direction(0)
    run_direction(1)

    def finish(i, carry):
        rows = block_rows(i)
        y = _rms(oacc_sc[rows, :] + oint_sc[0, rows, :] + oint_sc[1, rows, :], on_ref[...])
        o_ref[0, rows, :] = (y * _silu(g_ref[0, rows, :].astype(F32))).astype(BF16)
        return carry

    lax.fori_loop(0, nblk, finish, 0)


def _gla(z, zs, fw, fb, on, n_lat, layer):
    b, nt, _ = z.shape
    cb0 = ZC_W // HEAD_W
    seq = lambda off: pl.BlockSpec((1, nt, HEAD_W), lambda bi, h: (bi, 0, cb0 + off * HEADS + h))
    kern = functools.partial(_gla_kernel, n_lat=n_lat)
    return pl.pallas_call(
        kern,
        grid=(b, HEADS),
        in_specs=[seq(0), seq(1), seq(2), seq(3),
                  pl.BlockSpec((1, nt, HEAD_W), lambda bi, h: (bi, 0, 0)),
                  pl.BlockSpec((None, 2, 1, HEAD_W, HEAD_W), lambda bi, h: (layer, 0, h, 0, 0)),
                  pl.BlockSpec((2, 1, 1, HEAD_W), lambda bi, h: (0, h, 0, 0)),
                  pl.BlockSpec((1, HEAD_W), lambda bi, h: (0, 0))],
        out_specs=pl.BlockSpec((1, nt, HEAD_W), lambda bi, h: (bi, 0, h)),
        out_shape=jax.ShapeDtypeStruct((b, nt, MIX_W), BF16),
        scratch_shapes=[pltpu.VMEM((2, nt, HEAD_W), F32),
                        pltpu.VMEM((nt, HEAD_W), BF16),
                        pltpu.VMEM((nt // GLA_CHUNK, HEAD_W, HEAD_W), F32),
                        pltpu.VMEM((nt // GLA_CHUNK, SUBLANES, HEAD_W), F32),
                        pltpu.VMEM((nt, HEAD_W), F32),
                        pltpu.VMEM((2, nt, HEAD_W), F32),
                        pltpu.VMEM((6, SEQ_BLOCK, SEQ_BLOCK), F32),
                        pltpu.VMEM((2, SEQ_BLOCK, SEQ_BLOCK), BF16),
                        pltpu.VMEM((GLA_DIAG * HEAD_W, SEQ_BLOCK), BF16)],
        compiler_params=_params(("arbitrary", "arbitrary")),
        name="gla_mixer",
    )(z, z, z, z, zs, fw, fb, on)


def _conv_kernel(zc_ref, zp_ref, zn_ref, dw_ref, dwb_ref, lng_ref, lnb_ref, scw_ref, o_ref,
                 u_sc, m_sc, *, lat_tiles, n_tiles):
    t = pl.program_id(1)
    tl = zc_ref.shape[1]
    w = MIX_W
    first = (t == 0) | (t == lat_tiles)
    last = (t == lat_tiles - 1) | (t == n_tiles - 1)

    def glu(ref, rows):
        return ref[0, rows, 0:w].astype(F32) * _sigmoid(ref[0, rows, w:2 * w].astype(F32))

    def gated(ref, rows):
        return ref[0, rows, 3 * w:4 * w].astype(F32) * ref[0, rows, 4 * w:5 * w].astype(F32)

    halo = slice(0, HALO)
    u_sc[0, 0:HALO, :] = jnp.where(first, 0.0, glu(zp_ref, halo))
    m_sc[0:HALO, :] = jnp.where(first, 0.0, gated(zp_ref, halo))
    u_sc[0, HALO + tl:, :] = jnp.where(last, 0.0, glu(zn_ref, halo))
    m_sc[HALO + tl:, :] = jnp.where(last, 0.0, gated(zn_ref, halo))
    rc = 32
    for r in range(tl // rc):
        rows = slice(r * rc, (r + 1) * rc)
        u_sc[0, HALO + r * rc:HALO + (r + 1) * rc, :] = glu(zc_ref, rows)
        m_sc[HALO + r * rc:HALO + (r + 1) * rc, :] = gated(zc_ref, rows)
    span = tl + 2 * HALO - SUBLANES
    for s in range(1, SUBLANES):
        u_sc[s, 0:span, :] = u_sc[0, s:s + span, :]

    pad = CONF_KERNEL // 2
    for r in range(tl // rc):
        r0 = r * rc
        acc = jnp.zeros((rc, w), F32)
        for kk in range(CONF_KERNEL):
            off = HALO + r0 + kk - pad
            base = off - off % SUBLANES
            acc = acc + dw_ref[kk:kk + 1, :] * u_sc[off % SUBLANES, base:base + rc, :]
        acc = acc + dwb_ref[...]
        mu = jnp.mean(acc, axis=-1, keepdims=True)
        xc = acc - mu
        y = xc * lax.rsqrt(jnp.mean(xc * xc, axis=-1, keepdims=True) + EPS) * lng_ref[...] + lnb_ref[...]
        o_ref[0, r0:r0 + rc, 0:w] = _silu(y).astype(BF16)
        s = HALO + r0
        cv = (scw_ref[0:1, :] * m_sc[s - 1:s - 1 + rc, :] + scw_ref[1:2, :] * m_sc[s:s + rc, :]
              + scw_ref[2:3, :] * m_sc[s + 1:s + 1 + rc, :])
        o_ref[0, r0:r0 + rc, w:2 * w] = (zc_ref[0, r0:r0 + rc, 2 * w:3 * w].astype(F32) * cv).astype(BF16)


def _conv_mixers(z, dw, dwb, lng, lnb, scw, n_lat):
    b, nt, _ = z.shape
    tl = SEQ_BLOCK
    n_tiles = nt // tl
    hb = tl // HALO
    nhb = nt // HALO
    kern = functools.partial(_conv_kernel, lat_tiles=n_lat // tl, n_tiles=n_tiles)
    vec = lambda n: pl.BlockSpec((n, MIX_W), lambda bi, t: (0, 0))
    return pl.pallas_call(
        kern,
        grid=(b, n_tiles),
        in_specs=[pl.BlockSpec((1, tl, ZC_W), lambda bi, t: (bi, t, 0)),
                  pl.BlockSpec((1, HALO, ZC_W), lambda bi, t: (bi, jnp.maximum(t * hb - 1, 0), 0)),
                  pl.BlockSpec((1, HALO, ZC_W), lambda bi, t: (bi, jnp.minimum((t + 1) * hb, nhb - 1), 0)),
                  vec(CONF_KERNEL), vec(1), vec(1), vec(1), vec(SC_KERNEL)],
        out_specs=pl.BlockSpec((1, tl, 2 * MIX_W), lambda bi, t: (bi, t, 0)),
        out_shape=jax.ShapeDtypeStruct((b, nt, 2 * MIX_W), BF16),
        scratch_shapes=[pltpu.VMEM((SUBLANES, tl + 2 * HALO, MIX_W), F32),
                        pltpu.VMEM((tl + 2 * HALO, MIX_W), F32)],
        compiler_params=_params(("arbitrary", "arbitrary")),
        name="conv_mixers",
    )(z, z, z, dw, dwb, lng, lnb, scw)


def _attn_kernel(q_ref, k_ref, v_ref, o_ref, *, sub):
    for r0 in range(0, q_ref.shape[1], sub):
        rows = slice(r0, r0 + sub)
        s = _dot_nt(q_ref[0, rows, :], k_ref[0])
        p = jnp.exp2(s - jnp.max(s, axis=-1, keepdims=True))
        denom = jnp.sum(p, axis=-1, keepdims=True)
        o_ref[0, rows, :] = (_dot(p.astype(BF16), v_ref[0]) / denom).astype(BF16)


def _attention(q, k, v, n_lat, need_ctx):
    b, nt, _ = q.shape
    kern = functools.partial(_attn_kernel, sub=Q_SUB)
    out = pl.pallas_call(
        kern,
        grid=(b, HEADS, n_lat // Q_BLOCK),
        in_specs=[pl.BlockSpec((1, Q_BLOCK, 2 * HEAD_W), lambda bi, h, qi: (bi, qi, h)),
                  pl.BlockSpec((1, nt, 2 * HEAD_W), lambda bi, h, qi: (bi, 0, h)),
                  pl.BlockSpec((1, nt, HEAD_W), lambda bi, h, qi: (bi, 0, h))],
        out_specs=pl.BlockSpec((1, Q_BLOCK, HEAD_W), lambda bi, h, qi: (bi, qi, h)),
        out_shape=jax.ShapeDtypeStruct((b, n_lat, MIX_W), BF16),
        compiler_params=_params(("arbitrary", "arbitrary", "arbitrary")),
        name="mla_attention",
    )(q, k, v)
    if not need_ctx:
        return out
    n_ctx = nt - n_lat
    cblk = n_lat // n_ctx
    ctx_spec = lambda wd: pl.BlockSpec((1, n_ctx, wd), lambda bi, h: (bi, cblk, h))
    out_ctx = pl.pallas_call(
        kern,
        grid=(b, HEADS),
        in_specs=[ctx_spec(2 * HEAD_W), ctx_spec(2 * HEAD_W), ctx_spec(HEAD_W)],
        out_specs=pl.BlockSpec((1, n_ctx, HEAD_W), lambda bi, h: (bi, 0, h)),
        out_shape=jax.ShapeDtypeStruct((b, n_ctx, MIX_W), BF16),
        compiler_params=_params(("arbitrary", "arbitrary")),
        name="mla_attention_ctx",
    )(q, k, v)
    return jnp.concatenate([out, out_ctx], axis=1)


def _wout_kernel(gla_ref, cs_ref, mla_ref, x_ref, ml_ref, mc_ref, g_ref, w_ref, xo_ref, h_ref,
                 *, n_lat, rc):
    t = pl.program_id(1)
    tm = x_ref.shape[1]
    for r in range(tm // rc):
        rows = slice(r * rc, (r + 1) * rc)
        mix = jnp.concatenate([gla_ref[0, rows, :], cs_ref[0, rows, :], mla_ref[0, rows, :]], axis=1)
        o = _dot(mix, w_ref[...])
        rowid = t * tm + r * rc + lax.broadcasted_iota(jnp.int32, (rc, 1), 0)
        is_ctx = rowid >= n_lat
        g1 = jnp.where(is_ctx, mc_ref[0, 2:3, :], ml_ref[0, 2:3, :])
        sh = jnp.where(is_ctx, mc_ref[0, 3:4, :], ml_ref[0, 3:4, :])
        sc = jnp.where(is_ctx, mc_ref[0, 4:5, :], ml_ref[0, 4:5, :])
        xn = x_ref[0, rows, :] + g1 * o
        xo_ref[0, rows, :] = xn
        h_ref[0, rows, :] = (_rms(xn, g_ref[...]) * (1.0 + sc) + sh).astype(BF16)


def _wout(gla, cs, mla, xx, mod, g, w, n_lat, n_rows, layer):
    b, _, d = xx.shape
    tm = n_rows // 8
    kern = functools.partial(_wout_kernel, n_lat=n_lat, rc=tm // 2)
    tile = lambda wd: pl.BlockSpec((1, tm, wd), lambda bi, t: (bi, t, 0))
    ml, mc = _mod_specs(layer, b, d)
    return pl.pallas_call(
        kern,
        grid=(b, n_rows // tm),
        in_specs=[tile(MIX_W), tile(2 * MIX_W), tile(MIX_W), tile(d), ml, mc,
                  pl.BlockSpec((1, d), lambda bi, t: (0, 0)),
                  pl.BlockSpec((None,) + w.shape[1:], lambda bi, t: (layer, 0, 0))],
        out_specs=[tile(d), tile(d)],
        out_shape=[jax.ShapeDtypeStruct((b, n_rows, d), F32),
                   jax.ShapeDtypeStruct((b, n_rows, d), BF16)],
        compiler_params=_params(("arbitrary", "arbitrary")),
        name="w_out_norm2",
    )(gla, cs, mla, xx, mod, mod, g, w)


def _ffn_up_kernel(h_ref, w1_ref, w3_ref, u_ref, *, rc):
    tm = h_ref.shape[1]
    w1 = w1_ref[...].astype(BF16)
    w3 = w3_ref[...].astype(BF16)
    for r in range(tm // rc):
        rows = slice(r * rc, (r + 1) * rc)
        hh = h_ref[0, rows, :]
        u_ref[0, rows, :] = (_silu(_dot(hh, w1)) * _dot(hh, w3)).astype(BF16)


def _ffn_up(h2, w1, w3, layer):
    b, n_rows, d = h2.shape
    tm = n_rows // 2
    tf = 512
    dff = w1.shape[2]
    kern = functools.partial(_ffn_up_kernel, rc=tm // 8)
    wspec = pl.BlockSpec((None, d, tf), lambda bi, t, j: (layer, 0, j))
    return pl.pallas_call(
        kern,
        grid=(b, n_rows // tm, dff // tf),
        in_specs=[pl.BlockSpec((1, tm, d), lambda bi, t, j: (bi, t, 0)), wspec, wspec],
        out_specs=pl.BlockSpec((1, tm, tf), lambda bi, t, j: (bi, t, j)),
        out_shape=jax.ShapeDtypeStruct((b, n_rows, dff), BF16),
        compiler_params=_params(("arbitrary", "arbitrary", "arbitrary")),
        name="ffn_up",
    )(h2, w1, w3)


def _ffn_down_kernel(u_ref, x_ref, ml_ref, mc_ref, w2_ref, fg_ref, o_ref, *, n_lat, rc, final):
    t = pl.program_id(1)
    j = pl.program_id(2)
    tm = u_ref.shape[1]
    tn = w2_ref.shape[1]
    cols = pl.ds(pl.multiple_of(j * tn, tn), tn)
    xo_cols = cols if final else slice(None)
    for r in range(tm // rc):
        rows = slice(r * rc, (r + 1) * rc)
        rowid = t * tm + r * rc + lax.broadcasted_iota(jnp.int32, (rc, 1), 0)
        g2 = jnp.where(rowid >= n_lat, mc_ref[0, 5:6, cols], ml_ref[0, 5:6, cols])
        o_ref[0, rows, xo_cols] = x_ref[0, rows, xo_cols] + g2 * _dot(u_ref[0, rows, :], w2_ref[...])

    if final:
        @pl.when(j == pl.num_programs(2) - 1)
        def _():
            for r in range(tm // rc):
                rows = slice(r * rc, (r + 1) * rc)
                o_ref[0, rows, :] = _rms(o_ref[0, rows, :], fg_ref[...])


def _ffn_down(u, xn, mod, w2, fg, n_lat, final, layer):
    b, n_rows, d = xn.shape
    tn = 512
    dff = w2.shape[1]
    if final:
        tm = n_rows // 8
        xo_spec = pl.BlockSpec((1, tm, d), lambda bi, t, j: (bi, t, 0))
    else:
        tm = n_rows // 4
        xo_spec = pl.BlockSpec((1, tm, tn), lambda bi, t, j: (bi, t, j))
    kern = functools.partial(_ffn_down_kernel, n_lat=n_lat, rc=tm // (2 if final else 4), final=final)
    ml, mc = _mod_specs(layer, b, d)
    return pl.pallas_call(
        kern,
        grid=(b, n_rows // tm, d // tn),
        in_specs=[pl.BlockSpec((1, tm, dff), lambda bi, t, j: (bi, t, 0)),
                  xo_spec, ml, mc,
                  pl.BlockSpec((None, dff, tn), lambda bi, t, j: (layer, 0, j)),
                  pl.BlockSpec((1, d), lambda bi, t, j: (0, 0))],
        out_specs=xo_spec,
        out_shape=jax.ShapeDtypeStruct((b, n_rows, d), F32),
        compiler_params=_params(("arbitrary", "arbitrary", "arbitrary")),
        name="ffn_down",
    )(u, xn, mod, mod, w2, fg)


def _rope_tables(n_lat, n_ctx):
    rows = n_lat // GRID_W
    f32 = np.float32
    rowp = np.repeat(np.arange(rows, dtype=f32), GRID_W)
    colp = np.tile(np.arange(GRID_W, dtype=f32), rows)
    inv = (f32(ROPE_BASE) ** (-np.arange(ROPE_FREQS, dtype=f32) * f32(2.0) / f32(2 * ROPE_FREQS))).astype(f32)
    ar, ac = (rowp[:, None] * inv).astype(f32), (colp[:, None] * inv).astype(f32)
    zeros = np.zeros((n_lat, HEAD_W - MLA_ROPE), f32)
    cos_l = np.concatenate([np.cos(ar), np.cos(ar), np.cos(ac), np.cos(ac), zeros], axis=1)
    sin_l = np.concatenate([-np.sin(ar), np.sin(ar), -np.sin(ac), np.sin(ac), zeros], axis=1)
    cos_c = np.concatenate([np.ones((n_ctx, MLA_ROPE), f32), np.zeros((n_ctx, HEAD_W - MLA_ROPE), f32)], axis=1)
    cos_t = np.concatenate([cos_l, cos_c], axis=0).astype(f32)
    sin_t = np.concatenate([sin_l, np.zeros_like(cos_c)], axis=0).astype(f32)
    return jnp.asarray(cos_t), jnp.asarray(sin_t)


def kernel(x, c, ctx, c_ctx, norm1_g, w_mod, b_mod, w_in, gla_fg_up, gla_fg_b, gla_onorm_g, conf_dw, conf_dw_b, conf_ln_g, conf_ln_b, sc_dw, mla_q_norm_g, mla_kv_norm_g, mla_w_uq, mla_w_ukv, w_out, norm2_g, ffn_w1, ffn_w3, ffn_w2, final_norm_g):
    bsz, n_lat, d = x.shape
    n_ctx = ctx.shape[1]
    depth = w_in.shape[0]
    w = MIX_W

    xx = jnp.concatenate([x, ctx], axis=1)
    c3 = jnp.concatenate([c, c_ctx[None, :]], axis=0)
    mod = _modulation(jnp.broadcast_to(c3[:, :, None], (bsz + 1, d, 128)), w_mod, b_mod)
    mod = mod.reshape(depth, MOD_ROWS, 6, d)
    cos_t, sin_t = _rope_tables(n_lat, n_ctx)

    o_gla, o_lr, o_conv, o_mla = 0, 4 * w, 4 * w + 2 * GLA_GATE_RANK, 4 * w + 2 * GLA_GATE_RANK + 5 * w
    wb = w_in.astype(BF16)
    w_main = jnp.concatenate([wb[:, :, o_conv:o_mla], wb[:, :, o_gla:o_lr]], axis=-1)
    w_small = jnp.concatenate([wb[:, :, o_mla:], wb[:, :, o_lr:o_conv],
                               jnp.zeros((depth, d, ZS_W - (w_in.shape[2] - o_mla) - 2 * GLA_GATE_RANK), BF16)],
                              axis=-1)
    lr0 = MLA_ROPE
    fw = jnp.stack([jnp.pad(gla_fg_up[:, dd], ((0, 0), (lr0 + dd * GLA_GATE_RANK,
                                                        HEAD_W - lr0 - (dd + 1) * GLA_GATE_RANK), (0, 0)))
                    for dd in range(2)], axis=1)
    fw = fw.reshape(depth, 2, HEAD_W, HEADS, HEAD_W).transpose(0, 1, 3, 2, 4)
    fb = gla_fg_b.reshape(depth, 2, HEADS, 1, HEAD_W)
    wq = mla_w_uq.reshape(depth, MLA_Q_RANK, HEADS, HEAD_W + MLA_ROPE)
    wq = jnp.pad(wq, ((0, 0), (0, 0), (0, 0), (0, HEAD_W - MLA_ROPE))).reshape(depth, MLA_Q_RANK, -1).astype(BF16)
    wkv = mla_w_ukv.astype(BF16)
    wo = w_out.astype(BF16)
    w2 = ffn_w2.astype(BF16)
    row = lambda a: a.reshape(1, -1)

    for i in range(depth):
        last = i == depth - 1
        z, zl, qq, kk, vv = _win(xx, mod, row(norm1_g[i]), w_main, w_small, row(mla_q_norm_g[i]),
                                 row(mla_kv_norm_g[i]), wq, wkv, cos_t, sin_t, n_lat, i)
        gla = _gla(z, zl, fw, fb[i], row(gla_onorm_g[i]), n_lat, i)
        cs = _conv_mixers(z, conf_dw[i], row(conf_dw_b[i]), row(conf_ln_g[i]), row(conf_ln_b[i]), sc_dw[i], n_lat)
        mla = _attention(qq, kk, vv, n_lat, not last)
        n_rows = n_lat if last else n_lat + n_ctx
        xn, h2 = _wout(gla, cs, mla, xx, mod, row(norm2_g[i]), wo, n_lat, n_rows, i)
        u = _ffn_up(h2, ffn_w1, ffn_w3, i)
        xx = _ffn_down(u, xn, mod, w2, row(final_norm_g), n_lat, last, i)
    return xx
```

```python
import functools
import math

import jax
import jax.numpy as jnp
import numpy as np
from jax import lax
from jax.experimental import pallas as pl
from jax.experimental.pallas import tpu as pltpu

F32 = jnp.float32
BF16 = jnp.bfloat16

EPS = 1e-6
GRID_W = 64
HEADS = 4
HEAD_W = 128
GLA_GATE_RANK = 16
GLA_GATE_NORM = 16.0
GLA_CHUNK = 64
GLA_DIAG = 16
GLA_SAFE_DECAY = 60.0
CONF_KERNEL = 31
SC_KERNEL = 3
MIX_W = 512
MLA_ROPE = 64
MLA_Q_RANK = 384
MLA_KV_RANK = 128
ROPE_FREQS = 16
ROPE_BASE = 10000.0
Q_BLOCK = 2048
Q_SUB = 256
SEQ_BLOCK = 256
HALO = 16
SUBLANES = 8
ZC_W = 5 * MIX_W
ZG_W = 4 * MIX_W
MOD_ROWS = 8
ZS_W = 640
VMEM_LIMIT_V7X = 56 * 1024 * 1024


def _dot(a, b):
    return jnp.dot(a, b, preferred_element_type=F32)


def _dot_nt(a, b):
    return lax.dot_general(a, b, (((1,), (1,)), ((), ())), preferred_element_type=F32)


def _dot_tn(a, b):
    return lax.dot_general(a, b, (((0,), (0,)), ((), ())), preferred_element_type=F32)


def _split3(x):
    h1 = x.astype(BF16)
    r1 = x - h1.astype(F32)
    h2 = r1.astype(BF16)
    h3 = (r1 - h2.astype(F32)).astype(BF16)
    return h1, h2, h3


def _sigmoid(x):
    return 1.0 / (1.0 + jnp.exp(-x))


def _silu(x):
    return x * _sigmoid(x)


def _rms(x, g):
    return x * lax.rsqrt(jnp.mean(x * x, axis=-1, keepdims=True) + EPS) * g


def _params(sem):
    return pltpu.CompilerParams(dimension_semantics=sem, vmem_limit_bytes=VMEM_LIMIT_V7X)


def _mod_kernel(c_ref, w_ref, b_ref, o_ref, act_sc):
    tn = w_ref.shape[2]
    rows = []
    nrow = c_ref.shape[0]

    @pl.when((pl.program_id(0) == 0) & (pl.program_id(1) == 0))
    def _():
        for m in range(nrow):
            act_sc[m] = _silu(c_ref[m])

    kc = 8 * SUBLANES
    ncb = tn // 128
    acc = [[jnp.zeros((SUBLANES, 128), F32) for _ in range(ncb)] for _ in range(nrow)]
    for k0 in range(0, w_ref.shape[1], kc):
        wk = w_ref[0, k0:k0 + kc, :]
        for m in range(nrow):
            am = act_sc[m, k0:k0 + kc, :]
            for cb in range(ncb):
                part = wk[:, cb * 128:(cb + 1) * 128] * am
                acc[m][cb] = acc[m][cb] + jnp.sum(part.reshape(kc // SUBLANES, SUBLANES, 128), axis=0)
    for m in range(nrow):
        out_m = jnp.concatenate([jnp.sum(a, axis=0, keepdims=True) for a in acc[m]], axis=1)
        rows.append(out_m + b_ref[0])
    rows.append(jnp.zeros((MOD_ROWS - nrow, tn), F32))
    o_ref[0] = jnp.concatenate(rows, axis=0)


def _modulation(cb, w_mod, b_mod):
    depth, d, n6 = w_mod.shape
    tn = 1024
    return pl.pallas_call(
        _mod_kernel,
        grid=(depth, n6 // tn),
        in_specs=[pl.BlockSpec(cb.shape, lambda i, j: (0, 0, 0)),
                  pl.BlockSpec((1, d, tn), lambda i, j: (i, 0, j)),
                  pl.BlockSpec((1, 1, tn), lambda i, j: (i, 0, j))],
        out_specs=pl.BlockSpec((1, MOD_ROWS, tn), lambda i, j: (i, 0, j)),
        out_shape=jax.ShapeDtypeStruct((depth, MOD_ROWS, n6), F32),
        scratch_shapes=[pltpu.VMEM(cb.shape, F32)],
        compiler_params=_params(("arbitrary", "arbitrary")),
        name="modulation",
    )(cb, w_mod, b_mod.reshape(depth, 1, n6))


def _mod_specs(layer, b, d):
    lat = lambda bi, *_: (layer, bi, 0, 0)
    ctx = lambda *_: (layer, b, 0, 0)
    return (pl.BlockSpec((None, 1, 6, d), lat), pl.BlockSpec((None, 1, 6, d), ctx))


def _win_kernel(x_ref, ml_ref, mc_ref, g_ref, wc_ref, wg_ref, ws_ref, qg_ref, kvg_ref, wq_ref, wkv_ref,
                cos_ref, sin_ref, z_ref, zl_ref, q_ref, k_ref, v_ref, *, n_lat, rc):
    t = pl.program_id(1)
    tm = x_ref.shape[1]
    scale = (HEAD_W + MLA_ROPE) ** -0.5 * math.log2(math.e)
    lane = lax.broadcasted_iota(jnp.int32, (1, HEAD_W), 1)
    lower = (lane & 31) < 16
    for r in range(tm // rc):
        rows = slice(r * rc, (r + 1) * rc)
        y = _rms(x_ref[0, rows, :], g_ref[...])
        rowid = t * tm + r * rc + lax.broadcasted_iota(jnp.int32, (rc, 1), 0)
        is_ctx = rowid >= n_lat
        sh = jnp.where(is_ctx, mc_ref[0, 0:1, :], ml_ref[0, 0:1, :])
        sc = jnp.where(is_ctx, mc_ref[0, 1:2, :], ml_ref[0, 1:2, :])
        h = (y * (1.0 + sc) + sh).astype(BF16)
        zs = _dot(h, ws_ref[...])
        z0 = 0
        for w_ref in (wc_ref, wg_ref):
            nw = w_ref.shape[1]
            for n0 in range(0, nw, nw // 2):
                z_ref[0, rows, z0 + n0:z0 + n0 + nw // 2] = _dot(h, w_ref[:, n0:n0 + nw // 2]).astype(BF16)
            z0 += nw

        krb = zs[:, MLA_Q_RANK + MLA_KV_RANK:ZS_W]
        zl_ref[0, rows, :] = krb
        q = _dot(_rms(zs[:, 0:MLA_Q_RANK], qg_ref[...]).astype(BF16), wq_ref[...])
        kv = _dot(_rms(zs[:, MLA_Q_RANK:MLA_Q_RANK + MLA_KV_RANK], kvg_ref[...]).astype(BF16), wkv_ref[...])
        cs = cos_ref[rows, :]
        sn = sin_ref[rows, :]

        def rope(tt):
            swapped = jnp.where(lower, pltpu.roll(tt, HEAD_W - 16, 1), pltpu.roll(tt, 16, 1))
            return tt * cs + swapped * sn

        krr = rope(krb).astype(BF16)
        for hd in range(HEADS):
            c0 = 2 * HEAD_W * hd
            q_ref[0, rows, c0:c0 + HEAD_W] = (q[:, c0:c0 + HEAD_W] * scale).astype(BF16)
            q_ref[0, rows, c0 + HEAD_W:c0 + 2 * HEAD_W] = (
                rope(q[:, c0 + HEAD_W:c0 + 2 * HEAD_W]) * scale).astype(BF16)
            k_ref[0, rows, c0:c0 + HEAD_W] = kv[:, c0:c0 + HEAD_W].astype(BF16)
            k_ref[0, rows, c0 + HEAD_W:c0 + 2 * HEAD_W] = krr
            v_ref[0, rows, HEAD_W * hd:HEAD_W * (hd + 1)] = kv[:, c0 + HEAD_W:c0 + 2 * HEAD_W].astype(BF16)


def _win(xx, mod, g, w_conv, w_gla, w_small, qg, kvg, wq, wkv, cos_t, sin_t, n_lat, layer):
    b, nt, d = xx.shape
    tm = nt // 8
    nw = w_conv.shape[2] + w_gla.shape[2]
    kw = 2 * HEAD_W * HEADS
    kern = functools.partial(_win_kernel, n_lat=n_lat, rc=tm // 2)
    ml, mc = _mod_specs(layer, b, d)
    resident = lambda a: pl.BlockSpec((None,) + a.shape[1:], lambda bi, t: (layer, 0, 0),
                                      pipeline_mode=pl.Buffered(1))
    vec = lambda a: pl.BlockSpec(a.shape, lambda bi, t: (0, 0))
    tile = lambda wd: pl.BlockSpec((1, tm, wd), lambda bi, t: (bi, t, 0))
    return pl.pallas_call(
        kern,
        grid=(b, nt // tm),
        in_specs=[tile(d), ml, mc, vec(g), resident(w_conv), resident(w_gla), resident(w_small),
                  vec(qg), vec(kvg), resident(wq), resident(wkv),
                  pl.BlockSpec((tm, HEAD_W), lambda bi, t: (t, 0)),
                  pl.BlockSpec((tm, HEAD_W), lambda bi, t: (t, 0))],
        out_specs=[tile(nw), tile(HEAD_W), tile(kw), tile(kw), tile(MIX_W)],
        out_shape=[jax.ShapeDtypeStruct((b, nt, nw), BF16),
                   jax.ShapeDtypeStruct((b, nt, HEAD_W), F32),
                   jax.ShapeDtypeStruct((b, nt, kw), BF16),
                   jax.ShapeDtypeStruct((b, nt, kw), BF16),
                   jax.ShapeDtypeStruct((b, nt, MIX_W), BF16)],
        compiler_params=_params(("arbitrary", "arbitrary")),
        name="norm1_w_in",
    )(xx, mod, mod, g, w_conv, w_gla, w_small, qg, kvg, wq, wkv, cos_t, sin_t)


def _gla_kernel(q_ref, k_ref, v_ref, g_ref, zs_ref, fw_ref, fb_ref, on_ref, o_ref,
                bc_sc, qin_sc, kv_sc, er_sc, oacc_sc, oint_sc, msk_sc, tri_sc, sel_sc, *, n_lat):
    n_tot = q_ref.shape[1]
    blk = SEQ_BLOCK
    cpb = blk // GLA_CHUNK
    nblk = n_tot // blk
    nchunk = n_tot // GLA_CHUNK
    lat_chunks = n_lat // GLA_CHUNK
    ctx_chunks = nchunk - lat_chunks
    ndb = blk // GLA_DIAG
    qscale = HEAD_W ** -0.5
    unroll = max(u for u in range(1, 18) if nchunk % u == 0)
    group_pro, group_intra = 8, 4

    ri = lax.broadcasted_iota(jnp.int32, (blk, blk), 0)
    ci = lax.broadcasted_iota(jnp.int32, (blk, blk), 1)
    same64 = (ri >> 6) == (ci >> 6)
    same16 = (ri >> 4) == (ci >> 4)
    one = lambda m: jnp.where(m, 1.0, 0.0).astype(F32)
    msk_sc[0] = one(same64)
    msk_sc[1] = one((ri >> 5) == (ci >> 5))
    msk_sc[2] = one(same16 & ((ri & 15) >= (ci & 15)))
    msk_sc[3] = one(same16 & ((ri & 15) <= (ci & 15)))
    msk_sc[4] = one(same64 & (ri >= ci))
    msk_sc[5] = one(same64 & (ri <= ci))
    tri_sc[0] = msk_sc[4].astype(BF16)
    tri_sc[1] = msk_sc[5].astype(BF16)
    cj = lax.broadcasted_iota(jnp.int32, (HEAD_W, blk), 1) & (GLA_DIAG - 1)
    for jj in range(GLA_DIAG):
        sel_sc[jj * HEAD_W:(jj + 1) * HEAD_W, :] = one(cj == jj).astype(BF16)

    row = lax.broadcasted_iota(jnp.int32, (blk, 1), 0)
    half0 = (row & 63) < 32
    quart0 = (row & 31) < 16

    def block_rows(i, nb=1):
        start = i * blk if isinstance(i, int) else pl.multiple_of(i * blk, blk)
        return pl.ds(start, nb * blk)

    def paired(fn, group):
        def body(p, carry):
            fn(group * p, group)
            return carry
        lax.fori_loop(0, nblk // group, body, 0)
        if nblk % group:
            fn(nblk - nblk % group, nblk % group)

    fsplit = [_split3(fw_ref[d, 0]) for d in range(2)]
    fcat = jnp.concatenate([jnp.concatenate([fh, fh, fl], axis=0) for fh, fl, _ in fsplit], axis=1)

    def prologue(i0, nb):
        for u in range(nb):
            rows = block_rows(i0 + u)
            zh, zl, _ = _split3(zs_ref[0, rows, :])
            xg2 = _dot(jnp.concatenate([zh, zl, zh], axis=1), fcat)
            for d in range(2):
                xg = xg2[:, d * HEAD_W:(d + 1) * HEAD_W] + fb_ref[d, 0]
                logd = (jnp.minimum(xg, 0.0) - jnp.log1p(jnp.exp(-jnp.abs(xg)))) * (1.0 / GLA_GATE_NORM)
                c2 = _dot(tri_sc[d], jnp.concatenate(_split3(logd)[:2], axis=1))
                bc_sc[d, rows, :] = c2[:, 0:HEAD_W] + c2[:, HEAD_W:]

    paired(prologue, group_pro)

    def run_direction(d):
        rev = d == 1

        def load_block(i, plain):
            rows = block_rows(i)
            b = bc_sc[d, rows, :]
            q = q_ref[0, rows, :].astype(F32) * qscale
            k = k_ref[0, rows, :].astype(F32)
            v = v_ref[0, rows, :]
            b4 = b.reshape(cpb, GLA_CHUNK, HEAD_W)
            blast = b4[:, 0:1, :] if rev else b4[:, GLA_CHUNK - 1:GLA_CHUNK, :]
            ebl = jnp.exp(blast)
            if plain:
                kt = k * jnp.exp(-b)
                kst = (kt.reshape(cpb, GLA_CHUNK, HEAD_W) * ebl).astype(BF16)
            else:
                kt = None
                kst = (k.reshape(cpb, GLA_CHUNK, HEAD_W) * jnp.exp(blast - b4)).astype(BF16)
            v4 = v.reshape(cpb, GLA_CHUNK, HEAD_W)
            for c in range(cpb):
                kv_sc[i * cpb + c] = _dot_tn(v4[c], kst[c])
                er_sc[i * cpb + c] = jnp.broadcast_to(ebl[c], (SUBLANES, HEAD_W))
            qin = (q * jnp.exp(b)).astype(BF16)
            qin_sc[rows, :] = qin
            return rows, b, q, k, v, qin, kt

        def store_block(rows, scores, v):
            o_blk = _dot(scores.astype(BF16), v)
            if rev:
                oacc_sc[rows, :] += o_blk
            else:
                oacc_sc[rows, :] = o_blk

        def intra_plain(i):
            rows, _, _, _, v, qin, kt = load_block(i, True)
            s = _dot_nt(qin, kt.astype(BF16))
            store_block(rows, s * msk_sc[5 if rev else 4], v)

        def intra_robust(i):
            rows, b, q, k, v, _, _ = load_block(i, False)
            b4 = b.reshape(cpb, GLA_CHUNK, HEAD_W)
            r1 = b4[:, 32:33, :] if rev else b4[:, 31:32, :]
            r1 = jnp.broadcast_to(r1, b4.shape).reshape(blk, HEAD_W)
            qsel, ksel = (half0, ~half0) if rev else (~half0, half0)
            qa = q * jnp.where(qsel, jnp.exp(jnp.minimum(b - r1, 0.0)), 0.0)
            ka = k * jnp.where(ksel, jnp.exp(jnp.minimum(r1 - b, 0.0)), 0.0)
            s1 = _dot_nt(qa.astype(BF16), ka.astype(BF16))
            b8 = b.reshape(blk // 32, 32, HEAD_W)
            r2 = b8[:, 16:17, :] if rev else b8[:, 15:16, :]
            r2 = jnp.broadcast_to(r2, b8.shape).reshape(blk, HEAD_W)
            qsel, ksel = (quart0, ~quart0) if rev else (~quart0, quart0)
            qb = q * jnp.where(qsel, jnp.exp(jnp.minimum(b - r2, 0.0)), 0.0)
            kb = k * jnp.where(ksel, jnp.exp(jnp.minimum(r2 - b, 0.0)), 0.0)
            s2 = _dot_nt(qb.astype(BF16), kb.astype(BF16))
            q16 = q.reshape(ndb, GLA_DIAG, HEAD_W)
            k16 = k.reshape(ndb, GLA_DIAG, HEAD_W)
            b16 = b.reshape(ndb, GLA_DIAG, HEAD_W)
            half = GLA_DIAG // 2
            terms = []
            for jj in range(GLA_DIAG):
                bj = b16[:, jj:jj + 1, :]
                kj = k16[:, jj:jj + 1, :]
                need = ((True, jj >= half) if rev else (jj < half, True))
                parts = []
                for hsel, needed in zip((slice(0, half), slice(half, GLA_DIAG)), need):
                    if needed:
                        e = jnp.exp(jnp.minimum(b16[:, hsel, :] - bj, 0.0))
                        parts.append(q16[:, hsel, :] * kj * e)
                    else:
                        parts.append(jnp.zeros((ndb, half, HEAD_W), F32))
                terms.append(jnp.concatenate(parts, axis=1).reshape(blk, HEAD_W).astype(BF16))
            sdiag = _dot(jnp.concatenate(terms, axis=1), sel_sc[...])
            store_block(rows, s1 * msk_sc[0] + s2 * msk_sc[1] + sdiag * msk_sc[3 if rev else 2], v)

        def intra(i0, nb):
            plain_ok = jnp.min(bc_sc[d, block_rows(i0, nb), :]) >= -GLA_SAFE_DECAY

            @pl.when(plain_ok)
            def _():
                for u in range(nb):
                    intra_plain(i0 + u)

            @pl.when(jnp.logical_not(plain_ok))
            def _():
                for u in range(nb):
                    intra_robust(i0 + u)

        paired(intra, group_intra)

        def inter(m, st):
            for u in range(unroll):
                n = m * unroll + u
                if rev:
                    cid = jnp.where(n < ctx_chunks, lat_chunks + (ctx_chunks - 1 - n),
                                    lat_chunks - 1 - (n - ctx_chunks))
                else:
                    cid = jnp.where(n < ctx_chunks, lat_chunks + n, n - ctx_chunks)
                rows = pl.ds(pl.multiple_of(cid * GLA_CHUNK, GLA_CHUNK), GLA_CHUNK)
                oint_sc[d, rows, :] = _dot_nt(qin_sc[rows, :], st.astype(BF16))
                st = st * er_sc[cid][0:1, :] + kv_sc[cid]
            return st

        lax.fori_loop(0, nchunk // unroll, inter, jnp.zeros((HEAD_W, HEAD_W), F32))

    run_direction(0)
    run_direction(1)

    def finish(i, carry):
        rows = block_rows(i)
        y = _rms(oacc_sc[rows, :] + oint_sc[0, rows, :] + oint_sc[1, rows, :], on_ref[...])
        o_ref[0, rows, :] = (y * _silu(g_ref[0, rows, :].astype(F32))).astype(BF16)
        return carry

    lax.fori_loop(0, nblk, finish, 0)


def _gla(z, zs, fw, fb, on, n_lat, layer):
    b, nt, _ = z.shape
    cb0 = ZC_W // HEAD_W
    seq = lambda off: pl.BlockSpec((1, nt, HEAD_W), lambda bi, h: (bi, 0, cb0 + off * HEADS + h))
    kern = functools.partial(_gla_kernel, n_lat=n_lat)
    return pl.pallas_call(
        kern,
        grid=(b, HEADS),
        in_specs=[seq(0), seq(1), seq(2), seq(3),
                  pl.BlockSpec((1, nt, HEAD_W), lambda bi, h: (bi, 0, 0)),
                  pl.BlockSpec((None, 2, 1, HEAD_W, HEAD_W), lambda bi, h: (layer, 0, h, 0, 0)),
                  pl.BlockSpec((2, 1, 1, HEAD_W), lambda bi, h: (0, h, 0, 0)),
                  pl.BlockSpec((1, HEAD_W), lambda bi, h: (0, 0))],
        out_specs=pl.BlockSpec((1, nt, HEAD_W), lambda bi, h: (bi, 0, h)),
        out_shape=jax.ShapeDtypeStruct((b, nt, MIX_W), BF16),
        scratch_shapes=[pltpu.VMEM((2, nt, HEAD_W), F32),
                        pltpu.VMEM((nt, HEAD_W), BF16),
                        pltpu.VMEM((nt // GLA_CHUNK, HEAD_W, HEAD_W), F32),
                        pltpu.VMEM((nt // GLA_CHUNK, SUBLANES, HEAD_W), F32),
                        pltpu.VMEM((nt, HEAD_W), F32),
                        pltpu.VMEM((2, nt, HEAD_W), F32),
                        pltpu.VMEM((6, SEQ_BLOCK, SEQ_BLOCK), F32),
                        pltpu.VMEM((2, SEQ_BLOCK, SEQ_BLOCK), BF16),
                        pltpu.VMEM((GLA_DIAG * HEAD_W, SEQ_BLOCK), BF16)],
        compiler_params=_params(("arbitrary", "arbitrary")),
        name="gla_mixer",
    )(z, z, z, z, zs, fw, fb, on)


def _conv_kernel(zc_ref, zp_ref, zn_ref, dw_ref, dwb_ref, lng_ref, lnb_ref, scw_ref, o_ref,
                 u_sc, m_sc, *, lat_tiles, n_tiles):
    t = pl.program_id(1)
    tl = zc_ref.shape[1]
    w = MIX_W
    first = (t == 0) | (t == lat_tiles)
    last = (t == lat_tiles - 1) | (t == n_tiles - 1)

    def glu(ref, rows):
        return ref[0, rows, 0:w].astype(F32) * _sigmoid(ref[0, rows, w:2 * w].astype(F32))

    def gated(ref, rows):
        return ref[0, rows, 3 * w:4 * w].astype(F32) * ref[0, rows, 4 * w:5 * w].astype(F32)

    halo = slice(0, HALO)
    u_sc[0, 0:HALO, :] = jnp.where(first, 0.0, glu(zp_ref, halo))
    m_sc[0:HALO, :] = jnp.where(first, 0.0, gated(zp_ref, halo))
    u_sc[0, HALO + tl:, :] = jnp.where(last, 0.0, glu(zn_ref, halo))
    m_sc[HALO + tl:, :] = jnp.where(last, 0.0, gated(zn_ref, halo))
    rc = 32
    for r in range(tl // rc):
        rows = slice(r * rc, (r + 1) * rc)
        u_sc[0, HALO + r * rc:HALO + (r + 1) * rc, :] = glu(zc_ref, rows)
        m_sc[HALO + r * rc:HALO + (r + 1) * rc, :] = gated(zc_ref, rows)
    span = tl + 2 * HALO - SUBLANES
    for s in range(1, SUBLANES):
        u_sc[s, 0:span, :] = u_sc[0, s:s + span, :]

    pad = CONF_KERNEL // 2
    for r in range(tl // rc):
        r0 = r * rc
        acc = jnp.zeros((rc, w), F32)
        for kk in range(CONF_KERNEL):
            off = HALO + r0 + kk - pad
            base = off - off % SUBLANES
            acc = acc + dw_ref[kk:kk + 1, :] * u_sc[off % SUBLANES, base:base + rc, :]
        acc = acc + dwb_ref[...]
        mu = jnp.mean(acc, axis=-1, keepdims=True)
        xc = acc - mu
        y = xc * lax.rsqrt(jnp.mean(xc * xc, axis=-1, keepdims=True) + EPS) * lng_ref[...] + lnb_ref[...]
        o_ref[0, r0:r0 + rc, 0:w] = _silu(y).astype(BF16)
        s = HALO + r0
        cv = (scw_ref[0:1, :] * m_sc[s - 1:s - 1 + rc, :] + scw_ref[1:2, :] * m_sc[s:s + rc, :]
              + scw_ref[2:3, :] * m_sc[s + 1:s + 1 + rc, :])
        o_ref[0, r0:r0 + rc, w:2 * w] = (zc_ref[0, r0:r0 + rc, 2 * w:3 * w].astype(F32) * cv).astype(BF16)


def _conv_mixers(z, dw, dwb, lng, lnb, scw, n_lat):
    b, nt, _ = z.shape
    tl = SEQ_BLOCK
    n_tiles = nt // tl
    hb = tl // HALO
    nhb = nt // HALO
    kern = functools.partial(_conv_kernel, lat_tiles=n_lat // tl, n_tiles=n_tiles)
    vec = lambda n: pl.BlockSpec((n, MIX_W), lambda bi, t: (0, 0))
    return pl.pallas_call(
        kern,
        grid=(b, n_tiles),
        in_specs=[pl.BlockSpec((1, tl, ZC_W), lambda bi, t: (bi, t, 0)),
                  pl.BlockSpec((1, HALO, ZC_W), lambda bi, t: (bi, jnp.maximum(t * hb - 1, 0), 0)),
                  pl.BlockSpec((1, HALO, ZC_W), lambda bi, t: (bi, jnp.minimum((t + 1) * hb, nhb - 1), 0)),
                  vec(CONF_KERNEL), vec(1), vec(1), vec(1), vec(SC_KERNEL)],
        out_specs=pl.BlockSpec((1, tl, 2 * MIX_W), lambda bi, t: (bi, t, 0)),
        out_shape=jax.ShapeDtypeStruct((b, nt, 2 * MIX_W), BF16),
        scratch_shapes=[pltpu.VMEM((SUBLANES, tl + 2 * HALO, MIX_W), F32),
                        pltpu.VMEM((tl + 2 * HALO, MIX_W), F32)],
        compiler_params=_params(("arbitrary", "arbitrary")),
        name="conv_mixers",
    )(z, z, z, dw, dwb, lng, lnb, scw)


def _attn_kernel(q_ref, k_ref, v_ref, o_ref, *, sub):
    for r0 in range(0, q_ref.shape[1], sub):
        rows = slice(r0, r0 + sub)
        s = _dot_nt(q_ref[0, rows, :], k_ref[0])
        p = jnp.exp2(s - jnp.max(s, axis=-1, keepdims=True))
        denom = jnp.sum(p, axis=-1, keepdims=True)
        o_ref[0, rows, :] = (_dot(p.astype(BF16), v_ref[0]) / denom).astype(BF16)


def _attention(q, k, v, n_lat, need_ctx):
    b, nt, _ = q.shape
    kern = functools.partial(_attn_kernel, sub=Q_SUB)
    out = pl.pallas_call(
        kern,
        grid=(b, HEADS, n_lat // Q_BLOCK),
        in_specs=[pl.BlockSpec((1, Q_BLOCK, 2 * HEAD_W), lambda bi, h, qi: (bi, qi, h)),
                  pl.BlockSpec((1, nt, 2 * HEAD_W), lambda bi, h, qi: (bi, 0, h)),
                  pl.BlockSpec((1, nt, HEAD_W), lambda bi, h, qi: (bi, 0, h))],
        out_specs=pl.BlockSpec((1, Q_BLOCK, HEAD_W), lambda bi, h, qi: (bi, qi, h)),
        out_shape=jax.ShapeDtypeStruct((b, n_lat, MIX_W), BF16),
        compiler_params=_params(("arbitrary", "arbitrary", "arbitrary")),
        name="mla_attention",
    )(q, k, v)
    if not need_ctx:
        return out
    n_ctx = nt - n_lat
    cblk = n_lat // n_ctx
    ctx_spec = lambda wd: pl.BlockSpec((1, n_ctx, wd), lambda bi, h: (bi, cblk, h))
    out_ctx = pl.pallas_call(
        kern,
        grid=(b, HEADS),
        in_specs=[ctx_spec(2 * HEAD_W), ctx_spec(2 * HEAD_W), ctx_spec(HEAD_W)],
        out_specs=pl.BlockSpec((1, n_ctx, HEAD_W), lambda bi, h: (bi, 0, h)),
        out_shape=jax.ShapeDtypeStruct((b, n_ctx, MIX_W), BF16),
        compiler_params=_params(("arbitrary", "arbitrary")),
        name="mla_attention_ctx",
    )(q, k, v)
    return jnp.concatenate([out, out_ctx], axis=1)


def _wout_kernel(gla_ref, cs_ref, mla_ref, x_ref, ml_ref, mc_ref, g_ref, w_ref, xo_ref, h_ref,
                 *, n_lat, rc):
    t = pl.program_id(1)
    tm = x_ref.shape[1]
    for r in range(tm // rc):
        rows = slice(r * rc, (r + 1) * rc)
        mix = jnp.concatenate([gla_ref[0, rows, :], cs_ref[0, rows, :], mla_ref[0, rows, :]], axis=1)
        o = _dot(mix, w_ref[...])
        rowid = t * tm + r * rc + lax.broadcasted_iota(jnp.int32, (rc, 1), 0)
        is_ctx = rowid >= n_lat
        g1 = jnp.where(is_ctx, mc_ref[0, 2:3, :], ml_ref[0, 2:3, :])
        sh = jnp.where(is_ctx, mc_ref[0, 3:4, :], ml_ref[0, 3:4, :])
        sc = jnp.where(is_ctx, mc_ref[0, 4:5, :], ml_ref[0, 4:5, :])
        xn = x_ref[0, rows, :] + g1 * o
        xo_ref[0, rows, :] = xn
        h_ref[0, rows, :] = (_rms(xn, g_ref[...]) * (1.0 + sc) + sh).astype(BF16)


def _wout(gla, cs, mla, xx, mod, g, w, n_lat, n_rows, layer):
    b, _, d = xx.shape
    tm = n_rows // 8
    kern = functools.partial(_wout_kernel, n_lat=n_lat, rc=tm // 2)
    tile = lambda wd: pl.BlockSpec((1, tm, wd), lambda bi, t: (bi, t, 0))
    ml, mc = _mod_specs(layer, b, d)
    return pl.pallas_call(
        kern,
        grid=(b, n_rows // tm),
        in_specs=[tile(MIX_W), tile(2 * MIX_W), tile(MIX_W), tile(d), ml, mc,
                  pl.BlockSpec((1, d), lambda bi, t: (0, 0)),
                  pl.BlockSpec((None,) + w.shape[1:], lambda bi, t: (layer, 0, 0))],
        out_specs=[tile(d), tile(d)],
        out_shape=[jax.ShapeDtypeStruct((b, n_rows, d), F32),
                   jax.ShapeDtypeStruct((b, n_rows, d), BF16)],
        compiler_params=_params(("arbitrary", "arbitrary")),
        name="w_out_norm2",
    )(gla, cs, mla, xx, mod, mod, g, w)


def _ffn_up_kernel(h_ref, w1_ref, w3_ref, u_ref, *, rc):
    tm = h_ref.shape[1]
    w1 = w1_ref[...].astype(BF16)
    w3 = w3_ref[...].astype(BF16)
    for r in range(tm // rc):
        rows = slice(r * rc, (r + 1) * rc)
        hh = h_ref[0, rows, :]
        u_ref[0, rows, :] = (_silu(_dot(hh, w1)) * _dot(hh, w3)).astype(BF16)


def _ffn_up(h2, w1, w3, layer):
    b, n_rows, d = h2.shape
    tm = n_rows // 2
    tf = 512
    dff = w1.shape[2]
    kern = functools.partial(_ffn_up_kernel, rc=tm // 8)
    wspec = pl.BlockSpec((None, d, tf), lambda bi, t, j: (layer, 0, j))
    return pl.pallas_call(
        kern,
        grid=(b, n_rows // tm, dff // tf),
        in_specs=[pl.BlockSpec((1, tm, d), lambda bi, t, j: (bi, t, 0)), wspec, wspec],
        out_specs=pl.BlockSpec((1, tm, tf), lambda bi, t, j: (bi, t, j)),
        out_shape=jax.ShapeDtypeStruct((b, n_rows, dff), BF16),
        compiler_params=_params(("arbitrary", "arbitrary", "arbitrary")),
        name="ffn_up",
    )(h2, w1, w3)


def _ffn_down_kernel(u_ref, x_ref, ml_ref, mc_ref, w2_ref, fg_ref, o_ref, *, n_lat, rc, final):
    t = pl.program_id(1)
    j = pl.program_id(2)
    tm = u_ref.shape[1]
    tn = w2_ref.shape[1]
    cols = pl.ds(pl.multiple_of(j * tn, tn), tn)
    xo_cols = cols if final else slice(None)
    for r in range(tm // rc):
        rows = slice(r * rc, (r + 1) * rc)
        rowid = t * tm + r * rc + lax.broadcasted_iota(jnp.int32, (rc, 1), 0)
        g2 = jnp.where(rowid >= n_lat, mc_ref[0, 5:6, cols], ml_ref[0, 5:6, cols])
        o_ref[0, rows, xo_cols] = x_ref[0, rows, xo_cols] + g2 * _dot(u_ref[0, rows, :], w2_ref[...])

    if final:
        @pl.when(j == pl.num_programs(2) - 1)
        def _():
            for r in range(tm // rc):
                rows = slice(r * rc, (r + 1) * rc)
                o_ref[0, rows, :] = _rms(o_ref[0, rows, :], fg_ref[...])


def _ffn_down(u, xn, mod, w2, fg, n_lat, final, layer):
    b, n_rows, d = xn.shape
    tn = 512
    dff = w2.shape[1]
    if final:
        tm = n_rows // 8
        xo_spec = pl.BlockSpec((1, tm, d), lambda bi, t, j: (bi, t, 0))
    else:
        tm = n_rows // 4
        xo_spec = pl.BlockSpec((1, tm, tn), lambda bi, t, j: (bi, t, j))
    kern = functools.partial(_ffn_down_kernel, n_lat=n_lat, rc=tm // (2 if final else 4), final=final)
    ml, mc = _mod_specs(layer, b, d)
    return pl.pallas_call(
        kern,
        grid=(b, n_rows // tm, d // tn),
        in_specs=[pl.BlockSpec((1, tm, dff), lambda bi, t, j: (bi, t, 0)),
                  xo_spec, ml, mc,
                  pl.BlockSpec((None, dff, tn), lambda bi, t, j: (layer, 0, j)),
                  pl.BlockSpec((1, d), lambda bi, t, j: (0, 0))],
        out_specs=xo_spec,
        out_shape=jax.ShapeDtypeStruct((b, n_rows, d), F32),
        compiler_params=_params(("arbitrary", "arbitrary", "arbitrary")),
        name="ffn_down",
    )(u, xn, mod, mod, w2, fg)


def _rope_tables(n_lat, n_ctx):
    rows = n_lat // GRID_W
    f32 = np.float32
    rowp = np.repeat(np.arange(rows, dtype=f32), GRID_W)
    colp = np.tile(np.arange(GRID_W, dtype=f32), rows)
    inv = (f32(ROPE_BASE) ** (-np.arange(ROPE_FREQS, dtype=f32) * f32(2.0) / f32(2 * ROPE_FREQS))).astype(f32)
    ar, ac = (rowp[:, None] * inv).astype(f32), (colp[:, None] * inv).astype(f32)
    zeros = np.zeros((n_lat, HEAD_W - MLA_ROPE), f32)
    cos_l = np.concatenate([np.cos(ar), np.cos(ar), np.cos(ac), np.cos(ac), zeros], axis=1)
    sin_l = np.concatenate([-np.sin(ar), np.sin(ar), -np.sin(ac), np.sin(ac), zeros], axis=1)
    cos_c = np.concatenate([np.ones((n_ctx, MLA_ROPE), f32), np.zeros((n_ctx, HEAD_W - MLA_ROPE), f32)], axis=1)
    cos_t = np.concatenate([cos_l, cos_c], axis=0).astype(f32)
    sin_t = np.concatenate([sin_l, np.zeros_like(cos_c)], axis=0).astype(f32)
    return jnp.asarray(cos_t), jnp.asarray(sin_t)


def kernel(x, c, ctx, c_ctx, norm1_g, w_mod, b_mod, w_in, gla_fg_up, gla_fg_b, gla_onorm_g, conf_dw, conf_dw_b, conf_ln_g, conf_ln_b, sc_dw, mla_q_norm_g, mla_kv_norm_g, mla_w_uq, mla_w_ukv, w_out, norm2_g, ffn_w1, ffn_w3, ffn_w2, final_norm_g):
    bsz, n_lat, d = x.shape
    n_ctx = ctx.shape[1]
    depth = w_in.shape[0]
    w = MIX_W

    xx = jnp.concatenate([x, ctx], axis=1)
    c3 = jnp.concatenate([c, c_ctx[None, :]], axis=0)
    mod = _modulation(jnp.broadcast_to(c3[:, :, None], (bsz + 1, d, 128)), w_mod, b_mod)
    mod = mod.reshape(depth, MOD_ROWS, 6, d)
    cos_t, sin_t = _rope_tables(n_lat, n_ctx)

    o_gla, o_lr, o_conv, o_mla = 0, 4 * w, 4 * w + 2 * GLA_GATE_RANK, 4 * w + 2 * GLA_GATE_RANK + 5 * w
    w_conv = w_in[:, :, o_conv:o_mla].astype(BF16)
    w_gla = w_in[:, :, o_gla:o_lr].astype(BF16)
    w_small = jnp.concatenate([w_in[:, :, o_mla:], w_in[:, :, o_lr:o_conv],
                               jnp.zeros((depth, d, ZS_W - (w_in.shape[2] - o_mla) - 2 * GLA_GATE_RANK), F32)],
                              axis=-1).astype(BF16)
    lr0 = MLA_ROPE
    fw = jnp.stack([jnp.pad(gla_fg_up[:, dd], ((0, 0), (lr0 + dd * GLA_GATE_RANK,
                                                        HEAD_W - lr0 - (dd + 1) * GLA_GATE_RANK), (0, 0)))
                    for dd in range(2)], axis=1)
    fw = fw.reshape(depth, 2, HEAD_W, HEADS, HEAD_W).transpose(0, 1, 3, 2, 4)
    fb = gla_fg_b.reshape(depth, 2, HEADS, 1, HEAD_W)
    wq = mla_w_uq.reshape(depth, MLA_Q_RANK, HEADS, HEAD_W + MLA_ROPE)
    wq = jnp.pad(wq, ((0, 0), (0, 0), (0, 0), (0, HEAD_W - MLA_ROPE))).reshape(depth, MLA_Q_RANK, -1).astype(BF16)
    wkv = mla_w_ukv.astype(BF16)
    wo = w_out.astype(BF16)
    w2 = ffn_w2.astype(BF16)
    row = lambda a: a.reshape(1, -1)

    for i in range(depth):
        last = i == depth - 1
        z, zl, qq, kk, vv = _win(xx, mod, row(norm1_g[i]), w_conv, w_gla, w_small, row(mla_q_norm_g[i]),
                                 row(mla_kv_norm_g[i]), wq, wkv, cos_t, sin_t, n_lat, i)
        gla = _gla(z, zl, fw, fb[i], row(gla_onorm_g[i]), n_lat, i)
        cs = _conv_mixers(z, conf_dw[i], row(conf_dw_b[i]), row(conf_ln_g[i]), row(conf_ln_b[i]), sc_dw[i], n_lat)
        mla = _attention(qq, kk, vv, n_lat, not last)
        n_rows = n_lat if last else n_lat + n_ctx
        xn, h2 = _wout(gla, cs, mla, xx, mod, row(norm2_g[i]), wo, n_lat, n_rows, i)
        u = _ffn_up(h2, ffn_w1, ffn_w3, i)
        xx = _ffn_down(u, xn, mod, w2, row(final_norm_g), n_lat, last, i)
    return xx
```

```python
import functools
import math

import jax
import jax.numpy as jnp
import numpy as np
from jax import lax
from jax.experimental import pallas as pl
from jax.experimental.pallas import tpu as pltpu

F32 = jnp.float32
BF16 = jnp.bfloat16

EPS = 1e-6
GRID_W = 64
HEADS = 4
HEAD_W = 128
GLA_GATE_RANK = 16
GLA_GATE_NORM = 16.0
GLA_CHUNK = 64
GLA_DIAG = 16
GLA_SAFE_DECAY = 60.0
CONF_KERNEL = 31
SC_KERNEL = 3
MIX_W = 512
MLA_ROPE = 64
MLA_Q_RANK = 384
MLA_KV_RANK = 128
ROPE_FREQS = 16
ROPE_BASE = 10000.0
Q_BLOCK = 2048
Q_SUB = 256
SEQ_BLOCK = 256
HALO = 16
SUBLANES = 8
ZC_W = 5 * MIX_W
ZG_W = 4 * MIX_W
MOD_ROWS = 8
ZS_W = 640
VMEM_LIMIT_V7X = 56 * 1024 * 1024


def _dot(a, b):
    return jnp.dot(a, b, preferred_element_type=F32)


def _dot_nt(a, b):
    return lax.dot_general(a, b, (((1,), (1,)), ((), ())), preferred_element_type=F32)


def _dot_tn(a, b):
    return lax.dot_general(a, b, (((0,), (0,)), ((), ())), preferred_element_type=F32)


def _split3(x):
    h1 = x.astype(BF16)
    r1 = x - h1.astype(F32)
    h2 = r1.astype(BF16)
    h3 = (r1 - h2.astype(F32)).astype(BF16)
    return h1, h2, h3


def _sigmoid(x):
    return 1.0 / (1.0 + jnp.exp(-x))


def _silu(x):
    return x * _sigmoid(x)


def _rms(x, g):
    return x * lax.rsqrt(jnp.mean(x * x, axis=-1, keepdims=True) + EPS) * g


def _params(sem):
    return pltpu.CompilerParams(dimension_semantics=sem, vmem_limit_bytes=VMEM_LIMIT_V7X)


def _mod_kernel(c_ref, w_ref, b_ref, o_ref, act_sc):
    tn = w_ref.shape[2]
    rows = []
    nrow = c_ref.shape[0]

    @pl.when((pl.program_id(0) == 0) & (pl.program_id(1) == 0))
    def _():
        for m in range(nrow):
            act_sc[m] = _silu(c_ref[m])

    kc = 8 * SUBLANES
    ncb = tn // 128
    acc = [[jnp.zeros((SUBLANES, 128), F32) for _ in range(ncb)] for _ in range(nrow)]
    for k0 in range(0, w_ref.shape[1], kc):
        wk = w_ref[0, k0:k0 + kc, :]
        for m in range(nrow):
            am = act_sc[m, k0:k0 + kc, :]
            for cb in range(ncb):
                part = wk[:, cb * 128:(cb + 1) * 128] * am
                acc[m][cb] = acc[m][cb] + jnp.sum(part.reshape(kc // SUBLANES, SUBLANES, 128), axis=0)
    for m in range(nrow):
        out_m = jnp.concatenate([jnp.sum(a, axis=0, keepdims=True) for a in acc[m]], axis=1)
        rows.append(out_m + b_ref[0])
    rows.append(jnp.zeros((MOD_ROWS - nrow, tn), F32))
    o_ref[0] = jnp.concatenate(rows, axis=0)


def _modulation(cb, w_mod, b_mod):
    depth, d, n6 = w_mod.shape
    tn = 1024
    return pl.pallas_call(
        _mod_kernel,
        grid=(depth, n6 // tn),
        in_specs=[pl.BlockSpec(cb.shape, lambda i, j: (0, 0, 0)),
                  pl.BlockSpec((1, d, tn), lambda i, j: (i, 0, j)),
                  pl.BlockSpec((1, 1, tn), lambda i, j: (i, 0, j))],
        out_specs=pl.BlockSpec((1, MOD_ROWS, tn), lambda i, j: (i, 0, j)),
        out_shape=jax.ShapeDtypeStruct((depth, MOD_ROWS, n6), F32),
        scratch_shapes=[pltpu.VMEM(cb.shape, F32)],
        compiler_params=_params(("arbitrary", "arbitrary")),
        name="modulation",
    )(cb, w_mod, b_mod.reshape(depth, 1, n6))


def _mod_specs(layer, b, d):
    lat = lambda bi, *_: (layer, bi, 0, 0)
    ctx = lambda *_: (layer, b, 0, 0)
    return (pl.BlockSpec((None, 1, 6, d), lat), pl.BlockSpec((None, 1, 6, d), ctx))


def _win_kernel(x_ref, ml_ref, mc_ref, g_ref, wc_ref, wg_ref, ws_ref, qg_ref, kvg_ref, wq_ref, wkv_ref,
                cos_ref, sin_ref, z_ref, zl_ref, q_ref, k_ref, v_ref, *, n_lat, rc):
    t = pl.program_id(1)
    tm = x_ref.shape[1]
    scale = (HEAD_W + MLA_ROPE) ** -0.5 * math.log2(math.e)
    lane = lax.broadcasted_iota(jnp.int32, (1, HEAD_W), 1)
    lower = (lane & 31) < 16
    for r in range(tm // rc):
        rows = slice(r * rc, (r + 1) * rc)
        y = _rms(x_ref[0, rows, :], g_ref[...])
        rowid = t * tm + r * rc + lax.broadcasted_iota(jnp.int32, (rc, 1), 0)
        is_ctx = rowid >= n_lat
        sh = jnp.where(is_ctx, mc_ref[0, 0:1, :], ml_ref[0, 0:1, :])
        sc = jnp.where(is_ctx, mc_ref[0, 1:2, :], ml_ref[0, 1:2, :])
        h = (y * (1.0 + sc) + sh).astype(BF16)
        zs = _dot(h, ws_ref[...])
        z0 = 0
        for w_ref in (wc_ref, wg_ref):
            nw = w_ref.shape[1]
            for n0 in range(0, nw, nw // 2):
                z_ref[0, rows, z0 + n0:z0 + n0 + nw // 2] = _dot(h, w_ref[:, n0:n0 + nw // 2]).astype(BF16)
            z0 += nw

        krb = zs[:, MLA_Q_RANK + MLA_KV_RANK:ZS_W]
        zl_ref[0, rows, :] = krb
        q = _dot(_rms(zs[:, 0:MLA_Q_RANK], qg_ref[...]).astype(BF16), wq_ref[...])
        kv = _dot(_rms(zs[:, MLA_Q_RANK:MLA_Q_RANK + MLA_KV_RANK], kvg_ref[...]).astype(BF16), wkv_ref[...])
        cs = cos_ref[rows, :]
        sn = sin_ref[rows, :]

        def rope(tt):
            swapped = jnp.where(lower, pltpu.roll(tt, HEAD_W - 16, 1), pltpu.roll(tt, 16, 1))
            return tt * cs + swapped * sn

        krr = rope(krb).astype(BF16)
        for hd in range(HEADS):
            c0 = 2 * HEAD_W * hd
            q_ref[0, rows, c0:c0 + HEAD_W] = (q[:, c0:c0 + HEAD_W] * scale).astype(BF16)
            q_ref[0, rows, c0 + HEAD_W:c0 + 2 * HEAD_W] = (
                rope(q[:, c0 + HEAD_W:c0 + 2 * HEAD_W]) * scale).astype(BF16)
            k_ref[0, rows, c0:c0 + HEAD_W] = kv[:, c0:c0 + HEAD_W].astype(BF16)
            k_ref[0, rows, c0 + HEAD_W:c0 + 2 * HEAD_W] = krr
            v_ref[0, rows, HEAD_W * hd:HEAD_W * (hd + 1)] = kv[:, c0 + HEAD_W:c0 + 2 * HEAD_W].astype(BF16)


def _win(xx, mod, g, w_conv, w_gla, w_small, qg, kvg, wq, wkv, cos_t, sin_t, n_lat, layer):
    b, nt, d = xx.shape
    tm = nt // 8
    nw = w_conv.shape[2] + w_gla.shape[2]
    kw = 2 * HEAD_W * HEADS
    kern = functools.partial(_win_kernel, n_lat=n_lat, rc=tm // 2)
    ml, mc = _mod_specs(layer, b, d)
    resident = lambda a: pl.BlockSpec((None,) + a.shape[1:], lambda bi, t: (layer, 0, 0),
                                      pipeline_mode=pl.Buffered(1))
    vec = lambda a: pl.BlockSpec(a.shape, lambda bi, t: (0, 0))
    tile = lambda wd: pl.BlockSpec((1, tm, wd), lambda bi, t: (bi, t, 0))
    return pl.pallas_call(
        kern,
        grid=(b, nt // tm),
        in_specs=[tile(d), ml, mc, vec(g), resident(w_conv), resident(w_gla), resident(w_small),
                  vec(qg), vec(kvg), resident(wq), resident(wkv),
                  pl.BlockSpec((tm, HEAD_W), lambda bi, t: (t, 0)),
                  pl.BlockSpec((tm, HEAD_W), lambda bi, t: (t, 0))],
        out_specs=[tile(nw), tile(HEAD_W), tile(kw), tile(kw), tile(MIX_W)],
        out_shape=[jax.ShapeDtypeStruct((b, nt, nw), BF16),
                   jax.ShapeDtypeStruct((b, nt, HEAD_W), F32),
                   jax.ShapeDtypeStruct((b, nt, kw), BF16),
                   jax.ShapeDtypeStruct((b, nt, kw), BF16),
                   jax.ShapeDtypeStruct((b, nt, MIX_W), BF16)],
        compiler_params=_params(("arbitrary", "arbitrary")),
        name="norm1_w_in",
    )(xx, mod, mod, g, w_conv, w_gla, w_small, qg, kvg, wq, wkv, cos_t, sin_t)


def _gla_kernel(q_ref, k_ref, v_ref, g_ref, zs_ref, fw_ref, fb_ref, on_ref, o_ref,
                bc_sc, qin_sc, kv_sc, er_sc, oacc_sc, oint_sc, msk_sc, tri_sc, sel_sc, *, n_lat):
    n_tot = q_ref.shape[1]
    blk = SEQ_BLOCK
    cpb = blk // GLA_CHUNK
    nblk = n_tot // blk
    nchunk = n_tot // GLA_CHUNK
    lat_chunks = n_lat // GLA_CHUNK
    ctx_chunks = nchunk - lat_chunks
    ndb = blk // GLA_DIAG
    qscale = HEAD_W ** -0.5
    unroll = max(u for u in range(1, 18) if nchunk % u == 0)
    group_pro, group_intra = 8, 4

    ri = lax.broadcasted_iota(jnp.int32, (blk, blk), 0)
    ci = lax.broadcasted_iota(jnp.int32, (blk, blk), 1)
    same64 = (ri >> 6) == (ci >> 6)
    same16 = (ri >> 4) == (ci >> 4)
    one = lambda m: jnp.where(m, 1.0, 0.0).astype(F32)
    msk_sc[0] = one(same64)
    msk_sc[1] = one((ri >> 5) == (ci >> 5))
    msk_sc[2] = one(same16 & ((ri & 15) >= (ci & 15)))
    msk_sc[3] = one(same16 & ((ri & 15) <= (ci & 15)))
    msk_sc[4] = one(same64 & (ri >= ci))
    msk_sc[5] = one(same64 & (ri <= ci))
    tri_sc[0] = msk_sc[4].astype(BF16)
    tri_sc[1] = msk_sc[5].astype(BF16)
    cj = lax.broadcasted_iota(jnp.int32, (HEAD_W, blk), 1) & (GLA_DIAG - 1)
    for jj in range(GLA_DIAG):
        sel_sc[jj * HEAD_W:(jj + 1) * HEAD_W, :] = one(cj == jj).astype(BF16)

    row = lax.broadcasted_iota(jnp.int32, (blk, 1), 0)
    half0 = (row & 63) < 32
    quart0 = (row & 31) < 16

    def block_rows(i, nb=1):
        start = i * blk if isinstance(i, int) else pl.multiple_of(i * blk, blk)
        return pl.ds(start, nb * blk)

    def paired(fn, group):
        def body(p, carry):
            fn(group * p, group)
            return carry
        lax.fori_loop(0, nblk // group, body, 0)
        if nblk % group:
            fn(nblk - nblk % group, nblk % group)

    fsplit = [_split3(fw_ref[d, 0]) for d in range(2)]
    fcat = jnp.concatenate([jnp.concatenate([fh, fh, fl], axis=0) for fh, fl, _ in fsplit], axis=1)

    def prologue(i0, nb):
        rows_l = [block_rows(i0 + u) for u in range(nb)]
        zcats = []
        for rows in rows_l:
            zh, zl, _ = _split3(zs_ref[0, rows, :])
            zcats.append(jnp.concatenate([zh, zl, zh], axis=1))
        xg2s = [_dot(zc, fcat) for zc in zcats]
        logds = []
        for xg2 in xg2s:
            for d in range(2):
                xg = xg2[:, d * HEAD_W:(d + 1) * HEAD_W] + fb_ref[d, 0]
                logd = (jnp.minimum(xg, 0.0) - jnp.log1p(jnp.exp(-jnp.abs(xg)))) * (1.0 / GLA_GATE_NORM)
                logds.append(jnp.concatenate(_split3(logd)[:2], axis=1))
        c2s = [_dot(tri_sc[n % 2], ld) for n, ld in enumerate(logds)]
        for n, c2 in enumerate(c2s):
            bc_sc[n % 2, rows_l[n // 2], :] = c2[:, 0:HEAD_W] + c2[:, HEAD_W:]

    paired(prologue, group_pro)

    def run_direction(d):
        rev = d == 1

        def load_block(i):
            rows = block_rows(i)
            b = bc_sc[d, rows, :]
            q = q_ref[0, rows, :].astype(F32) * qscale
            k = k_ref[0, rows, :].astype(F32)
            v = v_ref[0, rows, :]
            b4 = b.reshape(cpb, GLA_CHUNK, HEAD_W)
            blast = b4[:, 0:1, :] if rev else b4[:, GLA_CHUNK - 1:GLA_CHUNK, :]
            ebl = jnp.exp(blast)
            kst = (k.reshape(cpb, GLA_CHUNK, HEAD_W) * jnp.exp(blast - b4)).astype(BF16)
            v4 = v.reshape(cpb, GLA_CHUNK, HEAD_W)
            for c in range(cpb):
                kv_sc[i * cpb + c] = _dot_tn(v4[c], kst[c])
                er_sc[i * cpb + c] = jnp.broadcast_to(ebl[c], (SUBLANES, HEAD_W))
            qin_sc[rows, :] = (q * jnp.exp(b)).astype(BF16)
            return rows, b, q, k, v

        def store_block(rows, scores, v):
            o_blk = _dot(scores.astype(BF16), v)
            if rev:
                oacc_sc[rows, :] += o_blk
            else:
                oacc_sc[rows, :] = o_blk

        def intra_plain(i0, nb):
            prep = []
            for u in range(nb):
                i = i0 + u
                rows = block_rows(i)
                b = bc_sc[d, rows, :]
                q = q_ref[0, rows, :].astype(F32) * qscale
                k = k_ref[0, rows, :].astype(F32)
                b4 = b.reshape(cpb, GLA_CHUNK, HEAD_W)
                ebl = jnp.exp(b4[:, 0:1, :] if rev else b4[:, GLA_CHUNK - 1:GLA_CHUNK, :])
                kt = k * jnp.exp(-b)
                kst = (kt.reshape(cpb, GLA_CHUNK, HEAD_W) * ebl).astype(BF16)
                qin = (q * jnp.exp(b)).astype(BF16)
                qin_sc[rows, :] = qin
                prep.append((i, rows, qin, kt.astype(BF16), kst, ebl))
            scores = [_dot_nt(qin, ktb) for _, _, qin, ktb, _, _ in prep]
            for i, rows, _, _, kst, ebl in prep:
                v4 = v_ref[0, rows, :].reshape(cpb, GLA_CHUNK, HEAD_W)
                for c in range(cpb):
                    kv_sc[i * cpb + c] = _dot_tn(v4[c], kst[c])
                    er_sc[i * cpb + c] = jnp.broadcast_to(ebl[c], (SUBLANES, HEAD_W))
            probs = [(s * msk_sc[5 if rev else 4]).astype(BF16) for s in scores]
            for (_, rows, _, _, _, _), p in zip(prep, probs):
                store_block(rows, p, v_ref[0, rows, :])

        def intra_robust(i):
            rows, b, q, k, v = load_block(i)
            b4 = b.reshape(cpb, GLA_CHUNK, HEAD_W)
            r1 = b4[:, 32:33, :] if rev else b4[:, 31:32, :]
            r1 = jnp.broadcast_to(r1, b4.shape).reshape(blk, HEAD_W)
            qsel, ksel = (half0, ~half0) if rev else (~half0, half0)
            qa = q * jnp.where(qsel, jnp.exp(jnp.minimum(b - r1, 0.0)), 0.0)
            ka = k * jnp.where(ksel, jnp.exp(jnp.minimum(r1 - b, 0.0)), 0.0)
            s1 = _dot_nt(qa.astype(BF16), ka.astype(BF16))
            b8 = b.reshape(blk // 32, 32, HEAD_W)
            r2 = b8[:, 16:17, :] if rev else b8[:, 15:16, :]
            r2 = jnp.broadcast_to(r2, b8.shape).reshape(blk, HEAD_W)
            qsel, ksel = (quart0, ~quart0) if rev else (~quart0, quart0)
            qb = q * jnp.where(qsel, jnp.exp(jnp.minimum(b - r2, 0.0)), 0.0)
            kb = k * jnp.where(ksel, jnp.exp(jnp.minimum(r2 - b, 0.0)), 0.0)
            s2 = _dot_nt(qb.astype(BF16), kb.astype(BF16))
            q16 = q.reshape(ndb, GLA_DIAG, HEAD_W)
            k16 = k.reshape(ndb, GLA_DIAG, HEAD_W)
            b16 = b.reshape(ndb, GLA_DIAG, HEAD_W)
            half = GLA_DIAG // 2
            terms = []
            for jj in range(GLA_DIAG):
                bj = b16[:, jj:jj + 1, :]
                kj = k16[:, jj:jj + 1, :]
                need = ((True, jj >= half) if rev else (jj < half, True))
                parts = []
                for hsel, needed in zip((slice(0, half), slice(half, GLA_DIAG)), need):
                    if needed:
                        e = jnp.exp(jnp.minimum(b16[:, hsel, :] - bj, 0.0))
                        parts.append(q16[:, hsel, :] * kj * e)
                    else:
                        parts.append(jnp.zeros((ndb, half, HEAD_W), F32))
                terms.append(jnp.concatenate(parts, axis=1).reshape(blk, HEAD_W).astype(BF16))
            sdiag = _dot(jnp.concatenate(terms, axis=1), sel_sc[...])
            store_block(rows, s1 * msk_sc[0] + s2 * msk_sc[1] + sdiag * msk_sc[3 if rev else 2], v)

        def intra(i0, nb):
            plain_ok = jnp.min(bc_sc[d, block_rows(i0, nb), :]) >= -GLA_SAFE_DECAY

            @pl.when(plain_ok)
            def _():
                intra_plain(i0, nb)

            @pl.when(jnp.logical_not(plain_ok))
            def _():
                for u in range(nb):
                    intra_robust(i0 + u)

        paired(intra, group_intra)

        def inter(m, st):
            for u in range(unroll):
                n = m * unroll + u
                if rev:
                    cid = jnp.where(n < ctx_chunks, lat_chunks + (ctx_chunks - 1 - n),
                                    lat_chunks - 1 - (n - ctx_chunks))
                else:
                    cid = jnp.where(n < ctx_chunks, lat_chunks + n, n - ctx_chunks)
                rows = pl.ds(pl.multiple_of(cid * GLA_CHUNK, GLA_CHUNK), GLA_CHUNK)
                oint_sc[d, rows, :] = _dot_nt(qin_sc[rows, :], st.astype(BF16))
                st = st * er_sc[cid][0:1, :] + kv_sc[cid]
            return st

        lax.fori_loop(0, nchunk // unroll, inter, jnp.zeros((HEAD_W, HEAD_W), F32))

    run_direction(0)
    run_direction(1)

    def finish(i, carry):
        rows = block_rows(i)
        y = _rms(oacc_sc[rows, :] + oint_sc[0, rows, :] + oint_sc[1, rows, :], on_ref[...])
        o_ref[0, rows, :] = (y * _silu(g_ref[0, rows, :].astype(F32))).astype(BF16)
        return carry

    lax.fori_loop(0, nblk, finish, 0)


def _gla(z, zs, fw, fb, on, n_lat, layer):
    b, nt, _ = z.shape
    cb0 = ZC_W // HEAD_W
    seq = lambda off: pl.BlockSpec((1, nt, HEAD_W), lambda bi, h: (bi, 0, cb0 + off * HEADS + h))
    kern = functools.partial(_gla_kernel, n_lat=n_lat)
    return pl.pallas_call(
        kern,
        grid=(b, HEADS),
        in_specs=[seq(0), seq(1), seq(2), seq(3),
                  pl.BlockSpec((1, nt, HEAD_W), lambda bi, h: (bi, 0, 0)),
                  pl.BlockSpec((None, 2, 1, HEAD_W, HEAD_W), lambda bi, h: (layer, 0, h, 0, 0)),
                  pl.BlockSpec((2, 1, 1, HEAD_W), lambda bi, h: (0, h, 0, 0)),
                  pl.BlockSpec((1, HEAD_W), lambda bi, h: (0, 0))],
        out_specs=pl.BlockSpec((1, nt, HEAD_W), lambda bi, h: (bi, 0, h)),
        out_shape=jax.ShapeDtypeStruct((b, nt, MIX_W), BF16),
        scratch_shapes=[pltpu.VMEM((2, nt, HEAD_W), F32),
                        pltpu.VMEM((nt, HEAD_W), BF16),
                        pltpu.VMEM((nt // GLA_CHUNK, HEAD_W, HEAD_W), F32),
                        pltpu.VMEM((nt // GLA_CHUNK, SUBLANES, HEAD_W), F32),
                        pltpu.VMEM((nt, HEAD_W), F32),
                        pltpu.VMEM((2, nt, HEAD_W), F32),
                        pltpu.VMEM((6, SEQ_BLOCK, SEQ_BLOCK), F32),
                        pltpu.VMEM((2, SEQ_BLOCK, SEQ_BLOCK), BF16),
                        pltpu.VMEM((GLA_DIAG * HEAD_W, SEQ_BLOCK), BF16)],
        compiler_params=_params(("arbitrary", "arbitrary")),
        name="gla_mixer",
    )(z, z, z, z, zs, fw, fb, on)


def _conv_kernel(zc_ref, zp_ref, zn_ref, dw_ref, dwb_ref, lng_ref, lnb_ref, scw_ref, o_ref,
                 u_sc, m_sc, *, lat_tiles, n_tiles):
    t = pl.program_id(1)
    tl = zc_ref.shape[1]
    w = MIX_W
    first = (t == 0) | (t == lat_tiles)
    last = (t == lat_tiles - 1) | (t == n_tiles - 1)

    def glu(ref, rows):
        return ref[0, rows, 0:w].astype(F32) * _sigmoid(ref[0, rows, w:2 * w].astype(F32))

    def gated(ref, rows):
        return ref[0, rows, 3 * w:4 * w].astype(F32) * ref[0, rows, 4 * w:5 * w].astype(F32)

    halo = slice(0, HALO)
    u_sc[0, 0:HALO, :] = jnp.where(first, 0.0, glu(zp_ref, halo))
    m_sc[0:HALO, :] = jnp.where(first, 0.0, gated(zp_ref, halo))
    u_sc[0, HALO + tl:, :] = jnp.where(last, 0.0, glu(zn_ref, halo))
    m_sc[HALO + tl:, :] = jnp.where(last, 0.0, gated(zn_ref, halo))
    rc = 32
    for r in range(tl // rc):
        rows = slice(r * rc, (r + 1) * rc)
        u_sc[0, HALO + r * rc:HALO + (r + 1) * rc, :] = glu(zc_ref, rows)
        m_sc[HALO + r * rc:HALO + (r + 1) * rc, :] = gated(zc_ref, rows)
    span = tl + 2 * HALO - SUBLANES
    for s in range(1, SUBLANES):
        u_sc[s, 0:span, :] = u_sc[0, s:s + span, :]

    pad = CONF_KERNEL // 2
    for r in range(tl // rc):
        r0 = r * rc
        acc = jnp.zeros((rc, w), F32)
        for kk in range(CONF_KERNEL):
            off = HALO + r0 + kk - pad
            base = off - off % SUBLANES
            acc = acc + dw_ref[kk:kk + 1, :] * u_sc[off % SUBLANES, base:base + rc, :]
        acc = acc + dwb_ref[...]
        mu = jnp.mean(acc, axis=-1, keepdims=True)
        xc = acc - mu
        y = xc * lax.rsqrt(jnp.mean(xc * xc, axis=-1, keepdims=True) + EPS) * lng_ref[...] + lnb_ref[...]
        o_ref[0, r0:r0 + rc, 0:w] = _silu(y).astype(BF16)
        s = HALO + r0
        cv = (scw_ref[0:1, :] * m_sc[s - 1:s - 1 + rc, :] + scw_ref[1:2, :] * m_sc[s:s + rc, :]
              + scw_ref[2:3, :] * m_sc[s + 1:s + 1 + rc, :])
        o_ref[0, r0:r0 + rc, w:2 * w] = (zc_ref[0, r0:r0 + rc, 2 * w:3 * w].astype(F32) * cv).astype(BF16)


def _conv_mixers(z, dw, dwb, lng, lnb, scw, n_lat):
    b, nt, _ = z.shape
    tl = SEQ_BLOCK
    n_tiles = nt // tl
    hb = tl // HALO
    nhb = nt // HALO
    kern = functools.partial(_conv_kernel, lat_tiles=n_lat // tl, n_tiles=n_tiles)
    vec = lambda n: pl.BlockSpec((n, MIX_W), lambda bi, t: (0, 0))
    return pl.pallas_call(
        kern,
        grid=(b, n_tiles),
        in_specs=[pl.BlockSpec((1, tl, ZC_W), lambda bi, t: (bi, t, 0)),
                  pl.BlockSpec((1, HALO, ZC_W), lambda bi, t: (bi, jnp.maximum(t * hb - 1, 0), 0)),
                  pl.BlockSpec((1, HALO, ZC_W), lambda bi, t: (bi, jnp.minimum((t + 1) * hb, nhb - 1), 0)),
                  vec(CONF_KERNEL), vec(1), vec(1), vec(1), vec(SC_KERNEL)],
        out_specs=pl.BlockSpec((1, tl, 2 * MIX_W), lambda bi, t: (bi, t, 0)),
        out_shape=jax.ShapeDtypeStruct((b, nt, 2 * MIX_W), BF16),
        scratch_shapes=[pltpu.VMEM((SUBLANES, tl + 2 * HALO, MIX_W), F32),
                        pltpu.VMEM((tl + 2 * HALO, MIX_W), F32)],
        compiler_params=_params(("arbitrary", "arbitrary")),
        name="conv_mixers",
    )(z, z, z, dw, dwb, lng, lnb, scw)


def _attn_kernel(q_ref, k_ref, v_ref, o_ref, *, sub):
    for r0 in range(0, q_ref.shape[1], sub):
        rows = slice(r0, r0 + sub)
        s = _dot_nt(q_ref[0, rows, :], k_ref[0])
        p = jnp.exp2(s - jnp.max(s, axis=-1, keepdims=True))
        denom = jnp.sum(p, axis=-1, keepdims=True)
        o_ref[0, rows, :] = (_dot(p.astype(BF16), v_ref[0]) / denom).astype(BF16)


def _attention(q, k, v, n_lat, need_ctx):
    b, nt, _ = q.shape
    kern = functools.partial(_attn_kernel, sub=Q_SUB)
    out = pl.pallas_call(
        kern,
        grid=(b, HEADS, n_lat // Q_BLOCK),
        in_specs=[pl.BlockSpec((1, Q_BLOCK, 2 * HEAD_W), lambda bi, h, qi: (bi, qi, h)),
                  pl.BlockSpec((1, nt, 2 * HEAD_W), lambda bi, h, qi: (bi, 0, h)),
                  pl.BlockSpec((1, nt, HEAD_W), lambda bi, h, qi: (bi, 0, h))],
        out_specs=pl.BlockSpec((1, Q_BLOCK, HEAD_W), lambda bi, h, qi: (bi, qi, h)),
        out_shape=jax.ShapeDtypeStruct((b, n_lat, MIX_W), BF16),
        compiler_params=_params(("arbitrary", "arbitrary", "arbitrary")),
        name="mla_attention",
    )(q, k, v)
    if not need_ctx:
        return out
    n_ctx = nt - n_lat
    cblk = n_lat // n_ctx
    ctx_spec = lambda wd: pl.BlockSpec((1, n_ctx, wd), lambda bi, h: (bi, cblk, h))
    out_ctx = pl.pallas_call(
        kern,
        grid=(b, HEADS),
        in_specs=[ctx_spec(2 * HEAD_W), ctx_spec(2 * HEAD_W), ctx_spec(HEAD_W)],
        out_specs=pl.BlockSpec((1, n_ctx, HEAD_W), lambda bi, h: (bi, 0, h)),
        out_shape=jax.ShapeDtypeStruct((b, n_ctx, MIX_W), BF16),
        compiler_params=_params(("arbitrary", "arbitrary")),
        name="mla_attention_ctx",
    )(q, k, v)
    return jnp.concatenate([out, out_ctx], axis=1)


def _wout_kernel(gla_ref, cs_ref, mla_ref, x_ref, ml_ref, mc_ref, g_ref, w_ref, xo_ref, h_ref,
                 *, n_lat, bounds):
    t = pl.program_id(1)
    tm = x_ref.shape[1]
    for r0, r1 in zip(bounds[:-1], bounds[1:]):
        rows = slice(r0, r1)
        rc = r1 - r0
        mix = jnp.concatenate([gla_ref[0, rows, :], cs_ref[0, rows, :], mla_ref[0, rows, :]], axis=1)
        o = _dot(mix, w_ref[...])
        rowid = t * tm + r0 + lax.broadcasted_iota(jnp.int32, (rc, 1), 0)
        is_ctx = rowid >= n_lat
        g1 = jnp.where(is_ctx, mc_ref[0, 2:3, :], ml_ref[0, 2:3, :])
        sh = jnp.where(is_ctx, mc_ref[0, 3:4, :], ml_ref[0, 3:4, :])
        sc = jnp.where(is_ctx, mc_ref[0, 4:5, :], ml_ref[0, 4:5, :])
        xn = x_ref[0, rows, :] + g1 * o
        xo_ref[0, rows, :] = xn
        h_ref[0, rows, :] = (_rms(xn, g_ref[...]) * (1.0 + sc) + sh).astype(BF16)


def _wout(gla, cs, mla, xx, mod, g, w, n_lat, n_rows, layer):
    b, _, d = xx.shape
    tm = n_rows // 8
    bounds = tuple(-(-(tm * c // 4) // 16) * 16 for c in range(4)) + (tm,)
    kern = functools.partial(_wout_kernel, n_lat=n_lat, bounds=bounds)
    tile = lambda wd: pl.BlockSpec((1, tm, wd), lambda bi, t: (bi, t, 0))
    ml, mc = _mod_specs(layer, b, d)
    return pl.pallas_call(
        kern,
        grid=(b, n_rows // tm),
        in_specs=[tile(MIX_W), tile(2 * MIX_W), tile(MIX_W), tile(d), ml, mc,
                  pl.BlockSpec((1, d), lambda bi, t: (0, 0)),
                  pl.BlockSpec((None,) + w.shape[1:], lambda bi, t: (layer, 0, 0))],
        out_specs=[tile(d), tile(d)],
        out_shape=[jax.ShapeDtypeStruct((b, n_rows, d), F32),
                   jax.ShapeDtypeStruct((b, n_rows, d), BF16)],
        compiler_params=_params(("arbitrary", "arbitrary")),
        name="w_out_norm2",
    )(gla, cs, mla, xx, mod, mod, g, w)


def _ffn_up_kernel(h_ref, w1_ref, w3_ref, u_ref, *, rc):
    tm = h_ref.shape[1]
    w1 = w1_ref[...].astype(BF16)
    w3 = w3_ref[...].astype(BF16)
    for r in range(tm // rc):
        rows = slice(r * rc, (r + 1) * rc)
        hh = h_ref[0, rows, :]
        u_ref[0, rows, :] = (_silu(_dot(hh, w1)) * _dot(hh, w3)).astype(BF16)


def _ffn_up(h2, w1, w3, layer):
    b, n_rows, d = h2.shape
    tm = n_rows // 2
    tf = 512
    dff = w1.shape[2]
    kern = functools.partial(_ffn_up_kernel, rc=tm // 8)
    wspec = pl.BlockSpec((None, d, tf), lambda bi, t, j: (layer, 0, j))
    return pl.pallas_call(
        kern,
        grid=(b, n_rows // tm, dff // tf),
        in_specs=[pl.BlockSpec((1, tm, d), lambda bi, t, j: (bi, t, 0)), wspec, wspec],
        out_specs=pl.BlockSpec((1, tm, tf), lambda bi, t, j: (bi, t, j)),
        out_shape=jax.ShapeDtypeStruct((b, n_rows, dff), BF16),
        compiler_params=_params(("arbitrary", "arbitrary", "arbitrary")),
        name="ffn_up",
    )(h2, w1, w3)


def _ffn_down_kernel(u_ref, x_ref, ml_ref, mc_ref, w2_ref, fg_ref, o_ref, *, n_lat, rc, final):
    t = pl.program_id(1)
    j = pl.program_id(2)
    tm = u_ref.shape[1]
    tn = w2_ref.shape[1]
    cols = pl.ds(pl.multiple_of(j * tn, tn), tn)
    xo_cols = cols if final else slice(None)
    for r in range(tm // rc):
        rows = slice(r * rc, (r + 1) * rc)
        rowid = t * tm + r * rc + lax.broadcasted_iota(jnp.int32, (rc, 1), 0)
        g2 = jnp.where(rowid >= n_lat, mc_ref[0, 5:6, cols], ml_ref[0, 5:6, cols])
        o_ref[0, rows, xo_cols] = x_ref[0, rows, xo_cols] + g2 * _dot(u_ref[0, rows, :], w2_ref[...])

    if final:
        @pl.when(j == pl.num_programs(2) - 1)
        def _():
            for r in range(tm // rc):
                rows = slice(r * rc, (r + 1) * rc)
                o_ref[0, rows, :] = _rms(o_ref[0, rows, :], fg_ref[...])


def _ffn_down(u, xn, mod, w2, fg, n_lat, final, layer):
    b, n_rows, d = xn.shape
    tn = 512
    dff = w2.shape[1]
    if final:
        tm = n_rows // 8
        xo_spec = pl.BlockSpec((1, tm, d), lambda bi, t, j: (bi, t, 0))
    else:
        tm = n_rows // 4
        xo_spec = pl.BlockSpec((1, tm, tn), lambda bi, t, j: (bi, t, j))
    kern = functools.partial(_ffn_down_kernel, n_lat=n_lat, rc=tm // (2 if final else 4), final=final)
    ml, mc = _mod_specs(layer, b, d)
    return pl.pallas_call(
        kern,
        grid=(b, n_rows // tm, d // tn),
        in_specs=[pl.BlockSpec((1, tm, dff), lambda bi, t, j: (bi, t, 0)),
                  xo_spec, ml, mc,
                  pl.BlockSpec((None, dff, tn), lambda bi, t, j: (layer, 0, j)),
                  pl.BlockSpec((1, d), lambda bi, t, j: (0, 0))],
        out_specs=xo_spec,
        out_shape=jax.ShapeDtypeStruct((b, n_rows, d), F32),
        compiler_params=_params(("arbitrary", "arbitrary", "arbitrary")),
        name="ffn_down",
    )(u, xn, mod, mod, w2, fg)


def _rope_tables(n_lat, n_ctx):
    rows = n_lat // GRID_W
    f32 = np.float32
    rowp = np.repeat(np.arange(rows, dtype=f32), GRID_W)
    colp = np.tile(np.arange(GRID_W, dtype=f32), rows)
    inv = (f32(ROPE_BASE) ** (-np.arange(ROPE_FREQS, dtype=f32) * f32(2.0) / f32(2 * ROPE_FREQS))).astype(f32)
    ar, ac = (rowp[:, None] * inv).astype(f32), (colp[:, None] * inv).astype(f32)
    zeros = np.zeros((n_lat, HEAD_W - MLA_ROPE), f32)
    cos_l = np.concatenate([np.cos(ar), np.cos(ar), np.cos(ac), np.cos(ac), zeros], axis=1)
    sin_l = np.concatenate([-np.sin(ar), np.sin(ar), -np.sin(ac), np.sin(ac), zeros], axis=1)
    cos_c = np.concatenate([np.ones((n_ctx, MLA_ROPE), f32), np.zeros((n_ctx, HEAD_W - MLA_ROPE), f32)], axis=1)
    cos_t = np.concatenate([cos_l, cos_c], axis=0).astype(f32)
    sin_t = np.concatenate([sin_l, np.zeros_like(cos_c)], axis=0).astype(f32)
    return jnp.asarray(cos_t), jnp.asarray(sin_t)


def kernel(x, c, ctx, c_ctx, norm1_g, w_mod, b_mod, w_in, gla_fg_up, gla_fg_b, gla_onorm_g, conf_dw, conf_dw_b, conf_ln_g, conf_ln_b, sc_dw, mla_q_norm_g, mla_kv_norm_g, mla_w_uq, mla_w_ukv, w_out, norm2_g, ffn_w1, ffn_w3, ffn_w2, final_norm_g):
    bsz, n_lat, d = x.shape
    n_ctx = ctx.shape[1]
    depth = w_in.shape[0]
    w = MIX_W

    xx = lax.dynamic_update_slice(jnp.pad(x, ((0, 0), (0, n_ctx), (0, 0))), ctx, (0, n_lat, 0))
    c3 = jnp.concatenate([c, c_ctx[None, :]], axis=0)
    mod = _modulation(jnp.broadcast_to(c3[:, :, None], (bsz + 1, d, 128)), w_mod, b_mod)
    mod = mod.reshape(depth, MOD_ROWS, 6, d)
    cos_t, sin_t = _rope_tables(n_lat, n_ctx)

    o_gla, o_lr, o_conv, o_mla = 0, 4 * w, 4 * w + 2 * GLA_GATE_RANK, 4 * w + 2 * GLA_GATE_RANK + 5 * w
    w_conv = w_in[:, :, o_conv:o_mla].astype(BF16)
    w_gla = w_in[:, :, o_gla:o_lr].astype(BF16)
    w_small = jnp.concatenate([w_in[:, :, o_mla:], w_in[:, :, o_lr:o_conv],
                               jnp.zeros((depth, d, ZS_W - (w_in.shape[2] - o_mla) - 2 * GLA_GATE_RANK), F32)],
                              axis=-1).astype(BF16)
    lr0 = MLA_ROPE
    fw = jnp.stack([jnp.pad(gla_fg_up[:, dd], ((0, 0), (lr0 + dd * GLA_GATE_RANK,
                                                        HEAD_W - lr0 - (dd + 1) * GLA_GATE_RANK), (0, 0)))
                    for dd in range(2)], axis=1)
    fw = fw.reshape(depth, 2, HEAD_W, HEADS, HEAD_W).transpose(0, 1, 3, 2, 4)
    fb = gla_fg_b.reshape(depth, 2, HEADS, 1, HEAD_W)
    wq = mla_w_uq.reshape(depth, MLA_Q_RANK, HEADS, HEAD_W + MLA_ROPE)
    wq = jnp.pad(wq, ((0, 0), (0, 0), (0, 0), (0, HEAD_W - MLA_ROPE))).reshape(depth, MLA_Q_RANK, -1).astype(BF16)
    wkv = mla_w_ukv.astype(BF16)
    wo = w_out.astype(BF16)
    w2 = ffn_w2.astype(BF16)
    row = lambda a: a.reshape(1, -1)

    for i in range(depth):
        last = i == depth - 1
        z, zl, qq, kk, vv = _win(xx, mod, row(norm1_g[i]), w_conv, w_gla, w_small, row(mla_q_norm_g[i]),
                                 row(mla_kv_norm_g[i]), wq, wkv, cos_t, sin_t, n_lat, i)
        gla = _gla(z, zl, fw, fb[i], row(gla_onorm_g[i]), n_lat, i)
        cs = _conv_mixers(z, conf_dw[i], row(conf_dw_b[i]), row(conf_ln_g[i]), row(conf_ln_b[i]), sc_dw[i], n_lat)
        mla = _attention(qq, kk, vv, n_lat, not last)
        n_rows = n_lat if last else n_lat + n_ctx
        xn, h2 = _wout(gla, cs, mla, xx, mod, row(norm2_g[i]), wo, n_lat, n_rows, i)
        u = _ffn_up(h2, ffn_w1, ffn_w3, i)
        xx = _ffn_down(u, xn, mod, w2, row(final_norm_g), n_lat, last, i)
    return xx
```

```python
import functools
import math

import jax
import jax.numpy as jnp
import numpy as np
from jax import lax
from jax.experimental import pallas as pl
from jax.experimental.pallas import tpu as pltpu

F32 = jnp.float32
BF16 = jnp.bfloat16

EPS = 1e-6
GRID_W = 64
HEADS = 4
HEAD_W = 128
GLA_GATE_RANK = 16
GLA_GATE_NORM = 16.0
GLA_CHUNK = 64
GLA_DIAG = 16
GLA_SAFE_DECAY = 60.0
CONF_KERNEL = 31
SC_KERNEL = 3
MIX_W = 512
MLA_ROPE = 64
MLA_Q_RANK = 384
MLA_KV_RANK = 128
ROPE_FREQS = 16
ROPE_BASE = 10000.0
Q_BLOCK = 2048
Q_SUB = 256
SEQ_BLOCK = 256
HALO = 16
SUBLANES = 8
ZC_W = 5 * MIX_W
ZG_W = 4 * MIX_W
MOD_ROWS = 8
ZS_W = 640
VMEM_LIMIT_V7X = 56 * 1024 * 1024


def _dot(a, b):
    return jnp.dot(a, b, preferred_element_type=F32)


def _dot_nt(a, b):
    return lax.dot_general(a, b, (((1,), (1,)), ((), ())), preferred_element_type=F32)


def _dot_tn(a, b):
    return lax.dot_general(a, b, (((0,), (0,)), ((), ())), preferred_element_type=F32)


def _split3(x):
    h1 = x.astype(BF16)
    r1 = x - h1.astype(F32)
    h2 = r1.astype(BF16)
    h3 = (r1 - h2.astype(F32)).astype(BF16)
    return h1, h2, h3


def _sigmoid(x):
    return 1.0 / (1.0 + jnp.exp(-x))


def _silu(x):
    return x * _sigmoid(x)


def _rms(x, g):
    return x * lax.rsqrt(jnp.mean(x * x, axis=-1, keepdims=True) + EPS) * g


def _params(sem):
    return pltpu.CompilerParams(dimension_semantics=sem, vmem_limit_bytes=VMEM_LIMIT_V7X)


def _mod_kernel(c_ref, w_ref, b_ref, o_ref, act_sc):
    tn = w_ref.shape[2]
    rows = []
    nrow = c_ref.shape[0]

    @pl.when((pl.program_id(0) == 0) & (pl.program_id(1) == 0))
    def _():
        for m in range(nrow):
            act_sc[m] = _silu(c_ref[m])

    kc = 8 * SUBLANES
    ncb = tn // 128
    acc = [[jnp.zeros((SUBLANES, 128), F32) for _ in range(ncb)] for _ in range(nrow)]
    for k0 in range(0, w_ref.shape[1], kc):
        wk = w_ref[0, k0:k0 + kc, :]
        for m in range(nrow):
            am = act_sc[m, k0:k0 + kc, :]
            for cb in range(ncb):
                part = wk[:, cb * 128:(cb + 1) * 128] * am
                acc[m][cb] = acc[m][cb] + jnp.sum(part.reshape(kc // SUBLANES, SUBLANES, 128), axis=0)
    for m in range(nrow):
        out_m = jnp.concatenate([jnp.sum(a, axis=0, keepdims=True) for a in acc[m]], axis=1)
        rows.append(out_m + b_ref[0])
    rows.append(jnp.zeros((MOD_ROWS - nrow, tn), F32))
    o_ref[0] = jnp.concatenate(rows, axis=0)


def _modulation(cb, w_mod, b_mod):
    depth, d, n6 = w_mod.shape
    tn = 1024
    return pl.pallas_call(
        _mod_kernel,
        grid=(depth, n6 // tn),
        in_specs=[pl.BlockSpec(cb.shape, lambda i, j: (0, 0, 0)),
                  pl.BlockSpec((1, d, tn), lambda i, j: (i, 0, j)),
                  pl.BlockSpec((1, 1, tn), lambda i, j: (i, 0, j))],
        out_specs=pl.BlockSpec((1, MOD_ROWS, tn), lambda i, j: (i, 0, j)),
        out_shape=jax.ShapeDtypeStruct((depth, MOD_ROWS, n6), F32),
        scratch_shapes=[pltpu.VMEM(cb.shape, F32)],
        compiler_params=_params(("arbitrary", "arbitrary")),
        name="modulation",
    )(cb, w_mod, b_mod.reshape(depth, 1, n6))


def _mod_specs(layer, b, d):
    lat = lambda bi, *_: (layer, bi, 0, 0)
    ctx = lambda *_: (layer, b, 0, 0)
    return (pl.BlockSpec((None, 1, 6, d), lat), pl.BlockSpec((None, 1, 6, d), ctx))


def _win_kernel(x_ref, ml_ref, mc_ref, g_ref, wc_ref, wg_ref, ws_ref, qg_ref, kvg_ref, wq_ref, wkv_ref,
                cos_ref, sin_ref, z_ref, zl_ref, q_ref, k_ref, v_ref, *, n_lat, rc):
    t = pl.program_id(1)
    tm = x_ref.shape[1]
    scale = (HEAD_W + MLA_ROPE) ** -0.5 * math.log2(math.e)
    lane = lax.broadcasted_iota(jnp.int32, (1, HEAD_W), 1)
    lower = (lane & 31) < 16
    for r in range(tm // rc):
        rows = slice(r * rc, (r + 1) * rc)
        y = _rms(x_ref[0, rows, :], g_ref[...])
        rowid = t * tm + r * rc + lax.broadcasted_iota(jnp.int32, (rc, 1), 0)
        is_ctx = rowid >= n_lat
        sh = jnp.where(is_ctx, mc_ref[0, 0:1, :], ml_ref[0, 0:1, :])
        sc = jnp.where(is_ctx, mc_ref[0, 1:2, :], ml_ref[0, 1:2, :])
        h = (y * (1.0 + sc) + sh).astype(BF16)
        zs = _dot_nt(h, ws_ref[...])
        z0 = 0
        for w_ref in (wc_ref, wg_ref):
            nw = w_ref.shape[0]
            for n0 in range(0, nw, nw // 2):
                z_ref[0, rows, z0 + n0:z0 + n0 + nw // 2] = _dot_nt(h, w_ref[n0:n0 + nw // 2, :]).astype(BF16)
            z0 += nw

        krb = zs[:, MLA_Q_RANK + MLA_KV_RANK:ZS_W]
        zl_ref[0, rows, :] = krb
        q = _dot(_rms(zs[:, 0:MLA_Q_RANK], qg_ref[...]).astype(BF16), wq_ref[...])
        kv = _dot(_rms(zs[:, MLA_Q_RANK:MLA_Q_RANK + MLA_KV_RANK], kvg_ref[...]).astype(BF16), wkv_ref[...])
        cs = cos_ref[rows, :]
        sn = sin_ref[rows, :]

        def rope(tt):
            swapped = jnp.where(lower, pltpu.roll(tt, HEAD_W - 16, 1), pltpu.roll(tt, 16, 1))
            return tt * cs + swapped * sn

        krr = rope(krb).astype(BF16)
        for hd in range(HEADS):
            c0 = 2 * HEAD_W * hd
            q_ref[0, rows, c0:c0 + HEAD_W] = (q[:, c0:c0 + HEAD_W] * scale).astype(BF16)
            q_ref[0, rows, c0 + HEAD_W:c0 + 2 * HEAD_W] = (
                rope(q[:, c0 + HEAD_W:c0 + 2 * HEAD_W]) * scale).astype(BF16)
            k_ref[0, rows, c0:c0 + HEAD_W] = kv[:, c0:c0 + HEAD_W].astype(BF16)
            k_ref[0, rows, c0 + HEAD_W:c0 + 2 * HEAD_W] = krr
            v_ref[0, rows, HEAD_W * hd:HEAD_W * (hd + 1)] = kv[:, c0 + HEAD_W:c0 + 2 * HEAD_W].astype(BF16)


def _win(xx, mod, g, w_conv, w_gla, w_small, qg, kvg, wq, wkv, cos_t, sin_t, n_lat, layer):
    b, nt, d = xx.shape
    tm = nt // 8
    nw = w_conv.shape[1] + w_gla.shape[1]
    kw = 2 * HEAD_W * HEADS
    kern = functools.partial(_win_kernel, n_lat=n_lat, rc=tm // 2)
    ml, mc = _mod_specs(layer, b, d)
    resident = lambda a: pl.BlockSpec((None,) + a.shape[1:], lambda bi, t: (layer, 0, 0),
                                      pipeline_mode=pl.Buffered(1))
    vec = lambda a: pl.BlockSpec(a.shape, lambda bi, t: (0, 0))
    tile = lambda wd: pl.BlockSpec((1, tm, wd), lambda bi, t: (bi, t, 0))
    return pl.pallas_call(
        kern,
        grid=(b, nt // tm),
        in_specs=[tile(d), ml, mc, vec(g), resident(w_conv), resident(w_gla), resident(w_small),
                  vec(qg), vec(kvg), resident(wq), resident(wkv),
                  pl.BlockSpec((tm, HEAD_W), lambda bi, t: (t, 0)),
                  pl.BlockSpec((tm, HEAD_W), lambda bi, t: (t, 0))],
        out_specs=[tile(nw), tile(HEAD_W), tile(kw), tile(kw), tile(MIX_W)],
        out_shape=[jax.ShapeDtypeStruct((b, nt, nw), BF16),
                   jax.ShapeDtypeStruct((b, nt, HEAD_W), F32),
                   jax.ShapeDtypeStruct((b, nt, kw), BF16),
                   jax.ShapeDtypeStruct((b, nt, kw), BF16),
                   jax.ShapeDtypeStruct((b, nt, MIX_W), BF16)],
        compiler_params=_params(("arbitrary", "arbitrary")),
        name="norm1_w_in",
    )(xx, mod, mod, g, w_conv, w_gla, w_small, qg, kvg, wq, wkv, cos_t, sin_t)


def _gla_kernel(q_ref, k_ref, v_ref, g_ref, zs_ref, fw_ref, fb_ref, on_ref, o_ref,
                bc_sc, qin_sc, kv_sc, er_sc, oacc_sc, oint_sc, msk_sc, tri_sc, sel_sc, *, n_lat):
    n_tot = q_ref.shape[1]
    blk = SEQ_BLOCK
    cpb = blk // GLA_CHUNK
    nblk = n_tot // blk
    nchunk = n_tot // GLA_CHUNK
    lat_chunks = n_lat // GLA_CHUNK
    ctx_chunks = nchunk - lat_chunks
    ndb = blk // GLA_DIAG
    qscale = HEAD_W ** -0.5
    unroll = max(u for u in range(1, 18) if nchunk % u == 0)
    group_pro, group_intra = 8, 4

    ri = lax.broadcasted_iota(jnp.int32, (blk, blk), 0)
    ci = lax.broadcasted_iota(jnp.int32, (blk, blk), 1)
    same64 = (ri >> 6) == (ci >> 6)
    same16 = (ri >> 4) == (ci >> 4)
    one = lambda m: jnp.where(m, 1.0, 0.0).astype(F32)
    msk_sc[0] = one(same64)
    msk_sc[1] = one((ri >> 5) == (ci >> 5))
    msk_sc[2] = one(same16 & ((ri & 15) >= (ci & 15)))
    msk_sc[3] = one(same16 & ((ri & 15) <= (ci & 15)))
    msk_sc[4] = one(same64 & (ri >= ci))
    msk_sc[5] = one(same64 & (ri <= ci))
    tri_sc[0] = msk_sc[4].astype(BF16)
    tri_sc[1] = msk_sc[5].astype(BF16)
    cj = lax.broadcasted_iota(jnp.int32, (HEAD_W, blk), 1) & (GLA_DIAG - 1)
    for jj in range(GLA_DIAG):
        sel_sc[jj * HEAD_W:(jj + 1) * HEAD_W, :] = one(cj == jj).astype(BF16)

    row = lax.broadcasted_iota(jnp.int32, (blk, 1), 0)
    half0 = (row & 63) < 32
    quart0 = (row & 31) < 16

    def block_rows(i, nb=1):
        start = i * blk if isinstance(i, int) else pl.multiple_of(i * blk, blk)
        return pl.ds(start, nb * blk)

    def paired(fn, group):
        def body(p, carry):
            fn(group * p, group)
            return carry
        lax.fori_loop(0, nblk // group, body, 0)
        if nblk % group:
            fn(nblk - nblk % group, nblk % group)

    fsplit = [_split3(fw_ref[d, 0]) for d in range(2)]
    fcat = jnp.concatenate([jnp.concatenate([fh, fh, fl], axis=0) for fh, fl, _ in fsplit], axis=1)

    def prologue(i0, nb):
        rows_l = [block_rows(i0 + u) for u in range(nb)]
        zcats = []
        for rows in rows_l:
            zh, zl, _ = _split3(zs_ref[0, rows, :])
            zcats.append(jnp.concatenate([zh, zl, zh], axis=1))
        xg2s = [_dot(zc, fcat) for zc in zcats]
        logds = []
        for xg2 in xg2s:
            for d in range(2):
                xg = xg2[:, d * HEAD_W:(d + 1) * HEAD_W] + fb_ref[d, 0]
                logd = (jnp.minimum(xg, 0.0) - jnp.log1p(jnp.exp(-jnp.abs(xg)))) * (1.0 / GLA_GATE_NORM)
                logds.append(jnp.concatenate(_split3(logd)[:2], axis=1))
        c2s = [_dot(tri_sc[n % 2], ld) for n, ld in enumerate(logds)]
        for n, c2 in enumerate(c2s):
            bc_sc[n % 2, rows_l[n // 2], :] = c2[:, 0:HEAD_W] + c2[:, HEAD_W:]

    paired(prologue, group_pro)

    def run_direction(d):
        rev = d == 1

        def load_block(i):
            rows = block_rows(i)
            b = bc_sc[d, rows, :]
            q = q_ref[0, rows, :].astype(F32) * qscale
            k = k_ref[0, rows, :].astype(F32)
            v = v_ref[0, rows, :]
            b4 = b.reshape(cpb, GLA_CHUNK, HEAD_W)
            blast = b4[:, 0:1, :] if rev else b4[:, GLA_CHUNK - 1:GLA_CHUNK, :]
            ebl = jnp.exp(blast)
            kst = (k.reshape(cpb, GLA_CHUNK, HEAD_W) * jnp.exp(blast - b4)).astype(BF16)
            v4 = v.reshape(cpb, GLA_CHUNK, HEAD_W)
            for c in range(cpb):
                kv_sc[i * cpb + c] = _dot_tn(v4[c], kst[c])
                er_sc[i * cpb + c] = jnp.broadcast_to(ebl[c], (SUBLANES, HEAD_W))
            qin_sc[rows, :] = (q * jnp.exp(b)).astype(BF16)
            return rows, b, q, k, v

        def store_block(rows, scores, v):
            o_blk = _dot(scores.astype(BF16), v)
            if rev:
                oacc_sc[rows, :] += o_blk
            else:
                oacc_sc[rows, :] = o_blk

        def intra_plain(i0, nb):
            prep = []
            for u in range(nb):
                i = i0 + u
                rows = block_rows(i)
                b = bc_sc[d, rows, :]
                q = q_ref[0, rows, :].astype(F32) * qscale
                k = k_ref[0, rows, :].astype(F32)
                b4 = b.reshape(cpb, GLA_CHUNK, HEAD_W)
                ebl = jnp.exp(b4[:, 0:1, :] if rev else b4[:, GLA_CHUNK - 1:GLA_CHUNK, :])
                kt = k * jnp.exp(-b)
                kst = (kt.reshape(cpb, GLA_CHUNK, HEAD_W) * ebl).astype(BF16)
                qin = (q * jnp.exp(b)).astype(BF16)
                qin_sc[rows, :] = qin
                prep.append((i, rows, qin, kt.astype(BF16), kst, ebl))
            scores = [_dot_nt(qin, ktb) for _, _, qin, ktb, _, _ in prep]
            for i, rows, _, _, kst, ebl in prep:
                v4 = v_ref[0, rows, :].reshape(cpb, GLA_CHUNK, HEAD_W)
                for c in range(cpb):
                    kv_sc[i * cpb + c] = _dot_tn(v4[c], kst[c])
                    er_sc[i * cpb + c] = jnp.broadcast_to(ebl[c], (SUBLANES, HEAD_W))
            probs = [(s * msk_sc[5 if rev else 4]).astype(BF16) for s in scores]
            for (_, rows, _, _, _, _), p in zip(prep, probs):
                store_block(rows, p, v_ref[0, rows, :])

        def intra_robust(i):
            rows, b, q, k, v = load_block(i)
            b4 = b.reshape(cpb, GLA_CHUNK, HEAD_W)
            r1 = b4[:, 32:33, :] if rev else b4[:, 31:32, :]
            r1 = jnp.broadcast_to(r1, b4.shape).reshape(blk, HEAD_W)
            qsel, ksel = (half0, ~half0) if rev else (~half0, half0)
            qa = q * jnp.where(qsel, jnp.exp(jnp.minimum(b - r1, 0.0)), 0.0)
            ka = k * jnp.where(ksel, jnp.exp(jnp.minimum(r1 - b, 0.0)), 0.0)
            s1 = _dot_nt(qa.astype(BF16), ka.astype(BF16))
            b8 = b.reshape(blk // 32, 32, HEAD_W)
            r2 = b8[:, 16:17, :] if rev else b8[:, 15:16, :]
            r2 = jnp.broadcast_to(r2, b8.shape).reshape(blk, HEAD_W)
            qsel, ksel = (quart0, ~quart0) if rev else (~quart0, quart0)
            qb = q * jnp.where(qsel, jnp.exp(jnp.minimum(b - r2, 0.0)), 0.0)
            kb = k * jnp.where(ksel, jnp.exp(jnp.minimum(r2 - b, 0.0)), 0.0)
            s2 = _dot_nt(qb.astype(BF16), kb.astype(BF16))
            q16 = q.reshape(ndb, GLA_DIAG, HEAD_W)
            k16 = k.reshape(ndb, GLA_DIAG, HEAD_W)
            b16 = b.reshape(ndb, GLA_DIAG, HEAD_W)
            half = GLA_DIAG // 2
            terms = []
            for jj in range(GLA_DIAG):
                bj = b16[:, jj:jj + 1, :]
                kj = k16[:, jj:jj + 1, :]
                need = ((True, jj >= half) if rev else (jj < half, True))
                parts = []
                for hsel, needed in zip((slice(0, half), slice(half, GLA_DIAG)), need):
                    if needed:
                        e = jnp.exp(jnp.minimum(b16[:, hsel, :] - bj, 0.0))
                        parts.append(q16[:, hsel, :] * kj * e)
                    else:
                        parts.append(jnp.zeros((ndb, half, HEAD_W), F32))
                terms.append(jnp.concatenate(parts, axis=1).reshape(blk, HEAD_W).astype(BF16))
            sdiag = _dot(jnp.concatenate(terms, axis=1), sel_sc[...])
            store_block(rows, s1 * msk_sc[0] + s2 * msk_sc[1] + sdiag * msk_sc[3 if rev else 2], v)

        def intra(i0, nb):
            plain_ok = jnp.min(bc_sc[d, block_rows(i0, nb), :]) >= -GLA_SAFE_DECAY

            @pl.when(plain_ok)
            def _():
                intra_plain(i0, nb)

            @pl.when(jnp.logical_not(plain_ok))
            def _():
                for u in range(nb):
                    intra_robust(i0 + u)

        paired(intra, group_intra)

        def inter(m, st):
            for u in range(unroll):
                n = m * unroll + u
                if rev:
                    cid = jnp.where(n < ctx_chunks, lat_chunks + (ctx_chunks - 1 - n),
                                    lat_chunks - 1 - (n - ctx_chunks))
                else:
                    cid = jnp.where(n < ctx_chunks, lat_chunks + n, n - ctx_chunks)
                rows = pl.ds(pl.multiple_of(cid * GLA_CHUNK, GLA_CHUNK), GLA_CHUNK)
                oint_sc[d, rows, :] = _dot_nt(qin_sc[rows, :], st.astype(BF16))
                st = st * er_sc[cid][0:1, :] + kv_sc[cid]
            return st

        lax.fori_loop(0, nchunk // unroll, inter, jnp.zeros((HEAD_W, HEAD_W), F32))

    run_direction(0)
    run_direction(1)

    def finish(i, carry):
        rows = block_rows(i)
        y = _rms(oacc_sc[rows, :] + oint_sc[0, rows, :] + oint_sc[1, rows, :], on_ref[...])
        o_ref[0, rows, :] = (y * _silu(g_ref[0, rows, :].astype(F32))).astype(BF16)
        return carry

    lax.fori_loop(0, nblk, finish, 0)


def _gla(z, zs, fw, fb, on, n_lat, layer):
    b, nt, _ = z.shape
    cb0 = ZC_W // HEAD_W
    seq = lambda off: pl.BlockSpec((1, nt, HEAD_W), lambda bi, h: (bi, 0, cb0 + off * HEADS + h))
    kern = functools.partial(_gla_kernel, n_lat=n_lat)
    return pl.pallas_call(
        kern,
        grid=(b, HEADS),
        in_specs=[seq(0), seq(1), seq(2), seq(3),
                  pl.BlockSpec((1, nt, HEAD_W), lambda bi, h: (bi, 0, 0)),
                  pl.BlockSpec((None, 2, 1, HEAD_W, HEAD_W), lambda bi, h: (layer, 0, h, 0, 0)),
                  pl.BlockSpec((2, 1, 1, HEAD_W), lambda bi, h: (0, h, 0, 0)),
                  pl.BlockSpec((1, HEAD_W), lambda bi, h: (0, 0))],
        out_specs=pl.BlockSpec((1, nt, HEAD_W), lambda bi, h: (bi, 0, h)),
        out_shape=jax.ShapeDtypeStruct((b, nt, MIX_W), BF16),
        scratch_shapes=[pltpu.VMEM((2, nt, HEAD_W), F32),
                        pltpu.VMEM((nt, HEAD_W), BF16),
                        pltpu.VMEM((nt // GLA_CHUNK, HEAD_W, HEAD_W), F32),
                        pltpu.VMEM((nt // GLA_CHUNK, SUBLANES, HEAD_W), F32),
                        pltpu.VMEM((nt, HEAD_W), F32),
                        pltpu.VMEM((2, nt, HEAD_W), F32),
                        pltpu.VMEM((6, SEQ_BLOCK, SEQ_BLOCK), F32),
                        pltpu.VMEM((2, SEQ_BLOCK, SEQ_BLOCK), BF16),
                        pltpu.VMEM((GLA_DIAG * HEAD_W, SEQ_BLOCK), BF16)],
        compiler_params=_params(("arbitrary", "arbitrary")),
        name="gla_mixer",
    )(z, z, z, z, zs, fw, fb, on)


def _conv_kernel(zc_ref, zp_ref, zn_ref, dw_ref, dwb_ref, lng_ref, lnb_ref, scw_ref, o_ref,
                 u_sc, m_sc, *, lat_tiles, n_tiles):
    t = pl.program_id(1)
    tl = zc_ref.shape[1]
    w = MIX_W
    first = (t == 0) | (t == lat_tiles)
    last = (t == lat_tiles - 1) | (t == n_tiles - 1)

    def glu(ref, rows):
        return ref[0, rows, 0:w].astype(F32) * _sigmoid(ref[0, rows, w:2 * w].astype(F32))

    def gated(ref, rows):
        return ref[0, rows, 3 * w:4 * w].astype(F32) * ref[0, rows, 4 * w:5 * w].astype(F32)

    halo = slice(0, HALO)
    u_sc[0, 0:HALO, :] = jnp.where(first, 0.0, glu(zp_ref, halo))
    m_sc[0:HALO, :] = jnp.where(first, 0.0, gated(zp_ref, halo))
    u_sc[0, HALO + tl:, :] = jnp.where(last, 0.0, glu(zn_ref, halo))
    m_sc[HALO + tl:, :] = jnp.where(last, 0.0, gated(zn_ref, halo))
    rc = 32
    for r in range(tl // rc):
        rows = slice(r * rc, (r + 1) * rc)
        u_sc[0, HALO + r * rc:HALO + (r + 1) * rc, :] = glu(zc_ref, rows)
        m_sc[HALO + r * rc:HALO + (r + 1) * rc, :] = gated(zc_ref, rows)
    span = tl + 2 * HALO - SUBLANES
    for s in range(1, SUBLANES):
        u_sc[s, 0:span, :] = u_sc[0, s:s + span, :]

    pad = CONF_KERNEL // 2
    for r in range(tl // rc):
        r0 = r * rc
        acc = jnp.zeros((rc, w), F32)
        for kk in range(CONF_KERNEL):
            off = HALO + r0 + kk - pad
            base = off - off % SUBLANES
            acc = acc + dw_ref[kk:kk + 1, :] * u_sc[off % SUBLANES, base:base + rc, :]
        acc = acc + dwb_ref[...]
        mu = jnp.mean(acc, axis=-1, keepdims=True)
        xc = acc - mu
        y = xc * lax.rsqrt(jnp.mean(xc * xc, axis=-1, keepdims=True) + EPS) * lng_ref[...] + lnb_ref[...]
        o_ref[0, r0:r0 + rc, 0:w] = _silu(y).astype(BF16)
        s = HALO + r0
        cv = (scw_ref[0:1, :] * m_sc[s - 1:s - 1 + rc, :] + scw_ref[1:2, :] * m_sc[s:s + rc, :]
              + scw_ref[2:3, :] * m_sc[s + 1:s + 1 + rc, :])
        o_ref[0, r0:r0 + rc, w:2 * w] = (zc_ref[0, r0:r0 + rc, 2 * w:3 * w].astype(F32) * cv).astype(BF16)


def _conv_mixers(z, dw, dwb, lng, lnb, scw, n_lat):
    b, nt, _ = z.shape
    tl = SEQ_BLOCK
    n_tiles = nt // tl
    hb = tl // HALO
    nhb = nt // HALO
    kern = functools.partial(_conv_kernel, lat_tiles=n_lat // tl, n_tiles=n_tiles)
    vec = lambda n: pl.BlockSpec((n, MIX_W), lambda bi, t: (0, 0))
    return pl.pallas_call(
        kern,
        grid=(b, n_tiles),
        in_specs=[pl.BlockSpec((1, tl, ZC_W), lambda bi, t: (bi, t, 0)),
                  pl.BlockSpec((1, HALO, ZC_W), lambda bi, t: (bi, jnp.maximum(t * hb - 1, 0), 0)),
                  pl.BlockSpec((1, HALO, ZC_W), lambda bi, t: (bi, jnp.minimum((t + 1) * hb, nhb - 1), 0)),
                  vec(CONF_KERNEL), vec(1), vec(1), vec(1), vec(SC_KERNEL)],
        out_specs=pl.BlockSpec((1, tl, 2 * MIX_W), lambda bi, t: (bi, t, 0)),
        out_shape=jax.ShapeDtypeStruct((b, nt, 2 * MIX_W), BF16),
        scratch_shapes=[pltpu.VMEM((SUBLANES, tl + 2 * HALO, MIX_W), F32),
                        pltpu.VMEM((tl + 2 * HALO, MIX_W), F32)],
        compiler_params=_params(("arbitrary", "arbitrary")),
        name="conv_mixers",
    )(z, z, z, dw, dwb, lng, lnb, scw)


def _attn_kernel(q_ref, k_ref, v_ref, o_ref, *, sub):
    for r0 in range(0, q_ref.shape[1], sub):
        rows = slice(r0, r0 + sub)
        s = _dot_nt(q_ref[0, rows, :], k_ref[0])
        p = jnp.exp2(s - jnp.max(s, axis=-1, keepdims=True))
        denom = jnp.sum(p, axis=-1, keepdims=True)
        o_ref[0, rows, :] = (_dot(p.astype(BF16), v_ref[0]) / denom).astype(BF16)


def _attention(q, k, v, n_lat, need_ctx):
    b, nt, _ = q.shape
    kern = functools.partial(_attn_kernel, sub=Q_SUB)
    out = pl.pallas_call(
        kern,
        grid=(b, HEADS, n_lat // Q_BLOCK),
        in_specs=[pl.BlockSpec((1, Q_BLOCK, 2 * HEAD_W), lambda bi, h, qi: (bi, qi, h)),
                  pl.BlockSpec((1, nt, 2 * HEAD_W), lambda bi, h, qi: (bi, 0, h)),
                  pl.BlockSpec((1, nt, HEAD_W), lambda bi, h, qi: (bi, 0, h))],
        out_specs=pl.BlockSpec((1, Q_BLOCK, HEAD_W), lambda bi, h, qi: (bi, qi, h)),
        out_shape=jax.ShapeDtypeStruct((b, n_lat, MIX_W), BF16),
        compiler_params=_params(("arbitrary", "arbitrary", "arbitrary")),
        name="mla_attention",
    )(q, k, v)
    if not need_ctx:
        return out
    n_ctx = nt - n_lat
    cblk = n_lat // n_ctx
    ctx_spec = lambda wd: pl.BlockSpec((1, n_ctx, wd), lambda bi, h: (bi, cblk, h))
    out_ctx = pl.pallas_call(
        kern,
        grid=(b, HEADS),
        in_specs=[ctx_spec(2 * HEAD_W), ctx_spec(2 * HEAD_W), ctx_spec(HEAD_W)],
        out_specs=pl.BlockSpec((1, n_ctx, HEAD_W), lambda bi, h: (bi, 0, h)),
        out_shape=jax.ShapeDtypeStruct((b, n_ctx, MIX_W), BF16),
        compiler_params=_params(("arbitrary", "arbitrary")),
        name="mla_attention_ctx",
    )(q, k, v)
    return jnp.concatenate([out, out_ctx], axis=1)


def _wout_kernel(gla_ref, cs_ref, mla_ref, x_ref, ml_ref, mc_ref, g_ref, w_ref, xo_ref, h_ref,
                 *, n_lat, bounds):
    t = pl.program_id(1)
    tm = x_ref.shape[1]
    for r0, r1 in zip(bounds[:-1], bounds[1:]):
        rows = slice(r0, r1)
        rc = r1 - r0
        mix = jnp.concatenate([gla_ref[0, rows, :], cs_ref[0, rows, :], mla_ref[0, rows, :]], axis=1)
        o = _dot(mix, w_ref[...])
        rowid = t * tm + r0 + lax.broadcasted_iota(jnp.int32, (rc, 1), 0)
        is_ctx = rowid >= n_lat
        g1 = jnp.where(is_ctx, mc_ref[0, 2:3, :], ml_ref[0, 2:3, :])
        sh = jnp.where(is_ctx, mc_ref[0, 3:4, :], ml_ref[0, 3:4, :])
        sc = jnp.where(is_ctx, mc_ref[0, 4:5, :], ml_ref[0, 4:5, :])
        xn = x_ref[0, rows, :] + g1 * o
        xo_ref[0, rows, :] = xn
        h_ref[0, rows, :] = (_rms(xn, g_ref[...]) * (1.0 + sc) + sh).astype(BF16)


def _wout(gla, cs, mla, xx, mod, g, w, n_lat, n_rows, layer):
    b, _, d = xx.shape
    tm = n_rows // 8
    bounds = tuple(-(-(tm * c // 4) // 16) * 16 for c in range(4)) + (tm,)
    kern = functools.partial(_wout_kernel, n_lat=n_lat, bounds=bounds)
    tile = lambda wd: pl.BlockSpec((1, tm, wd), lambda bi, t: (bi, t, 0))
    ml, mc = _mod_specs(layer, b, d)
    return pl.pallas_call(
        kern,
        grid=(b, n_rows // tm),
        in_specs=[tile(MIX_W), tile(2 * MIX_W), tile(MIX_W), tile(d), ml, mc,
                  pl.BlockSpec((1, d), lambda bi, t: (0, 0)),
                  pl.BlockSpec((None,) + w.shape[1:], lambda bi, t: (layer, 0, 0))],
        out_specs=[tile(d), tile(d)],
        out_shape=[jax.ShapeDtypeStruct((b, n_rows, d), F32),
                   jax.ShapeDtypeStruct((b, n_rows, d), BF16)],
        compiler_params=_params(("arbitrary", "arbitrary")),
        name="w_out_norm2",
    )(gla, cs, mla, xx, mod, mod, g, w)


def _ffn_up_kernel(h_ref, w1_ref, w3_ref, u_ref, *, rc):
    tm = h_ref.shape[1]
    w1 = w1_ref[...].astype(BF16)
    w3 = w3_ref[...].astype(BF16)
    for r in range(tm // rc):
        rows = slice(r * rc, (r + 1) * rc)
        hh = h_ref[0, rows, :]
        u_ref[0, rows, :] = (_silu(_dot(hh, w1)) * _dot(hh, w3)).astype(BF16)


def _ffn_up(h2, w1, w3, layer):
    b, n_rows, d = h2.shape
    tm = n_rows // 2
    tf = 512
    dff = w1.shape[2]
    kern = functools.partial(_ffn_up_kernel, rc=tm // 8)
    wspec = pl.BlockSpec((None, d, tf), lambda bi, t, j: (layer, 0, j))
    return pl.pallas_call(
        kern,
        grid=(b, n_rows // tm, dff // tf),
        in_specs=[pl.BlockSpec((1, tm, d), lambda bi, t, j: (bi, t, 0)), wspec, wspec],
        out_specs=pl.BlockSpec((1, tm, tf), lambda bi, t, j: (bi, t, j)),
        out_shape=jax.ShapeDtypeStruct((b, n_rows, dff), BF16),
        compiler_params=_params(("arbitrary", "arbitrary", "arbitrary")),
        name="ffn_up",
    )(h2, w1, w3)


def _ffn_down_kernel(u_ref, x_ref, ml_ref, mc_ref, w2_ref, fg_ref, o_ref, *, n_lat, rc, final):
    t = pl.program_id(1)
    j = pl.program_id(2)
    tm = u_ref.shape[1]
    tn = w2_ref.shape[1]
    cols = pl.ds(pl.multiple_of(j * tn, tn), tn)
    xo_cols = cols if final else slice(None)
    for r in range(tm // rc):
        rows = slice(r * rc, (r + 1) * rc)
        rowid = t * tm + r * rc + lax.broadcasted_iota(jnp.int32, (rc, 1), 0)
        g2 = jnp.where(rowid >= n_lat, mc_ref[0, 5:6, cols], ml_ref[0, 5:6, cols])
        o_ref[0, rows, xo_cols] = x_ref[0, rows, xo_cols] + g2 * _dot(u_ref[0, rows, :], w2_ref[...])

    if final:
        @pl.when(j == pl.num_programs(2) - 1)
        def _():
            for r in range(tm // rc):
                rows = slice(r * rc, (r + 1) * rc)
                o_ref[0, rows, :] = _rms(o_ref[0, rows, :], fg_ref[...])


def _ffn_down(u, xn, mod, w2, fg, n_lat, final, layer):
    b, n_rows, d = xn.shape
    tn = 512
    dff = w2.shape[1]
    if final:
        tm = n_rows // 8
        xo_spec = pl.BlockSpec((1, tm, d), lambda bi, t, j: (bi, t, 0))
    else:
        tm = n_rows // 4
        xo_spec = pl.BlockSpec((1, tm, tn), lambda bi, t, j: (bi, t, j))
    kern = functools.partial(_ffn_down_kernel, n_lat=n_lat, rc=tm // (2 if final else 4), final=final)
    ml, mc = _mod_specs(layer, b, d)
    return pl.pallas_call(
        kern,
        grid=(b, n_rows // tm, d // tn),
        in_specs=[pl.BlockSpec((1, tm, dff), lambda bi, t, j: (bi, t, 0)),
                  xo_spec, ml, mc,
                  pl.BlockSpec((None, dff, tn), lambda bi, t, j: (layer, 0, j)),
                  pl.BlockSpec((1, d), lambda bi, t, j: (0, 0))],
        out_specs=xo_spec,
        out_shape=jax.ShapeDtypeStruct((b, n_rows, d), F32),
        compiler_params=_params(("arbitrary", "arbitrary", "arbitrary")),
        name="ffn_down",
    )(u, xn, mod, mod, w2, fg)


def _rope_tables(n_lat, n_ctx):
    rows = n_lat // GRID_W
    f32 = np.float32
    rowp = np.repeat(np.arange(rows, dtype=f32), GRID_W)
    colp = np.tile(np.arange(GRID_W, dtype=f32), rows)
    inv = (f32(ROPE_BASE) ** (-np.arange(ROPE_FREQS, dtype=f32) * f32(2.0) / f32(2 * ROPE_FREQS))).astype(f32)
    ar, ac = (rowp[:, None] * inv).astype(f32), (colp[:, None] * inv).astype(f32)
    zeros = np.zeros((n_lat, HEAD_W - MLA_ROPE), f32)
    cos_l = np.concatenate([np.cos(ar), np.cos(ar), np.cos(ac), np.cos(ac), zeros], axis=1)
    sin_l = np.concatenate([-np.sin(ar), np.sin(ar), -np.sin(ac), np.sin(ac), zeros], axis=1)
    cos_c = np.concatenate([np.ones((n_ctx, MLA_ROPE), f32), np.zeros((n_ctx, HEAD_W - MLA_ROPE), f32)], axis=1)
    cos_t = np.concatenate([cos_l, cos_c], axis=0).astype(f32)
    sin_t = np.concatenate([sin_l, np.zeros_like(cos_c)], axis=0).astype(f32)
    return jnp.asarray(cos_t), jnp.asarray(sin_t)


def kernel(x, c, ctx, c_ctx, norm1_g, w_mod, b_mod, w_in, gla_fg_up, gla_fg_b, gla_onorm_g, conf_dw, conf_dw_b, conf_ln_g, conf_ln_b, sc_dw, mla_q_norm_g, mla_kv_norm_g, mla_w_uq, mla_w_ukv, w_out, norm2_g, ffn_w1, ffn_w3, ffn_w2, final_norm_g):
    bsz, n_lat, d = x.shape
    n_ctx = ctx.shape[1]
    depth = w_in.shape[0]
    w = MIX_W

    xx = lax.dynamic_update_slice(jnp.pad(x, ((0, 0), (0, n_ctx), (0, 0))), ctx, (0, n_lat, 0))
    c3 = jnp.concatenate([c, c_ctx[None, :]], axis=0)
    mod = _modulation(jnp.broadcast_to(c3[:, :, None], (bsz + 1, d, 128)), w_mod, b_mod)
    mod = mod.reshape(depth, MOD_ROWS, 6, d)
    cos_t, sin_t = _rope_tables(n_lat, n_ctx)

    o_gla, o_lr, o_conv, o_mla = 0, 4 * w, 4 * w + 2 * GLA_GATE_RANK, 4 * w + 2 * GLA_GATE_RANK + 5 * w
    w_in_t = jnp.swapaxes(w_in, 1, 2).astype(BF16)
    w_conv = w_in_t[:, o_conv:o_mla]
    w_gla = w_in_t[:, o_gla:o_lr]
    w_small = jnp.concatenate([w_in_t[:, o_mla:], w_in_t[:, o_lr:o_conv],
                               jnp.zeros((depth, ZS_W - (w_in.shape[2] - o_mla) - 2 * GLA_GATE_RANK, d), BF16)],
                              axis=1)
    lr0 = MLA_ROPE
    fw = jnp.stack([jnp.pad(gla_fg_up[:, dd], ((0, 0), (lr0 + dd * GLA_GATE_RANK,
                                                        HEAD_W - lr0 - (dd + 1) * GLA_GATE_RANK), (0, 0)))
                    for dd in range(2)], axis=1)
    fw = fw.reshape(depth, 2, HEAD_W, HEADS, HEAD_W).transpose(0, 1, 3, 2, 4)
    fb = gla_fg_b.reshape(depth, 2, HEADS, 1, HEAD_W)
    wq = mla_w_uq.reshape(depth, MLA_Q_RANK, HEADS, HEAD_W + MLA_ROPE)
    wq = jnp.pad(wq, ((0, 0), (0, 0), (0, 0), (0, HEAD_W - MLA_ROPE))).reshape(depth, MLA_Q_RANK, -1).astype(BF16)
    wkv = mla_w_ukv.astype(BF16)
    wo = w_out.astype(BF16)
    w2 = ffn_w2.astype(BF16)
    row = lambda a: a.reshape(1, -1)

    for i in range(depth):
        last = i == depth - 1
        z, zl, qq, kk, vv = _win(xx, mod, row(norm1_g[i]), w_conv, w_gla, w_small, row(mla_q_norm_g[i]),
                                 row(mla_kv_norm_g[i]), wq, wkv, cos_t, sin_t, n_lat, i)
        gla = _gla(z, zl, fw, fb[i], row(gla_onorm_g[i]), n_lat, i)
        cs = _conv_mixers(z, conf_dw[i], row(conf_dw_b[i]), row(conf_ln_g[i]), row(conf_ln_b[i]), sc_dw[i], n_lat)
        mla = _attention(qq, kk, vv, n_lat, not last)
        n_rows = n_lat if last else n_lat + n_ctx
        xn, h2 = _wout(gla, cs, mla, xx, mod, row(norm2_g[i]), wo, n_lat, n_rows, i)
        u = _ffn_up(h2, ffn_w1, ffn_w3, i)
        xx = _ffn_down(u, xn, mod, w2, row(final_norm_g), n_lat, last, i)
    return xx
```

```python
import functools
import math

import jax
import jax.numpy as jnp
import numpy as np
from jax import lax
from jax.experimental import pallas as pl
from jax.experimental.pallas import tpu as pltpu

F32 = jnp.float32
BF16 = jnp.bfloat16

EPS = 1e-6
GRID_W = 64
HEADS = 4
HEAD_W = 128
GLA_GATE_RANK = 16
GLA_GATE_NORM = 16.0
GLA_CHUNK = 64
GLA_DIAG = 16
GLA_SAFE_DECAY = 60.0
CONF_KERNEL = 31
SC_KERNEL = 3
MIX_W = 512
MLA_ROPE = 64
MLA_Q_RANK = 384
MLA_KV_RANK = 128
ROPE_FREQS = 16
ROPE_BASE = 10000.0
Q_BLOCK = 2048
Q_SUB = 256
SEQ_BLOCK = 256
HALO = 16
SUBLANES = 8
ZC_W = 5 * MIX_W
MOD_ROWS = 8
ZS_W = 640
VMEM_LIMIT_V7X = 56 * 1024 * 1024


def _dot(a, b):
    return jnp.dot(a, b, preferred_element_type=F32)


def _dot_nt(a, b):
    return lax.dot_general(a, b, (((1,), (1,)), ((), ())), preferred_element_type=F32)


def _dot_tn(a, b):
    return lax.dot_general(a, b, (((0,), (0,)), ((), ())), preferred_element_type=F32)


def _split3(x):
    h1 = x.astype(BF16)
    r1 = x - h1.astype(F32)
    h2 = r1.astype(BF16)
    h3 = (r1 - h2.astype(F32)).astype(BF16)
    return h1, h2, h3


def _sigmoid(x):
    return 1.0 / (1.0 + jnp.exp(-x))


def _silu(x):
    return x * _sigmoid(x)


def _rms(x, g):
    return x * lax.rsqrt(jnp.mean(x * x, axis=-1, keepdims=True) + EPS) * g


def _params(sem):
    return pltpu.CompilerParams(dimension_semantics=sem, vmem_limit_bytes=VMEM_LIMIT_V7X)


def _mod_kernel(c_ref, w_ref, b_ref, o_ref, act_sc):
    tn = w_ref.shape[2]
    rows = []
    nrow = c_ref.shape[0]

    @pl.when((pl.program_id(0) == 0) & (pl.program_id(1) == 0))
    def _():
        for m in range(nrow):
            act_sc[m] = _silu(c_ref[m])

    kc = 8 * SUBLANES
    ncb = tn // 128
    acc = [[jnp.zeros((SUBLANES, 128), F32) for _ in range(ncb)] for _ in range(nrow)]
    for k0 in range(0, w_ref.shape[1], kc):
        wk = w_ref[0, k0:k0 + kc, :]
        for m in range(nrow):
            am = act_sc[m, k0:k0 + kc, :]
            for cb in range(ncb):
                part = wk[:, cb * 128:(cb + 1) * 128] * am
                acc[m][cb] = acc[m][cb] + jnp.sum(part.reshape(kc // SUBLANES, SUBLANES, 128), axis=0)
    for m in range(nrow):
        out_m = jnp.concatenate([jnp.sum(a, axis=0, keepdims=True) for a in acc[m]], axis=1)
        rows.append(out_m + b_ref[0])
    rows.append(jnp.zeros((MOD_ROWS - nrow, tn), F32))
    o_ref[0] = jnp.concatenate(rows, axis=0)


def _modulation(cb, w_mod, b_mod):
    depth, d, n6 = w_mod.shape
    tn = 1024
    return pl.pallas_call(
        _mod_kernel,
        grid=(depth, n6 // tn),
        in_specs=[pl.BlockSpec(cb.shape, lambda i, j: (0, 0, 0)),
                  pl.BlockSpec((1, d, tn), lambda i, j: (i, 0, j)),
                  pl.BlockSpec((1, 1, tn), lambda i, j: (i, 0, j))],
        out_specs=pl.BlockSpec((1, MOD_ROWS, tn), lambda i, j: (i, 0, j)),
        out_shape=jax.ShapeDtypeStruct((depth, MOD_ROWS, n6), F32),
        scratch_shapes=[pltpu.VMEM(cb.shape, F32)],
        compiler_params=_params(("arbitrary", "arbitrary")),
        name="modulation",
    )(cb, w_mod, b_mod.reshape(depth, 1, n6))


def _mod_specs(layer, b, d):
    lat = lambda bi, *_: (layer, bi, 0, 0)
    ctx = lambda *_: (layer, b, 0, 0)
    return (pl.BlockSpec((None, 1, 6, d), lat), pl.BlockSpec((None, 1, 6, d), ctx))


def _win_kernel(x_ref, ml_ref, mc_ref, g_ref, wc_ref, wg_ref, ws_ref, qg_ref, kvg_ref, wq_ref, wkv_ref,
                cos_ref, sin_ref, z_ref, zl_ref, q_ref, k_ref, v_ref, *, n_lat, rc):
    t = pl.program_id(1)
    tm = x_ref.shape[1]
    scale = (HEAD_W + MLA_ROPE) ** -0.5 * math.log2(math.e)
    lane = lax.broadcasted_iota(jnp.int32, (1, HEAD_W), 1)
    lower = (lane & 31) < 16
    for r in range(tm // rc):
        rows = slice(r * rc, (r + 1) * rc)
        y = _rms(x_ref[0, rows, :], g_ref[...])
        rowid = t * tm + r * rc + lax.broadcasted_iota(jnp.int32, (rc, 1), 0)
        is_ctx = rowid >= n_lat
        sh = jnp.where(is_ctx, mc_ref[0, 0:1, :], ml_ref[0, 0:1, :])
        sc = jnp.where(is_ctx, mc_ref[0, 1:2, :], ml_ref[0, 1:2, :])
        h = (y * (1.0 + sc) + sh).astype(BF16)
        zs = _dot_nt(h, ws_ref[...])
        z0 = 0
        for w_ref in (wc_ref, wg_ref):
            nw = w_ref.shape[0]
            for n0 in range(0, nw, nw // 2):
                z_ref[0, rows, z0 + n0:z0 + n0 + nw // 2] = _dot_nt(h, w_ref[n0:n0 + nw // 2, :]).astype(BF16)
            z0 += nw

        krb = zs[:, MLA_Q_RANK + MLA_KV_RANK:ZS_W]
        zl_ref[0, rows, :] = krb
        q = _dot(_rms(zs[:, 0:MLA_Q_RANK], qg_ref[...]).astype(BF16), wq_ref[...])
        kv = _dot(_rms(zs[:, MLA_Q_RANK:MLA_Q_RANK + MLA_KV_RANK], kvg_ref[...]).astype(BF16), wkv_ref[...])
        cs = cos_ref[rows, :]
        sn = sin_ref[rows, :]

        def rope(tt):
            swapped = jnp.where(lower, pltpu.roll(tt, HEAD_W - 16, 1), pltpu.roll(tt, 16, 1))
            return tt * cs + swapped * sn

        krr = rope(krb).astype(BF16)
        for hd in range(HEADS):
            c0 = 2 * HEAD_W * hd
            q_ref[0, rows, c0:c0 + HEAD_W] = (q[:, c0:c0 + HEAD_W] * scale).astype(BF16)
            q_ref[0, rows, c0 + HEAD_W:c0 + 2 * HEAD_W] = (
                rope(q[:, c0 + HEAD_W:c0 + 2 * HEAD_W]) * scale).astype(BF16)
            k_ref[0, rows, c0:c0 + HEAD_W] = kv[:, c0:c0 + HEAD_W].astype(BF16)
            k_ref[0, rows, c0 + HEAD_W:c0 + 2 * HEAD_W] = krr
            v_ref[0, rows, HEAD_W * hd:HEAD_W * (hd + 1)] = kv[:, c0 + HEAD_W:c0 + 2 * HEAD_W].astype(BF16)


def _win(xx, mod, g, w_conv, w_gla, w_small, qg, kvg, wq, wkv, cos_t, sin_t, n_lat, layer):
    b, nt, d = xx.shape
    tm = nt // 8
    nw = w_conv.shape[1] + w_gla.shape[1]
    kw = 2 * HEAD_W * HEADS
    kern = functools.partial(_win_kernel, n_lat=n_lat, rc=tm // 2)
    ml, mc = _mod_specs(layer, b, d)
    resident = lambda a: pl.BlockSpec((None,) + a.shape[1:], lambda bi, t: (layer, 0, 0),
                                      pipeline_mode=pl.Buffered(1))
    vec = lambda a: pl.BlockSpec(a.shape, lambda bi, t: (0, 0))
    tile = lambda wd: pl.BlockSpec((1, tm, wd), lambda bi, t: (bi, t, 0))
    return pl.pallas_call(
        kern,
        grid=(b, nt // tm),
        in_specs=[tile(d), ml, mc, vec(g), resident(w_conv), resident(w_gla), resident(w_small),
                  vec(qg), vec(kvg), resident(wq), resident(wkv),
                  pl.BlockSpec((tm, HEAD_W), lambda bi, t: (t, 0)),
                  pl.BlockSpec((tm, HEAD_W), lambda bi, t: (t, 0))],
        out_specs=[tile(nw), tile(HEAD_W), tile(kw), tile(kw), tile(MIX_W)],
        out_shape=[jax.ShapeDtypeStruct((b, nt, nw), BF16),
                   jax.ShapeDtypeStruct((b, nt, HEAD_W), F32),
                   jax.ShapeDtypeStruct((b, nt, kw), BF16),
                   jax.ShapeDtypeStruct((b, nt, kw), BF16),
                   jax.ShapeDtypeStruct((b, nt, MIX_W), BF16)],
        compiler_params=_params(("arbitrary", "arbitrary")),
        name="norm1_w_in",
    )(xx, mod, mod, g, w_conv, w_gla, w_small, qg, kvg, wq, wkv, cos_t, sin_t)


def _gla_kernel(q_ref, k_ref, v_ref, g_ref, zs_ref, fw_ref, fb_ref, on_ref, o_ref,
                bc_sc, qin_sc, kv_sc, er_sc, oacc_sc, oint_sc, msk_sc, tri_sc, sel_sc, *, n_lat):
    n_tot = q_ref.shape[1]
    blk = SEQ_BLOCK
    cpb = blk // GLA_CHUNK
    nblk = n_tot // blk
    nchunk = n_tot // GLA_CHUNK
    lat_chunks = n_lat // GLA_CHUNK
    ctx_chunks = nchunk - lat_chunks
    ndb = blk // GLA_DIAG
    qscale = HEAD_W ** -0.5
    unroll = max(u for u in range(1, 35) if nchunk % u == 0)
    group_pro, group_intra = 8, 4

    ri = lax.broadcasted_iota(jnp.int32, (blk, blk), 0)
    ci = lax.broadcasted_iota(jnp.int32, (blk, blk), 1)
    same64 = (ri >> 6) == (ci >> 6)
    same16 = (ri >> 4) == (ci >> 4)
    one = lambda m: jnp.where(m, 1.0, 0.0).astype(F32)
    msk_sc[0] = one(same64)
    msk_sc[1] = one((ri >> 5) == (ci >> 5))
    msk_sc[2] = one(same16 & ((ri & 15) >= (ci & 15)))
    msk_sc[3] = one(same16 & ((ri & 15) <= (ci & 15)))
    msk_sc[4] = one(same64 & (ri >= ci))
    msk_sc[5] = one(same64 & (ri <= ci))
    tri_sc[0] = msk_sc[4].astype(BF16)
    tri_sc[1] = msk_sc[5].astype(BF16)
    cj = lax.broadcasted_iota(jnp.int32, (HEAD_W, blk), 1) & (GLA_DIAG - 1)
    for jj in range(GLA_DIAG):
        sel_sc[jj * HEAD_W:(jj + 1) * HEAD_W, :] = one(cj == jj).astype(BF16)

    row = lax.broadcasted_iota(jnp.int32, (blk, 1), 0)
    half0 = (row & 63) < 32
    quart0 = (row & 31) < 16

    def block_rows(i, nb=1):
        start = i * blk if isinstance(i, int) else pl.multiple_of(i * blk, blk)
        return pl.ds(start, nb * blk)

    def paired(fn, group):
        def body(p, carry):
            fn(group * p, group)
            return carry
        lax.fori_loop(0, nblk // group, body, 0)
        if nblk % group:
            fn(nblk - nblk % group, nblk % group)

    fsplit = [_split3(fw_ref[d, 0]) for d in range(2)]
    fcat = jnp.concatenate([jnp.concatenate([fh, fh, fl], axis=0) for fh, fl, _ in fsplit], axis=1)

    def prologue(i0, nb):
        rows_l = [block_rows(i0 + u) for u in range(nb)]
        zcats = []
        for rows in rows_l:
            zh, zl, _ = _split3(zs_ref[0, rows, :])
            zcats.append(jnp.concatenate([zh, zl, zh], axis=1))
        xg2s = [_dot(zc, fcat) for zc in zcats]
        logds = []
        for xg2 in xg2s:
            for d in range(2):
                xg = xg2[:, d * HEAD_W:(d + 1) * HEAD_W] + fb_ref[d, 0]
                logd = (jnp.minimum(xg, 0.0) - jnp.log1p(jnp.exp(-jnp.abs(xg)))) * (1.0 / GLA_GATE_NORM)
                logds.append(jnp.concatenate(_split3(logd)[:2], axis=1))
        c2s = [_dot(tri_sc[n % 2], ld) for n, ld in enumerate(logds)]
        for n, c2 in enumerate(c2s):
            bc_sc[n % 2, rows_l[n // 2], :] = c2[:, 0:HEAD_W] + c2[:, HEAD_W:]

    paired(prologue, group_pro)

    def run_direction(d):
        rev = d == 1

        def load_block(i):
            rows = block_rows(i)
            b = bc_sc[d, rows, :]
            q = q_ref[0, rows, :].astype(F32) * qscale
            k = k_ref[0, rows, :].astype(F32)
            v = v_ref[0, rows, :]
            b4 = b.reshape(cpb, GLA_CHUNK, HEAD_W)
            blast = b4[:, 0:1, :] if rev else b4[:, GLA_CHUNK - 1:GLA_CHUNK, :]
            ebl = jnp.exp(blast)
            kst = (k.reshape(cpb, GLA_CHUNK, HEAD_W) * jnp.exp(blast - b4)).astype(BF16)
            v4 = v.reshape(cpb, GLA_CHUNK, HEAD_W)
            for c in range(cpb):
                kv_sc[i * cpb + c] = _dot_tn(v4[c], kst[c])
                er_sc[i * cpb + c] = jnp.broadcast_to(ebl[c], (SUBLANES, HEAD_W))
            qin_sc[rows, :] = (q * jnp.exp(b)).astype(BF16)
            return rows, b, q, k, v

        def store_block(rows, scores, v):
            o_blk = _dot(scores.astype(BF16), v)
            if rev:
                oacc_sc[rows, :] += o_blk
            else:
                oacc_sc[rows, :] = o_blk

        def intra_plain(i0, nb):
            prep = []
            for u in range(nb):
                i = i0 + u
                rows = block_rows(i)
                b = bc_sc[d, rows, :]
                q = q_ref[0, rows, :].astype(F32) * qscale
                k = k_ref[0, rows, :].astype(F32)
                b4 = b.reshape(cpb, GLA_CHUNK, HEAD_W)
                ebl = jnp.exp(b4[:, 0:1, :] if rev else b4[:, GLA_CHUNK - 1:GLA_CHUNK, :])
                kt = k * jnp.exp(-b)
                kst = (kt.reshape(cpb, GLA_CHUNK, HEAD_W) * ebl).astype(BF16)
                qin = (q * jnp.exp(b)).astype(BF16)
                qin_sc[rows, :] = qin
                prep.append((i, rows, qin, kt.astype(BF16), kst, ebl))
            scores = [_dot_nt(qin, ktb) for _, _, qin, ktb, _, _ in prep]
            for i, rows, _, _, kst, ebl in prep:
                v4 = v_ref[0, rows, :].reshape(cpb, GLA_CHUNK, HEAD_W)
                for c in range(cpb):
                    kv_sc[i * cpb + c] = _dot_tn(v4[c], kst[c])
                    er_sc[i * cpb + c] = jnp.broadcast_to(ebl[c], (SUBLANES, HEAD_W))
            probs = [(s * msk_sc[5 if rev else 4]).astype(BF16) for s in scores]
            for (_, rows, _, _, _, _), p in zip(prep, probs):
                store_block(rows, p, v_ref[0, rows, :])

        def intra_robust(i):
            rows, b, q, k, v = load_block(i)
            b4 = b.reshape(cpb, GLA_CHUNK, HEAD_W)
            r1 = b4[:, 32:33, :] if rev else b4[:, 31:32, :]
            r1 = jnp.broadcast_to(r1, b4.shape).reshape(blk, HEAD_W)
            qsel, ksel = (half0, ~half0) if rev else (~half0, half0)
            qa = q * jnp.where(qsel, jnp.exp(jnp.minimum(b - r1, 0.0)), 0.0)
            ka = k * jnp.where(ksel, jnp.exp(jnp.minimum(r1 - b, 0.0)), 0.0)
            s1 = _dot_nt(qa.astype(BF16), ka.astype(BF16))
            b8 = b.reshape(blk // 32, 32, HEAD_W)
            r2 = b8[:, 16:17, :] if rev else b8[:, 15:16, :]
            r2 = jnp.broadcast_to(r2, b8.shape).reshape(blk, HEAD_W)
            qsel, ksel = (quart0, ~quart0) if rev else (~quart0, quart0)
            qb = q * jnp.where(qsel, jnp.exp(jnp.minimum(b - r2, 0.0)), 0.0)
            kb = k * jnp.where(ksel, jnp.exp(jnp.minimum(r2 - b, 0.0)), 0.0)
            s2 = _dot_nt(qb.astype(BF16), kb.astype(BF16))
            q16 = q.reshape(ndb, GLA_DIAG, HEAD_W)
            k16 = k.reshape(ndb, GLA_DIAG, HEAD_W)
            b16 = b.reshape(ndb, GLA_DIAG, HEAD_W)
            half = GLA_DIAG // 2
            terms = []
            for jj in range(GLA_DIAG):
                bj = b16[:, jj:jj + 1, :]
                kj = k16[:, jj:jj + 1, :]
                need = ((True, jj >= half) if rev else (jj < half, True))
                parts = []
                for hsel, needed in zip((slice(0, half), slice(half, GLA_DIAG)), need):
                    if needed:
                        e = jnp.exp(jnp.minimum(b16[:, hsel, :] - bj, 0.0))
                        parts.append(q16[:, hsel, :] * kj * e)
                    else:
                        parts.append(jnp.zeros((ndb, half, HEAD_W), F32))
                terms.append(jnp.concatenate(parts, axis=1).reshape(blk, HEAD_W).astype(BF16))
            sdiag = _dot(jnp.concatenate(terms, axis=1), sel_sc[...])
            store_block(rows, s1 * msk_sc[0] + s2 * msk_sc[1] + sdiag * msk_sc[3 if rev else 2], v)

        def intra(i0, nb):
            plain_ok = jnp.min(bc_sc[d, block_rows(i0, nb), :]) >= -GLA_SAFE_DECAY

            @pl.when(plain_ok)
            def _():
                intra_plain(i0, nb)

            @pl.when(jnp.logical_not(plain_ok))
            def _():
                for u in range(nb):
                    intra_robust(i0 + u)

        paired(intra, group_intra)

        def inter(m, st):
            for u in range(unroll):
                n = m * unroll + u
                if rev:
                    cid = jnp.where(n < ctx_chunks, lat_chunks + (ctx_chunks - 1 - n),
                                    lat_chunks - 1 - (n - ctx_chunks))
                else:
                    cid = jnp.where(n < ctx_chunks, lat_chunks + n, n - ctx_chunks)
                rows = pl.ds(pl.multiple_of(cid * GLA_CHUNK, GLA_CHUNK), GLA_CHUNK)
                oint_sc[d, rows, :] = _dot_nt(qin_sc[rows, :], st.astype(BF16))
                st = st * er_sc[cid][0:1, :] + kv_sc[cid]
            return st

        lax.fori_loop(0, nchunk // unroll, inter, jnp.zeros((HEAD_W, HEAD_W), F32))

    run_direction(0)
    run_direction(1)

    def finish(i, carry):
        rows = block_rows(i)
        y = _rms(oacc_sc[rows, :] + oint_sc[0, rows, :] + oint_sc[1, rows, :], on_ref[...])
        o_ref[0, rows, :] = (y * _silu(g_ref[0, rows, :].astype(F32))).astype(BF16)
        return carry

    lax.fori_loop(0, nblk, finish, 0)


def _gla(z, zs, fw, fb, on, n_lat, layer):
    b, nt, _ = z.shape
    cb0 = ZC_W // HEAD_W
    seq = lambda off: pl.BlockSpec((1, nt, HEAD_W), lambda bi, h: (bi, 0, cb0 + off * HEADS + h))
    kern = functools.partial(_gla_kernel, n_lat=n_lat)
    return pl.pallas_call(
        kern,
        grid=(b, HEADS),
        in_specs=[seq(0), seq(1), seq(2), seq(3),
                  pl.BlockSpec((1, nt, HEAD_W), lambda bi, h: (bi, 0, 0)),
                  pl.BlockSpec((None, 2, 1, HEAD_W, HEAD_W), lambda bi, h: (layer, 0, h, 0, 0)),
                  pl.BlockSpec((2, 1, 1, HEAD_W), lambda bi, h: (0, h, 0, 0)),
                  pl.BlockSpec((1, HEAD_W), lambda bi, h: (0, 0))],
        out_specs=pl.BlockSpec((1, nt, HEAD_W), lambda bi, h: (bi, 0, h)),
        out_shape=jax.ShapeDtypeStruct((b, nt, MIX_W), BF16),
        scratch_shapes=[pltpu.VMEM((2, nt, HEAD_W), F32),
                        pltpu.VMEM((nt, HEAD_W), BF16),
                        pltpu.VMEM((nt // GLA_CHUNK, HEAD_W, HEAD_W), F32),
                        pltpu.VMEM((nt // GLA_CHUNK, SUBLANES, HEAD_W), F32),
                        pltpu.VMEM((nt, HEAD_W), F32),
                        pltpu.VMEM((2, nt, HEAD_W), F32),
                        pltpu.VMEM((6, SEQ_BLOCK, SEQ_BLOCK), F32),
                        pltpu.VMEM((2, SEQ_BLOCK, SEQ_BLOCK), BF16),
                        pltpu.VMEM((GLA_DIAG * HEAD_W, SEQ_BLOCK), BF16)],
        compiler_params=_params(("arbitrary", "arbitrary")),
        name="gla_mixer",
    )(z, z, z, z, zs, fw, fb, on)


def _conv_kernel(zc_ref, zp_ref, zn_ref, dw_ref, dwb_ref, lng_ref, lnb_ref, scw_ref, o_ref,
                 u_sc, m_sc, *, lat_tiles, n_tiles):
    t = pl.program_id(1)
    tl = zc_ref.shape[1]
    w = MIX_W
    first = (t == 0) | (t == lat_tiles)
    last = (t == lat_tiles - 1) | (t == n_tiles - 1)

    def glu(ref, rows):
        return ref[0, rows, 0:w].astype(F32) * _sigmoid(ref[0, rows, w:2 * w].astype(F32))

    def gated(ref, rows):
        return ref[0, rows, 3 * w:4 * w].astype(F32) * ref[0, rows, 4 * w:5 * w].astype(F32)

    halo = slice(0, HALO)
    u_sc[0, 0:HALO, :] = jnp.where(first, 0.0, glu(zp_ref, halo))
    m_sc[0:HALO, :] = jnp.where(first, 0.0, gated(zp_ref, halo))
    u_sc[0, HALO + tl:, :] = jnp.where(last, 0.0, glu(zn_ref, halo))
    m_sc[HALO + tl:, :] = jnp.where(last, 0.0, gated(zn_ref, halo))
    rc = 32
    for r in range(tl // rc):
        rows = slice(r * rc, (r + 1) * rc)
        u_sc[0, HALO + r * rc:HALO + (r + 1) * rc, :] = glu(zc_ref, rows)
        m_sc[HALO + r * rc:HALO + (r + 1) * rc, :] = gated(zc_ref, rows)
    span = tl + 2 * HALO - SUBLANES
    for s in range(1, SUBLANES):
        u_sc[s, 0:span, :] = u_sc[0, s:s + span, :]

    pad = CONF_KERNEL // 2
    for r in range(tl // rc):
        r0 = r * rc
        acc = jnp.zeros((rc, w), F32)
        for kk in range(CONF_KERNEL):
            off = HALO + r0 + kk - pad
            base = off - off % SUBLANES
            acc = acc + dw_ref[kk:kk + 1, :] * u_sc[off % SUBLANES, base:base + rc, :]
        acc = acc + dwb_ref[...]
        mu = jnp.mean(acc, axis=-1, keepdims=True)
        xc = acc - mu
        y = xc * lax.rsqrt(jnp.mean(xc * xc, axis=-1, keepdims=True) + EPS) * lng_ref[...] + lnb_ref[...]
        o_ref[0, r0:r0 + rc, 0:w] = _silu(y).astype(BF16)
        s = HALO + r0
        cv = (scw_ref[0:1, :] * m_sc[s - 1:s - 1 + rc, :] + scw_ref[1:2, :] * m_sc[s:s + rc, :]
              + scw_ref[2:3, :] * m_sc[s + 1:s + 1 + rc, :])
        o_ref[0, r0:r0 + rc, w:2 * w] = (zc_ref[0, r0:r0 + rc, 2 * w:3 * w].astype(F32) * cv).astype(BF16)


def _conv_mixers(z, dw, dwb, lng, lnb, scw, n_lat):
    b, nt, _ = z.shape
    tl = SEQ_BLOCK
    n_tiles = nt // tl
    hb = tl // HALO
    nhb = nt // HALO
    kern = functools.partial(_conv_kernel, lat_tiles=n_lat // tl, n_tiles=n_tiles)
    vec = lambda n: pl.BlockSpec((n, MIX_W), lambda bi, t: (0, 0))
    return pl.pallas_call(
        kern,
        grid=(b, n_tiles),
        in_specs=[pl.BlockSpec((1, tl, ZC_W), lambda bi, t: (bi, t, 0)),
                  pl.BlockSpec((1, HALO, ZC_W), lambda bi, t: (bi, jnp.maximum(t * hb - 1, 0), 0)),
                  pl.BlockSpec((1, HALO, ZC_W), lambda bi, t: (bi, jnp.minimum((t + 1) * hb, nhb - 1), 0)),
                  vec(CONF_KERNEL), vec(1), vec(1), vec(1), vec(SC_KERNEL)],
        out_specs=pl.BlockSpec((1, tl, 2 * MIX_W), lambda bi, t: (bi, t, 0)),
        out_shape=jax.ShapeDtypeStruct((b, nt, 2 * MIX_W), BF16),
        scratch_shapes=[pltpu.VMEM((SUBLANES, tl + 2 * HALO, MIX_W), F32),
                        pltpu.VMEM((tl + 2 * HALO, MIX_W), F32)],
        compiler_params=_params(("arbitrary", "arbitrary")),
        name="conv_mixers",
    )(z, z, z, dw, dwb, lng, lnb, scw)


def _attn_kernel(q_ref, k_ref, v_ref, o_ref, *, sub):
    for r0 in range(0, q_ref.shape[1], sub):
        rows = slice(r0, r0 + sub)
        s = _dot_nt(q_ref[0, rows, :], k_ref[0])
        p = jnp.exp2(s - jnp.max(s, axis=-1, keepdims=True))
        denom = jnp.sum(p, axis=-1, keepdims=True)
        o_ref[0, rows, :] = (_dot(p.astype(BF16), v_ref[0]) / denom).astype(BF16)


def _attention(q, k, v, n_lat, need_ctx):
    b, nt, _ = q.shape
    kern = functools.partial(_attn_kernel, sub=Q_SUB)
    out = pl.pallas_call(
        kern,
        grid=(b, HEADS, n_lat // Q_BLOCK),
        in_specs=[pl.BlockSpec((1, Q_BLOCK, 2 * HEAD_W), lambda bi, h, qi: (bi, qi, h)),
                  pl.BlockSpec((1, nt, 2 * HEAD_W), lambda bi, h, qi: (bi, 0, h)),
                  pl.BlockSpec((1, nt, HEAD_W), lambda bi, h, qi: (bi, 0, h))],
        out_specs=pl.BlockSpec((1, Q_BLOCK, HEAD_W), lambda bi, h, qi: (bi, qi, h)),
        out_shape=jax.ShapeDtypeStruct((b, n_lat, MIX_W), BF16),
        compiler_params=_params(("arbitrary", "arbitrary", "arbitrary")),
        name="mla_attention",
    )(q, k, v)
    if not need_ctx:
        return out
    n_ctx = nt - n_lat
    cblk = n_lat // n_ctx
    ctx_spec = lambda wd: pl.BlockSpec((1, n_ctx, wd), lambda bi, h: (bi, cblk, h))
    out_ctx = pl.pallas_call(
        kern,
        grid=(b, HEADS),
        in_specs=[ctx_spec(2 * HEAD_W), ctx_spec(2 * HEAD_W), ctx_spec(HEAD_W)],
        out_specs=pl.BlockSpec((1, n_ctx, HEAD_W), lambda bi, h: (bi, 0, h)),
        out_shape=jax.ShapeDtypeStruct((b, n_ctx, MIX_W), BF16),
        compiler_params=_params(("arbitrary", "arbitrary")),
        name="mla_attention_ctx",
    )(q, k, v)
    return jnp.concatenate([out, out_ctx], axis=1)


def _wout_kernel(gla_ref, cs_ref, mla_ref, x_ref, ml_ref, mc_ref, g_ref, w_ref, xo_ref, h_ref,
                 *, n_lat, bounds):
    t = pl.program_id(1)
    tm = x_ref.shape[1]
    for r0, r1 in zip(bounds[:-1], bounds[1:]):
        rows = slice(r0, r1)
        rc = r1 - r0
        mix = jnp.concatenate([gla_ref[0, rows, :], cs_ref[0, rows, :], mla_ref[0, rows, :]], axis=1)
        o = _dot(mix, w_ref[...])
        rowid = t * tm + r0 + lax.broadcasted_iota(jnp.int32, (rc, 1), 0)
        is_ctx = rowid >= n_lat
        g1 = jnp.where(is_ctx, mc_ref[0, 2:3, :], ml_ref[0, 2:3, :])
        sh = jnp.where(is_ctx, mc_ref[0, 3:4, :], ml_ref[0, 3:4, :])
        sc = jnp.where(is_ctx, mc_ref[0, 4:5, :], ml_ref[0, 4:5, :])
        xn = x_ref[0, rows, :] + g1 * o
        xo_ref[0, rows, :] = xn
        h_ref[0, rows, :] = (_rms(xn, g_ref[...]) * (1.0 + sc) + sh).astype(BF16)


def _wout(gla, cs, mla, xx, mod, g, w, n_lat, n_rows, layer):
    b, _, d = xx.shape
    tm = n_rows // 8
    bounds = tuple(-(-(tm * c // 4) // 16) * 16 for c in range(4)) + (tm,)
    kern = functools.partial(_wout_kernel, n_lat=n_lat, bounds=bounds)
    tile = lambda wd: pl.BlockSpec((1, tm, wd), lambda bi, t: (bi, t, 0))
    ml, mc = _mod_specs(layer, b, d)
    return pl.pallas_call(
        kern,
        grid=(b, n_rows // tm),
        in_specs=[tile(MIX_W), tile(2 * MIX_W), tile(MIX_W), tile(d), ml, mc,
                  pl.BlockSpec((1, d), lambda bi, t: (0, 0)),
                  pl.BlockSpec((None,) + w.shape[1:], lambda bi, t: (layer, 0, 0))],
        out_specs=[tile(d), tile(d)],
        out_shape=[jax.ShapeDtypeStruct((b, n_rows, d), F32),
                   jax.ShapeDtypeStruct((b, n_rows, d), BF16)],
        compiler_params=_params(("arbitrary", "arbitrary")),
        name="w_out_norm2",
    )(gla, cs, mla, xx, mod, mod, g, w)


def _ffn_up_kernel(h_ref, w1_ref, w3_ref, u_ref, *, rc):
    tm = h_ref.shape[1]
    w1 = w1_ref[...].astype(BF16)
    w3 = w3_ref[...].astype(BF16)
    for r in range(tm // rc):
        rows = slice(r * rc, (r + 1) * rc)
        hh = h_ref[0, rows, :]
        u_ref[0, rows, :] = (_silu(_dot(hh, w1)) * _dot(hh, w3)).astype(BF16)


def _ffn_up(h2, w1, w3, layer):
    b, n_rows, d = h2.shape
    tm = n_rows // 2
    tf = 512
    dff = w1.shape[2]
    kern = functools.partial(_ffn_up_kernel, rc=tm // 8)
    wspec = pl.BlockSpec((None, d, tf), lambda bi, t, j: (layer, 0, j))
    return pl.pallas_call(
        kern,
        grid=(b, n_rows // tm, dff // tf),
        in_specs=[pl.BlockSpec((1, tm, d), lambda bi, t, j: (bi, t, 0)), wspec, wspec],
        out_specs=pl.BlockSpec((1, tm, tf), lambda bi, t, j: (bi, t, j)),
        out_shape=jax.ShapeDtypeStruct((b, n_rows, dff), BF16),
        compiler_params=_params(("arbitrary", "arbitrary", "arbitrary")),
        name="ffn_up",
    )(h2, w1, w3)


def _ffn_down_kernel(u_ref, x_ref, ml_ref, mc_ref, w2_ref, fg_ref, o_ref, *, n_lat, rc, final):
    t = pl.program_id(1)
    j = pl.program_id(2)
    tm = u_ref.shape[1]
    tn = w2_ref.shape[1]
    cols = pl.ds(pl.multiple_of(j * tn, tn), tn)
    xo_cols = cols if final else slice(None)
    for r in range(tm // rc):
        rows = slice(r * rc, (r + 1) * rc)
        rowid = t * tm + r * rc + lax.broadcasted_iota(jnp.int32, (rc, 1), 0)
        g2 = jnp.where(rowid >= n_lat, mc_ref[0, 5:6, cols], ml_ref[0, 5:6, cols])
        o_ref[0, rows, xo_cols] = x_ref[0, rows, xo_cols] + g2 * _dot(u_ref[0, rows, :], w2_ref[...])

    if final:
        @pl.when(j == pl.num_programs(2) - 1)
        def _():
            for r in range(tm // rc):
                rows = slice(r * rc, (r + 1) * rc)
                o_ref[0, rows, :] = _rms(o_ref[0, rows, :], fg_ref[...])


def _ffn_down(u, xn, mod, w2, fg, n_lat, final, layer):
    b, n_rows, d = xn.shape
    tn = 512
    dff = w2.shape[1]
    if final:
        tm = n_rows // 8
        xo_spec = pl.BlockSpec((1, tm, d), lambda bi, t, j: (bi, t, 0))
    else:
        tm = n_rows // 4
        xo_spec = pl.BlockSpec((1, tm, tn), lambda bi, t, j: (bi, t, j))
    kern = functools.partial(_ffn_down_kernel, n_lat=n_lat, rc=tm // (2 if final else 4), final=final)
    ml, mc = _mod_specs(layer, b, d)
    return pl.pallas_call(
        kern,
        grid=(b, n_rows // tm, d // tn),
        in_specs=[pl.BlockSpec((1, tm, dff), lambda bi, t, j: (bi, t, 0)),
                  xo_spec, ml, mc,
                  pl.BlockSpec((None, dff, tn), lambda bi, t, j: (layer, 0, j)),
                  pl.BlockSpec((1, d), lambda bi, t, j: (0, 0))],
        out_specs=xo_spec,
        out_shape=jax.ShapeDtypeStruct((b, n_rows, d), F32),
        compiler_params=_params(("arbitrary", "arbitrary", "arbitrary")),
        name="ffn_down",
    )(u, xn, mod, mod, w2, fg)


def _rope_tables(n_lat, n_ctx):
    rows = n_lat // GRID_W
    f32 = np.float32
    rowp = np.repeat(np.arange(rows, dtype=f32), GRID_W)
    colp = np.tile(np.arange(GRID_W, dtype=f32), rows)
    inv = (f32(ROPE_BASE) ** (-np.arange(ROPE_FREQS, dtype=f32) * f32(2.0) / f32(2 * ROPE_FREQS))).astype(f32)
    ar, ac = (rowp[:, None] * inv).astype(f32), (colp[:, None] * inv).astype(f32)
    zeros = np.zeros((n_lat, HEAD_W - MLA_ROPE), f32)
    cos_l = np.concatenate([np.cos(ar), np.cos(ar), np.cos(ac), np.cos(ac), zeros], axis=1)
    sin_l = np.concatenate([-np.sin(ar), np.sin(ar), -np.sin(ac), np.sin(ac), zeros], axis=1)
    cos_c = np.concatenate([np.ones((n_ctx, MLA_ROPE), f32), np.zeros((n_ctx, HEAD_W - MLA_ROPE), f32)], axis=1)
    cos_t = np.concatenate([cos_l, cos_c], axis=0).astype(f32)
    sin_t = np.concatenate([sin_l, np.zeros_like(cos_c)], axis=0).astype(f32)
    return jnp.asarray(cos_t), jnp.asarray(sin_t)


def kernel(x, c, ctx, c_ctx, norm1_g, w_mod, b_mod, w_in, gla_fg_up, gla_fg_b, gla_onorm_g, conf_dw, conf_dw_b, conf_ln_g, conf_ln_b, sc_dw, mla_q_norm_g, mla_kv_norm_g, mla_w_uq, mla_w_ukv, w_out, norm2_g, ffn_w1, ffn_w3, ffn_w2, final_norm_g):
    bsz, n_lat, d = x.shape
    n_ctx = ctx.shape[1]
    depth = w_in.shape[0]
    w = MIX_W

    xx = lax.dynamic_update_slice(jnp.pad(x, ((0, 0), (0, n_ctx), (0, 0))), ctx, (0, n_lat, 0))
    c3 = jnp.concatenate([c, c_ctx[None, :]], axis=0)
    mod = _modulation(jnp.broadcast_to(c3[:, :, None], (bsz + 1, d, 128)), w_mod, b_mod)
    mod = mod.reshape(depth, MOD_ROWS, 6, d)
    cos_t, sin_t = _rope_tables(n_lat, n_ctx)

    o_gla, o_lr, o_conv, o_mla = 0, 4 * w, 4 * w + 2 * GLA_GATE_RANK, 4 * w + 2 * GLA_GATE_RANK + 5 * w
    w_in_t = jnp.swapaxes(w_in, 1, 2).astype(BF16)
    w_conv = w_in_t[:, o_conv:o_mla]
    w_gla = w_in_t[:, o_gla:o_lr]
    w_small = jnp.concatenate([w_in_t[:, o_mla:], w_in_t[:, o_lr:o_conv],
                               jnp.zeros((depth, ZS_W - (w_in.shape[2] - o_mla) - 2 * GLA_GATE_RANK, d), BF16)],
                              axis=1)
    lr0 = MLA_ROPE
    fw = jnp.stack([jnp.pad(gla_fg_up[:, dd], ((0, 0), (lr0 + dd * GLA_GATE_RANK,
                                                        HEAD_W - lr0 - (dd + 1) * GLA_GATE_RANK), (0, 0)))
                    for dd in range(2)], axis=1)
    fw = fw.reshape(depth, 2, HEAD_W, HEADS, HEAD_W).transpose(0, 1, 3, 2, 4)
    fb = gla_fg_b.reshape(depth, 2, HEADS, 1, HEAD_W)
    wq = mla_w_uq.reshape(depth, MLA_Q_RANK, HEADS, HEAD_W + MLA_ROPE)
    wq = jnp.pad(wq, ((0, 0), (0, 0), (0, 0), (0, HEAD_W - MLA_ROPE))).reshape(depth, MLA_Q_RANK, -1).astype(BF16)
    wkv = mla_w_ukv.astype(BF16)
    wo = w_out.astype(BF16)
    w2 = ffn_w2.astype(BF16)
    row = lambda a: a.reshape(1, -1)

    for i in range(depth):
        last = i == depth - 1
        z, zl, qq, kk, vv = _win(xx, mod, row(norm1_g[i]), w_conv, w_gla, w_small, row(mla_q_norm_g[i]),
                                 row(mla_kv_norm_g[i]), wq, wkv, cos_t, sin_t, n_lat, i)
        gla = _gla(z, zl, fw, fb[i], row(gla_onorm_g[i]), n_lat, i)
        cs = _conv_mixers(z, conf_dw[i], row(conf_dw_b[i]), row(conf_ln_g[i]), row(conf_ln_b[i]), sc_dw[i], n_lat)
        mla = _attention(qq, kk, vv, n_lat, not last)
        n_rows = n_lat if last else n_lat + n_ctx
        xn, h2 = _wout(gla, cs, mla, xx, mod, row(norm2_g[i]), wo, n_lat, n_rows, i)
        u = _ffn_up(h2, ffn_w1, ffn_w3, i)
        xx = _ffn_down(u, xn, mod, w2, row(final_norm_g), n_lat, last, i)
    return xx
```

```python
import functools
import math

import jax
import jax.numpy as jnp
import numpy as np
from jax import lax
from jax.experimental import pallas as pl
from jax.experimental.pallas import tpu as pltpu

F32 = jnp.float32
BF16 = jnp.bfloat16

EPS = 1e-6
GRID_W = 64
HEADS = 4
HEAD_W = 128
GLA_GATE_RANK = 16
GLA_GATE_NORM = 16.0
GLA_CHUNK = 64
GLA_DIAG = 16
GLA_SAFE_DECAY = 60.0
CONF_KERNEL = 31
SC_KERNEL = 3
MIX_W = 512
MLA_ROPE = 64
MLA_Q_RANK = 384
MLA_KV_RANK = 128
ROPE_FREQS = 16
ROPE_BASE = 10000.0
Q_BLOCK = 2048
Q_SUB = 256
SEQ_BLOCK = 256
HALO = 16
SUBLANES = 8
ZC_W = 5 * MIX_W
MOD_ROWS = 8
ZS_W = 640
VMEM_LIMIT_V7X = 56 * 1024 * 1024


def _dot(a, b):
    return jnp.dot(a, b, preferred_element_type=F32)


def _dot_nt(a, b):
    return lax.dot_general(a, b, (((1,), (1,)), ((), ())), preferred_element_type=F32)


def _dot_tn(a, b):
    return lax.dot_general(a, b, (((0,), (0,)), ((), ())), preferred_element_type=F32)


def _split3(x):
    h1 = x.astype(BF16)
    r1 = x - h1.astype(F32)
    h2 = r1.astype(BF16)
    h3 = (r1 - h2.astype(F32)).astype(BF16)
    return h1, h2, h3


def _sigmoid(x):
    return 1.0 / (1.0 + jnp.exp(-x))


def _silu(x):
    return x * _sigmoid(x)


def _rms(x, g):
    return x * lax.rsqrt(jnp.mean(x * x, axis=-1, keepdims=True) + EPS) * g


def _params(sem):
    return pltpu.CompilerParams(dimension_semantics=sem, vmem_limit_bytes=VMEM_LIMIT_V7X)


def _mod_kernel(c_ref, w_ref, b_ref, o_ref, act_sc):
    tn = w_ref.shape[2]
    rows = []
    nrow = c_ref.shape[0]

    @pl.when((pl.program_id(0) == 0) & (pl.program_id(1) == 0))
    def _():
        for m in range(nrow):
            act_sc[m] = _silu(c_ref[m])

    kc = 8 * SUBLANES
    ncb = tn // 128
    acc = [[jnp.zeros((SUBLANES, 128), F32) for _ in range(ncb)] for _ in range(nrow)]
    for k0 in range(0, w_ref.shape[1], kc):
        wk = w_ref[0, k0:k0 + kc, :]
        for m in range(nrow):
            am = act_sc[m, k0:k0 + kc, :]
            for cb in range(ncb):
                part = wk[:, cb * 128:(cb + 1) * 128] * am
                acc[m][cb] = acc[m][cb] + jnp.sum(part.reshape(kc // SUBLANES, SUBLANES, 128), axis=0)
    for m in range(nrow):
        out_m = jnp.concatenate([jnp.sum(a, axis=0, keepdims=True) for a in acc[m]], axis=1)
        rows.append(out_m + b_ref[0])
    rows.append(jnp.zeros((MOD_ROWS - nrow, tn), F32))
    o_ref[0] = jnp.concatenate(rows, axis=0)


def _modulation(cb, w_mod, b_mod):
    depth, d, n6 = w_mod.shape
    tn = 1024
    return pl.pallas_call(
        _mod_kernel,
        grid=(depth, n6 // tn),
        in_specs=[pl.BlockSpec(cb.shape, lambda i, j: (0, 0, 0)),
                  pl.BlockSpec((1, d, tn), lambda i, j: (i, 0, j)),
                  pl.BlockSpec((1, 1, tn), lambda i, j: (i, 0, j))],
        out_specs=pl.BlockSpec((1, MOD_ROWS, tn), lambda i, j: (i, 0, j)),
        out_shape=jax.ShapeDtypeStruct((depth, MOD_ROWS, n6), F32),
        scratch_shapes=[pltpu.VMEM(cb.shape, F32)],
        compiler_params=_params(("arbitrary", "arbitrary")),
        name="modulation",
    )(cb, w_mod, b_mod.reshape(depth, 1, n6))


def _mod_specs(layer, b, d):
    lat = lambda bi, *_: (layer, bi, 0, 0)
    ctx = lambda *_: (layer, b, 0, 0)
    return (pl.BlockSpec((None, 1, 6, d), lat), pl.BlockSpec((None, 1, 6, d), ctx))


def _token_source(xs, tm, d):
    if not isinstance(xs, tuple):
        return [xs], [pl.BlockSpec((1, tm, d), lambda bi, t: (bi, t, 0))], None
    x, tail = xs
    n_full = x.shape[1] // tm
    specs = [pl.BlockSpec((1, tm, d), lambda bi, t: (bi, jnp.minimum(t, n_full - 1), 0)),
             pl.BlockSpec((1, tm, d), lambda bi, t: (bi, 0, 0), pipeline_mode=pl.Buffered(1))]
    return [x, tail], specs, n_full


def _token_rows(x_ref, xt_ref, t, tail_tile, rows):
    xin = x_ref[0, rows, :]
    if tail_tile is None:
        return xin
    return jnp.where(t == tail_tile, xt_ref[0, rows, :], xin)


def _win_kernel(x_ref, xt_ref, ml_ref, mc_ref, g_ref, wc_ref, wg_ref, ws_ref, qg_ref, kvg_ref, wq_ref, wkv_ref,
                cos_ref, sin_ref, z_ref, zl_ref, q_ref, k_ref, v_ref, *, n_lat, rc, tail_tile):
    t = pl.program_id(1)
    tm = x_ref.shape[1]
    scale = (HEAD_W + MLA_ROPE) ** -0.5 * math.log2(math.e)
    lane = lax.broadcasted_iota(jnp.int32, (1, HEAD_W), 1)
    lower = (lane & 31) < 16
    for r in range(tm // rc):
        rows = slice(r * rc, (r + 1) * rc)
        y = _rms(_token_rows(x_ref, xt_ref, t, tail_tile, rows), g_ref[...])
        rowid = t * tm + r * rc + lax.broadcasted_iota(jnp.int32, (rc, 1), 0)
        is_ctx = rowid >= n_lat
        sh = jnp.where(is_ctx, mc_ref[0, 0:1, :], ml_ref[0, 0:1, :])
        sc = jnp.where(is_ctx, mc_ref[0, 1:2, :], ml_ref[0, 1:2, :])
        h = (y * (1.0 + sc) + sh).astype(BF16)
        zs = _dot_nt(h, ws_ref[...])
        z0 = 0
        for w_ref in (wc_ref, wg_ref):
            nw = w_ref.shape[0]
            for n0 in range(0, nw, nw // 2):
                z_ref[0, rows, z0 + n0:z0 + n0 + nw // 2] = _dot_nt(h, w_ref[n0:n0 + nw // 2, :]).astype(BF16)
            z0 += nw

        krb = zs[:, MLA_Q_RANK + MLA_KV_RANK:ZS_W]
        zl_ref[0, rows, :] = krb
        q = _dot(_rms(zs[:, 0:MLA_Q_RANK], qg_ref[...]).astype(BF16), wq_ref[...])
        kv = _dot(_rms(zs[:, MLA_Q_RANK:MLA_Q_RANK + MLA_KV_RANK], kvg_ref[...]).astype(BF16), wkv_ref[...])
        cs = cos_ref[rows, :]
        sn = sin_ref[rows, :]

        def rope(tt):
            swapped = jnp.where(lower, pltpu.roll(tt, HEAD_W - 16, 1), pltpu.roll(tt, 16, 1))
            return tt * cs + swapped * sn

        krr = rope(krb).astype(BF16)
        for hd in range(HEADS):
            c0 = 2 * HEAD_W * hd
            q_ref[0, rows, c0:c0 + HEAD_W] = (q[:, c0:c0 + HEAD_W] * scale).astype(BF16)
            q_ref[0, rows, c0 + HEAD_W:c0 + 2 * HEAD_W] = (
                rope(q[:, c0 + HEAD_W:c0 + 2 * HEAD_W]) * scale).astype(BF16)
            k_ref[0, rows, c0:c0 + HEAD_W] = kv[:, c0:c0 + HEAD_W].astype(BF16)
            k_ref[0, rows, c0 + HEAD_W:c0 + 2 * HEAD_W] = krr
            v_ref[0, rows, HEAD_W * hd:HEAD_W * (hd + 1)] = kv[:, c0 + HEAD_W:c0 + 2 * HEAD_W].astype(BF16)


def _bind_tail(kernel_fn, n_lead, tail_tile, **kw):
    if tail_tile is not None:
        return functools.partial(kernel_fn, tail_tile=tail_tile, **kw)

    def bound(*refs):
        return kernel_fn(*refs[:n_lead + 1], None, *refs[n_lead + 1:], tail_tile=None, **kw)
    return bound


def _win(xs, nt, mod, g, w_conv, w_gla, w_small, qg, kvg, wq, wkv, cos_t, sin_t, n_lat, layer):
    x0 = xs[0] if isinstance(xs, tuple) else xs
    b, d = x0.shape[0], x0.shape[2]
    tm = nt // 8
    nw = w_conv.shape[1] + w_gla.shape[1]
    kw = 2 * HEAD_W * HEADS
    x_ops, x_specs, tail_tile = _token_source(xs, tm, d)
    kern = _bind_tail(_win_kernel, 0, tail_tile, n_lat=n_lat, rc=tm // 2)
    ml, mc = _mod_specs(layer, b, d)
    resident = lambda a: pl.BlockSpec((None,) + a.shape[1:], lambda bi, t: (layer, 0, 0),
                                      pipeline_mode=pl.Buffered(1))
    vec = lambda a: pl.BlockSpec(a.shape, lambda bi, t: (0, 0))
    tile = lambda wd: pl.BlockSpec((1, tm, wd), lambda bi, t: (bi, t, 0))
    return pl.pallas_call(
        kern,
        grid=(b, nt // tm),
        in_specs=x_specs + [ml, mc, vec(g), resident(w_conv), resident(w_gla), resident(w_small),
                  vec(qg), vec(kvg), resident(wq), resident(wkv),
                  pl.BlockSpec((tm, HEAD_W), lambda bi, t: (t, 0)),
                  pl.BlockSpec((tm, HEAD_W), lambda bi, t: (t, 0))],
        out_specs=[tile(nw), tile(HEAD_W), tile(kw), tile(kw), tile(MIX_W)],
        out_shape=[jax.ShapeDtypeStruct((b, nt, nw), BF16),
                   jax.ShapeDtypeStruct((b, nt, HEAD_W), F32),
                   jax.ShapeDtypeStruct((b, nt, kw), BF16),
                   jax.ShapeDtypeStruct((b, nt, kw), BF16),
                   jax.ShapeDtypeStruct((b, nt, MIX_W), BF16)],
        compiler_params=_params(("arbitrary", "arbitrary")),
        name="norm1_w_in",
    )(*x_ops, mod, mod, g, w_conv, w_gla, w_small, qg, kvg, wq, wkv, cos_t, sin_t)


def _gla_kernel(q_ref, k_ref, v_ref, g_ref, zs_ref, fw_ref, fb_ref, on_ref, o_ref,
                bc_sc, qin_sc, kv_sc, er_sc, oacc_sc, oint_sc, msk_sc, tri_sc, sel_sc, *, n_lat):
    n_tot = q_ref.shape[1]
    blk = SEQ_BLOCK
    cpb = blk // GLA_CHUNK
    nblk = n_tot // blk
    nchunk = n_tot // GLA_CHUNK
    lat_chunks = n_lat // GLA_CHUNK
    ctx_chunks = nchunk - lat_chunks
    ndb = blk // GLA_DIAG
    qscale = HEAD_W ** -0.5
    unroll = max(u for u in range(1, 35) if nchunk % u == 0)
    group_pro, group_intra = 8, 4

    ri = lax.broadcasted_iota(jnp.int32, (blk, blk), 0)
    ci = lax.broadcasted_iota(jnp.int32, (blk, blk), 1)
    same64 = (ri >> 6) == (ci >> 6)
    same16 = (ri >> 4) == (ci >> 4)
    one = lambda m: jnp.where(m, 1.0, 0.0).astype(F32)
    msk_sc[0] = one(same64)
    msk_sc[1] = one((ri >> 5) == (ci >> 5))
    msk_sc[2] = one(same16 & ((ri & 15) >= (ci & 15)))
    msk_sc[3] = one(same16 & ((ri & 15) <= (ci & 15)))
    msk_sc[4] = one(same64 & (ri >= ci))
    msk_sc[5] = one(same64 & (ri <= ci))
    tri_sc[0] = msk_sc[4].astype(BF16)
    tri_sc[1] = msk_sc[5].astype(BF16)
    cj = lax.broadcasted_iota(jnp.int32, (HEAD_W, blk), 1) & (GLA_DIAG - 1)
    for jj in range(GLA_DIAG):
        sel_sc[jj * HEAD_W:(jj + 1) * HEAD_W, :] = one(cj == jj).astype(BF16)

    row = lax.broadcasted_iota(jnp.int32, (blk, 1), 0)
    half0 = (row & 63) < 32
    quart0 = (row & 31) < 16

    def block_rows(i, nb=1):
        start = i * blk if isinstance(i, int) else pl.multiple_of(i * blk, blk)
        return pl.ds(start, nb * blk)

    def paired(fn, group):
        def body(p, carry):
            fn(group * p, group)
            return carry
        lax.fori_loop(0, nblk // group, body, 0)
        if nblk % group:
            fn(nblk - nblk % group, nblk % group)

    fsplit = [_split3(fw_ref[d, 0]) for d in range(2)]
    fcat = jnp.concatenate([jnp.concatenate([fh, fh, fl], axis=0) for fh, fl, _ in fsplit], axis=1)

    def prologue(i0, nb):
        rows_l = [block_rows(i0 + u) for u in range(nb)]
        zcats = []
        for rows in rows_l:
            zh, zl, _ = _split3(zs_ref[0, rows, :])
            zcats.append(jnp.concatenate([zh, zl, zh], axis=1))
        xg2s = [_dot(zc, fcat) for zc in zcats]
        logds = []
        for xg2 in xg2s:
            for d in range(2):
                xg = xg2[:, d * HEAD_W:(d + 1) * HEAD_W] + fb_ref[d, 0]
                logd = (jnp.minimum(xg, 0.0) - jnp.log1p(jnp.exp(-jnp.abs(xg)))) * (1.0 / GLA_GATE_NORM)
                logds.append(jnp.concatenate(_split3(logd)[:2], axis=1))
        c2s = [_dot(tri_sc[n % 2], ld) for n, ld in enumerate(logds)]
        for n, c2 in enumerate(c2s):
            bc_sc[n % 2, rows_l[n // 2], :] = c2[:, 0:HEAD_W] + c2[:, HEAD_W:]

    paired(prologue, group_pro)

    def run_direction(d):
        rev = d == 1

        def load_block(i):
            rows = block_rows(i)
            b = bc_sc[d, rows, :]
            q = q_ref[0, rows, :].astype(F32) * qscale
            k = k_ref[0, rows, :].astype(F32)
            v = v_ref[0, rows, :]
            b4 = b.reshape(cpb, GLA_CHUNK, HEAD_W)
            blast = b4[:, 0:1, :] if rev else b4[:, GLA_CHUNK - 1:GLA_CHUNK, :]
            ebl = jnp.exp(blast)
            kst = (k.reshape(cpb, GLA_CHUNK, HEAD_W) * jnp.exp(blast - b4)).astype(BF16)
            v4 = v.reshape(cpb, GLA_CHUNK, HEAD_W)
            for c in range(cpb):
                kv_sc[i * cpb + c] = _dot_tn(v4[c], kst[c])
                er_sc[i * cpb + c] = jnp.broadcast_to(ebl[c], (SUBLANES, HEAD_W))
            qin_sc[rows, :] = (q * jnp.exp(b)).astype(BF16)
            return rows, b, q, k, v

        def store_block(rows, scores, v):
            o_blk = _dot(scores.astype(BF16), v)
            if rev:
                oacc_sc[rows, :] += o_blk
            else:
                oacc_sc[rows, :] = o_blk

        def intra_plain(i0, nb):
            prep = []
            for u in range(nb):
                i = i0 + u
                rows = block_rows(i)
                b = bc_sc[d, rows, :]
                q = q_ref[0, rows, :].astype(F32) * qscale
                k = k_ref[0, rows, :].astype(F32)
                b4 = b.reshape(cpb, GLA_CHUNK, HEAD_W)
                ebl = jnp.exp(b4[:, 0:1, :] if rev else b4[:, GLA_CHUNK - 1:GLA_CHUNK, :])
                kt = k * jnp.exp(-b)
                kst = (kt.reshape(cpb, GLA_CHUNK, HEAD_W) * ebl).astype(BF16)
                qin = (q * jnp.exp(b)).astype(BF16)
                qin_sc[rows, :] = qin
                prep.append((i, rows, qin, kt.astype(BF16), kst, ebl))
            scores = [_dot_nt(qin, ktb) for _, _, qin, ktb, _, _ in prep]
            for i, rows, _, _, kst, ebl in prep:
                v4 = v_ref[0, rows, :].reshape(cpb, GLA_CHUNK, HEAD_W)
                for c in range(cpb):
                    kv_sc[i * cpb + c] = _dot_tn(v4[c], kst[c])
                    er_sc[i * cpb + c] = jnp.broadcast_to(ebl[c], (SUBLANES, HEAD_W))
            probs = [(s * msk_sc[5 if rev else 4]).astype(BF16) for s in scores]
            for (_, rows, _, _, _, _), p in zip(prep, probs):
                store_block(rows, p, v_ref[0, rows, :])

        def intra_robust(i):
            rows, b, q, k, v = load_block(i)
            b4 = b.reshape(cpb, GLA_CHUNK, HEAD_W)
            r1 = b4[:, 32:33, :] if rev else b4[:, 31:32, :]
            r1 = jnp.broadcast_to(r1, b4.shape).reshape(blk, HEAD_W)
            qsel, ksel = (half0, ~half0) if rev else (~half0, half0)
            qa = q * jnp.where(qsel, jnp.exp(jnp.minimum(b - r1, 0.0)), 0.0)
            ka = k * jnp.where(ksel, jnp.exp(jnp.minimum(r1 - b, 0.0)), 0.0)
            s1 = _dot_nt(qa.astype(BF16), ka.astype(BF16))
            b8 = b.reshape(blk // 32, 32, HEAD_W)
            r2 = b8[:, 16:17, :] if rev else b8[:, 15:16, :]
            r2 = jnp.broadcast_to(r2, b8.shape).reshape(blk, HEAD_W)
            qsel, ksel = (quart0, ~quart0) if rev else (~quart0, quart0)
            qb = q * jnp.where(qsel, jnp.exp(jnp.minimum(b - r2, 0.0)), 0.0)
            kb = k * jnp.where(ksel, jnp.exp(jnp.minimum(r2 - b, 0.0)), 0.0)
            s2 = _dot_nt(qb.astype(BF16), kb.astype(BF16))
            q16 = q.reshape(ndb, GLA_DIAG, HEAD_W)
            k16 = k.reshape(ndb, GLA_DIAG, HEAD_W)
            b16 = b.reshape(ndb, GLA_DIAG, HEAD_W)
            half = GLA_DIAG // 2
            terms = []
            for jj in range(GLA_DIAG):
                bj = b16[:, jj:jj + 1, :]
                kj = k16[:, jj:jj + 1, :]
                need = ((True, jj >= half) if rev else (jj < half, True))
                parts = []
                for hsel, needed in zip((slice(0, half), slice(half, GLA_DIAG)), need):
                    if needed:
                        e = jnp.exp(jnp.minimum(b16[:, hsel, :] - bj, 0.0))
                        parts.append(q16[:, hsel, :] * kj * e)
                    else:
                        parts.append(jnp.zeros((ndb, half, HEAD_W), F32))
                terms.append(jnp.concatenate(parts, axis=1).reshape(blk, HEAD_W).astype(BF16))
            sdiag = _dot(jnp.concatenate(terms, axis=1), sel_sc[...])
            store_block(rows, s1 * msk_sc[0] + s2 * msk_sc[1] + sdiag * msk_sc[3 if rev else 2], v)

        def intra(i0, nb):
            plain_ok = jnp.min(bc_sc[d, block_rows(i0, nb), :]) >= -GLA_SAFE_DECAY

            @pl.when(plain_ok)
            def _():
                intra_plain(i0, nb)

            @pl.when(jnp.logical_not(plain_ok))
            def _():
                for u in range(nb):
                    intra_robust(i0 + u)

        paired(intra, group_intra)

        def inter(m, st):
            for u in range(unroll):
                n = m * unroll + u
                if rev:
                    cid = jnp.where(n < ctx_chunks, lat_chunks + (ctx_chunks - 1 - n),
                                    lat_chunks - 1 - (n - ctx_chunks))
                else:
                    cid = jnp.where(n < ctx_chunks, lat_chunks + n, n - ctx_chunks)
                rows = pl.ds(pl.multiple_of(cid * GLA_CHUNK, GLA_CHUNK), GLA_CHUNK)
                oint_sc[d, rows, :] = _dot_nt(qin_sc[rows, :], st.astype(BF16))
                st = st * er_sc[cid][0:1, :] + kv_sc[cid]
            return st

        lax.fori_loop(0, nchunk // unroll, inter, jnp.zeros((HEAD_W, HEAD_W), F32))

    run_direction(0)
    run_direction(1)

    def finish(i, carry):
        rows = block_rows(i)
        y = _rms(oacc_sc[rows, :] + oint_sc[0, rows, :] + oint_sc[1, rows, :], on_ref[...])
        o_ref[0, rows, :] = (y * _silu(g_ref[0, rows, :].astype(F32))).astype(BF16)
        return carry

    lax.fori_loop(0, nblk, finish, 0)


def _gla(z, zs, fw, fb, on, n_lat, layer):
    b, nt, _ = z.shape
    cb0 = ZC_W // HEAD_W
    seq = lambda off: pl.BlockSpec((1, nt, HEAD_W), lambda bi, h: (bi, 0, cb0 + off * HEADS + h))
    kern = functools.partial(_gla_kernel, n_lat=n_lat)
    return pl.pallas_call(
        kern,
        grid=(b, HEADS),
        in_specs=[seq(0), seq(1), seq(2), seq(3),
                  pl.BlockSpec((1, nt, HEAD_W), lambda bi, h: (bi, 0, 0)),
                  pl.BlockSpec((None, 2, 1, HEAD_W, HEAD_W), lambda bi, h: (layer, 0, h, 0, 0)),
                  pl.BlockSpec((2, 1, 1, HEAD_W), lambda bi, h: (0, h, 0, 0)),
                  pl.BlockSpec((1, HEAD_W), lambda bi, h: (0, 0))],
        out_specs=pl.BlockSpec((1, nt, HEAD_W), lambda bi, h: (bi, 0, h)),
        out_shape=jax.ShapeDtypeStruct((b, nt, MIX_W), BF16),
        scratch_shapes=[pltpu.VMEM((2, nt, HEAD_W), F32),
                        pltpu.VMEM((nt, HEAD_W), BF16),
                        pltpu.VMEM((nt // GLA_CHUNK, HEAD_W, HEAD_W), F32),
                        pltpu.VMEM((nt // GLA_CHUNK, SUBLANES, HEAD_W), F32),
                        pltpu.VMEM((nt, HEAD_W), F32),
                        pltpu.VMEM((2, nt, HEAD_W), F32),
                        pltpu.VMEM((6, SEQ_BLOCK, SEQ_BLOCK), F32),
                        pltpu.VMEM((2, SEQ_BLOCK, SEQ_BLOCK), BF16),
                        pltpu.VMEM((GLA_DIAG * HEAD_W, SEQ_BLOCK), BF16)],
        compiler_params=_params(("arbitrary", "arbitrary")),
        name="gla_mixer",
    )(z, z, z, z, zs, fw, fb, on)


def _conv_kernel(zc_ref, zp_ref, zn_ref, dw_ref, dwb_ref, lng_ref, lnb_ref, scw_ref, o_ref,
                 u_sc, m_sc, *, lat_tiles, n_tiles):
    t = pl.program_id(1)
    tl = zc_ref.shape[1]
    w = MIX_W
    first = (t == 0) | (t == lat_tiles)
    last = (t == lat_tiles - 1) | (t == n_tiles - 1)

    def glu(ref, rows):
        return ref[0, rows, 0:w].astype(F32) * _sigmoid(ref[0, rows, w:2 * w].astype(F32))

    def gated(ref, rows):
        return ref[0, rows, 3 * w:4 * w].astype(F32) * ref[0, rows, 4 * w:5 * w].astype(F32)

    halo = slice(0, HALO)
    u_sc[0, 0:HALO, :] = jnp.where(first, 0.0, glu(zp_ref, halo))
    m_sc[0:HALO, :] = jnp.where(first, 0.0, gated(zp_ref, halo))
    u_sc[0, HALO + tl:, :] = jnp.where(last, 0.0, glu(zn_ref, halo))
    m_sc[HALO + tl:, :] = jnp.where(last, 0.0, gated(zn_ref, halo))
    rc = 32
    for r in range(tl // rc):
        rows = slice(r * rc, (r + 1) * rc)
        u_sc[0, HALO + r * rc:HALO + (r + 1) * rc, :] = glu(zc_ref, rows)
        m_sc[HALO + r * rc:HALO + (r + 1) * rc, :] = gated(zc_ref, rows)
    span = tl + 2 * HALO - SUBLANES
    for s in range(1, SUBLANES):
        u_sc[s, 0:span, :] = u_sc[0, s:s + span, :]

    pad = CONF_KERNEL // 2
    for r in range(tl // rc):
        r0 = r * rc
        acc = jnp.zeros((rc, w), F32)
        for kk in range(CONF_KERNEL):
            off = HALO + r0 + kk - pad
            base = off - off % SUBLANES
            acc = acc + dw_ref[kk:kk + 1, :] * u_sc[off % SUBLANES, base:base + rc, :]
        acc = acc + dwb_ref[...]
        mu = jnp.mean(acc, axis=-1, keepdims=True)
        xc = acc - mu
        y = xc * lax.rsqrt(jnp.mean(xc * xc, axis=-1, keepdims=True) + EPS) * lng_ref[...] + lnb_ref[...]
        o_ref[0, r0:r0 + rc, 0:w] = _silu(y).astype(BF16)
        s = HALO + r0
        cv = (scw_ref[0:1, :] * m_sc[s - 1:s - 1 + rc, :] + scw_ref[1:2, :] * m_sc[s:s + rc, :]
              + scw_ref[2:3, :] * m_sc[s + 1:s + 1 + rc, :])
        o_ref[0, r0:r0 + rc, w:2 * w] = (zc_ref[0, r0:r0 + rc, 2 * w:3 * w].astype(F32) * cv).astype(BF16)


def _conv_mixers(z, dw, dwb, lng, lnb, scw, n_lat):
    b, nt, _ = z.shape
    tl = SEQ_BLOCK
    n_tiles = nt // tl
    hb = tl // HALO
    nhb = nt // HALO
    kern = functools.partial(_conv_kernel, lat_tiles=n_lat // tl, n_tiles=n_tiles)
    vec = lambda n: pl.BlockSpec((n, MIX_W), lambda bi, t: (0, 0))
    return pl.pallas_call(
        kern,
        grid=(b, n_tiles),
        in_specs=[pl.BlockSpec((1, tl, ZC_W), lambda bi, t: (bi, t, 0)),
                  pl.BlockSpec((1, HALO, ZC_W), lambda bi, t: (bi, jnp.maximum(t * hb - 1, 0), 0)),
                  pl.BlockSpec((1, HALO, ZC_W), lambda bi, t: (bi, jnp.minimum((t + 1) * hb, nhb - 1), 0)),
                  vec(CONF_KERNEL), vec(1), vec(1), vec(1), vec(SC_KERNEL)],
        out_specs=pl.BlockSpec((1, tl, 2 * MIX_W), lambda bi, t: (bi, t, 0)),
        out_shape=jax.ShapeDtypeStruct((b, nt, 2 * MIX_W), BF16),
        scratch_shapes=[pltpu.VMEM((SUBLANES, tl + 2 * HALO, MIX_W), F32),
                        pltpu.VMEM((tl + 2 * HALO, MIX_W), F32)],
        compiler_params=_params(("arbitrary", "arbitrary")),
        name="conv_mixers",
    )(z, z, z, dw, dwb, lng, lnb, scw)


def _attn_kernel(q_ref, k_ref, v_ref, o_ref, *, sub):
    for r0 in range(0, q_ref.shape[1], sub):
        rows = slice(r0, r0 + sub)
        s = _dot_nt(q_ref[0, rows, :], k_ref[0])
        p = jnp.exp2(s - jnp.max(s, axis=-1, keepdims=True))
        denom = jnp.sum(p, axis=-1, keepdims=True)
        o_ref[0, rows, :] = (_dot(p.astype(BF16), v_ref[0]) / denom).astype(BF16)


def _attention(q, k, v, n_lat, need_ctx):
    b, nt, _ = q.shape
    kern = functools.partial(_attn_kernel, sub=Q_SUB)
    out = pl.pallas_call(
        kern,
        grid=(b, HEADS, n_lat // Q_BLOCK),
        in_specs=[pl.BlockSpec((1, Q_BLOCK, 2 * HEAD_W), lambda bi, h, qi: (bi, qi, h)),
                  pl.BlockSpec((1, nt, 2 * HEAD_W), lambda bi, h, qi: (bi, 0, h)),
                  pl.BlockSpec((1, nt, HEAD_W), lambda bi, h, qi: (bi, 0, h))],
        out_specs=pl.BlockSpec((1, Q_BLOCK, HEAD_W), lambda bi, h, qi: (bi, qi, h)),
        out_shape=jax.ShapeDtypeStruct((b, n_lat, MIX_W), BF16),
        compiler_params=_params(("arbitrary", "arbitrary", "arbitrary")),
        name="mla_attention",
    )(q, k, v)
    if not need_ctx:
        return out
    n_ctx = nt - n_lat
    cblk = n_lat // n_ctx
    ctx_spec = lambda wd: pl.BlockSpec((1, n_ctx, wd), lambda bi, h: (bi, cblk, h))
    out_ctx = pl.pallas_call(
        kern,
        grid=(b, HEADS),
        in_specs=[ctx_spec(2 * HEAD_W), ctx_spec(2 * HEAD_W), ctx_spec(HEAD_W)],
        out_specs=pl.BlockSpec((1, n_ctx, HEAD_W), lambda bi, h: (bi, 0, h)),
        out_shape=jax.ShapeDtypeStruct((b, n_ctx, MIX_W), BF16),
        compiler_params=_params(("arbitrary", "arbitrary")),
        name="mla_attention_ctx",
    )(q, k, v)
    return jnp.concatenate([out, out_ctx], axis=1)


def _wout_kernel(gla_ref, cs_ref, mla_ref, x_ref, xt_ref, ml_ref, mc_ref, g_ref, w_ref, xo_ref, h_ref,
                 *, n_lat, bounds, tail_tile):
    t = pl.program_id(1)
    tm = x_ref.shape[1]
    for r0, r1 in zip(bounds[:-1], bounds[1:]):
        rows = slice(r0, r1)
        rc = r1 - r0
        mix = jnp.concatenate([gla_ref[0, rows, :], cs_ref[0, rows, :], mla_ref[0, rows, :]], axis=1)
        o = _dot(mix, w_ref[...])
        rowid = t * tm + r0 + lax.broadcasted_iota(jnp.int32, (rc, 1), 0)
        is_ctx = rowid >= n_lat
        g1 = jnp.where(is_ctx, mc_ref[0, 2:3, :], ml_ref[0, 2:3, :])
        sh = jnp.where(is_ctx, mc_ref[0, 3:4, :], ml_ref[0, 3:4, :])
        sc = jnp.where(is_ctx, mc_ref[0, 4:5, :], ml_ref[0, 4:5, :])
        xn = _token_rows(x_ref, xt_ref, t, tail_tile, rows) + g1 * o
        xo_ref[0, rows, :] = xn
        h_ref[0, rows, :] = (_rms(xn, g_ref[...]) * (1.0 + sc) + sh).astype(BF16)


def _wout(gla, cs, mla, xs, mod, g, w, n_lat, n_rows, layer):
    x0 = xs[0] if isinstance(xs, tuple) else xs
    b, d = x0.shape[0], x0.shape[2]
    tm = n_rows // 8
    bounds = tuple(-(-(tm * c // 4) // 16) * 16 for c in range(4)) + (tm,)
    x_ops, x_specs, tail_tile = _token_source(xs, tm, d)
    kern = _bind_tail(_wout_kernel, 3, tail_tile, n_lat=n_lat, bounds=bounds)
    tile = lambda wd: pl.BlockSpec((1, tm, wd), lambda bi, t: (bi, t, 0))
    ml, mc = _mod_specs(layer, b, d)
    return pl.pallas_call(
        kern,
        grid=(b, n_rows // tm),
        in_specs=[tile(MIX_W), tile(2 * MIX_W), tile(MIX_W)] + x_specs + [ml, mc,
                  pl.BlockSpec((1, d), lambda bi, t: (0, 0)),
                  pl.BlockSpec((None,) + w.shape[1:], lambda bi, t: (layer, 0, 0))],
        out_specs=[tile(d), tile(d)],
        out_shape=[jax.ShapeDtypeStruct((b, n_rows, d), F32),
                   jax.ShapeDtypeStruct((b, n_rows, d), BF16)],
        compiler_params=_params(("arbitrary", "arbitrary")),
        name="w_out_norm2",
    )(gla, cs, mla, *x_ops, mod, mod, g, w)


def _ffn_up_kernel(h_ref, w1_ref, w3_ref, u_ref, *, rc):
    tm = h_ref.shape[1]
    w1 = w1_ref[...].astype(BF16)
    w3 = w3_ref[...].astype(BF16)
    for r in range(tm // rc):
        rows = slice(r * rc, (r + 1) * rc)
        hh = h_ref[0, rows, :]
        u_ref[0, rows, :] = (_silu(_dot(hh, w1)) * _dot(hh, w3)).astype(BF16)


def _ffn_up(h2, w1, w3, layer):
    b, n_rows, d = h2.shape
    tm = n_rows // 2
    tf = 512
    dff = w1.shape[2]
    kern = functools.partial(_ffn_up_kernel, rc=tm // 8)
    wspec = pl.BlockSpec((None, d, tf), lambda bi, t, j: (layer, 0, j))
    return pl.pallas_call(
        kern,
        grid=(b, n_rows // tm, dff // tf),
        in_specs=[pl.BlockSpec((1, tm, d), lambda bi, t, j: (bi, t, 0)), wspec, wspec],
        out_specs=pl.BlockSpec((1, tm, tf), lambda bi, t, j: (bi, t, j)),
        out_shape=jax.ShapeDtypeStruct((b, n_rows, dff), BF16),
        compiler_params=_params(("arbitrary", "arbitrary", "arbitrary")),
        name="ffn_up",
    )(h2, w1, w3)


def _ffn_down_kernel(u_ref, x_ref, ml_ref, mc_ref, w2_ref, fg_ref, o_ref, *, n_lat, rc, final):
    t = pl.program_id(1)
    j = pl.program_id(2)
    tm = u_ref.shape[1]
    tn = w2_ref.shape[1]
    cols = pl.ds(pl.multiple_of(j * tn, tn), tn)
    xo_cols = cols if final else slice(None)
    for r in range(tm // rc):
        rows = slice(r * rc, (r + 1) * rc)
        rowid = t * tm + r * rc + lax.broadcasted_iota(jnp.int32, (rc, 1), 0)
        g2 = jnp.where(rowid >= n_lat, mc_ref[0, 5:6, cols], ml_ref[0, 5:6, cols])
        o_ref[0, rows, xo_cols] = x_ref[0, rows, xo_cols] + g2 * _dot(u_ref[0, rows, :], w2_ref[...])

    if final:
        @pl.when(j == pl.num_programs(2) - 1)
        def _():
            for r in range(tm // rc):
                rows = slice(r * rc, (r + 1) * rc)
                o_ref[0, rows, :] = _rms(o_ref[0, rows, :], fg_ref[...])


def _ffn_down(u, xn, mod, w2, fg, n_lat, final, layer):
    b, n_rows, d = xn.shape
    tn = 512
    dff = w2.shape[1]
    if final:
        tm = n_rows // 8
        xo_spec = pl.BlockSpec((1, tm, d), lambda bi, t, j: (bi, t, 0))
    else:
        tm = n_rows // 4
        xo_spec = pl.BlockSpec((1, tm, tn), lambda bi, t, j: (bi, t, j))
    kern = functools.partial(_ffn_down_kernel, n_lat=n_lat, rc=tm // (2 if final else 4), final=final)
    ml, mc = _mod_specs(layer, b, d)
    return pl.pallas_call(
        kern,
        grid=(b, n_rows // tm, d // tn),
        in_specs=[pl.BlockSpec((1, tm, dff), lambda bi, t, j: (bi, t, 0)),
                  xo_spec, ml, mc,
                  pl.BlockSpec((None, dff, tn), lambda bi, t, j: (layer, 0, j)),
                  pl.BlockSpec((1, d), lambda bi, t, j: (0, 0))],
        out_specs=xo_spec,
        out_shape=jax.ShapeDtypeStruct((b, n_rows, d), F32),
        compiler_params=_params(("arbitrary", "arbitrary", "arbitrary")),
        name="ffn_down",
    )(u, xn, mod, mod, w2, fg)


def _rope_tables(n_lat, n_ctx):
    rows = n_lat // GRID_W
    f32 = np.float32
    rowp = np.repeat(np.arange(rows, dtype=f32), GRID_W)
    colp = np.tile(np.arange(GRID_W, dtype=f32), rows)
    inv = (f32(ROPE_BASE) ** (-np.arange(ROPE_FREQS, dtype=f32) * f32(2.0) / f32(2 * ROPE_FREQS))).astype(f32)
    ar, ac = (rowp[:, None] * inv).astype(f32), (colp[:, None] * inv).astype(f32)
    zeros = np.zeros((n_lat, HEAD_W - MLA_ROPE), f32)
    cos_l = np.concatenate([np.cos(ar), np.cos(ar), np.cos(ac), np.cos(ac), zeros], axis=1)
    sin_l = np.concatenate([-np.sin(ar), np.sin(ar), -np.sin(ac), np.sin(ac), zeros], axis=1)
    cos_c = np.concatenate([np.ones((n_ctx, MLA_ROPE), f32), np.zeros((n_ctx, HEAD_W - MLA_ROPE), f32)], axis=1)
    cos_t = np.concatenate([cos_l, cos_c], axis=0).astype(f32)
    sin_t = np.concatenate([sin_l, np.zeros_like(cos_c)], axis=0).astype(f32)
    return jnp.asarray(cos_t), jnp.asarray(sin_t)


def kernel(x, c, ctx, c_ctx, norm1_g, w_mod, b_mod, w_in, gla_fg_up, gla_fg_b, gla_onorm_g, conf_dw, conf_dw_b, conf_ln_g, conf_ln_b, sc_dw, mla_q_norm_g, mla_kv_norm_g, mla_w_uq, mla_w_ukv, w_out, norm2_g, ffn_w1, ffn_w3, ffn_w2, final_norm_g):
    bsz, n_lat, d = x.shape
    n_ctx = ctx.shape[1]
    depth = w_in.shape[0]
    w = MIX_W

    nt = n_lat + n_ctx
    tm_tok = nt // 8
    n_full = n_lat // tm_tok
    if depth > 1 and (n_full + 1) * tm_tok == nt:
        xx = (x, jnp.concatenate([x[:, n_full * tm_tok:], ctx], axis=1))
    else:
        xx = jnp.concatenate([x, ctx], axis=1)
    c3 = jnp.concatenate([c, c_ctx[None, :]], axis=0)
    mod = _modulation(jnp.broadcast_to(c3[:, :, None], (bsz + 1, d, 128)), w_mod, b_mod)
    mod = mod.reshape(depth, MOD_ROWS, 6, d)
    cos_t, sin_t = _rope_tables(n_lat, n_ctx)

    o_gla, o_lr, o_conv, o_mla = 0, 4 * w, 4 * w + 2 * GLA_GATE_RANK, 4 * w + 2 * GLA_GATE_RANK + 5 * w
    w_in_t = jnp.swapaxes(w_in, 1, 2).astype(BF16)
    w_conv = w_in_t[:, o_conv:o_mla]
    w_gla = w_in_t[:, o_gla:o_lr]
    w_small = jnp.concatenate([w_in_t[:, o_mla:], w_in_t[:, o_lr:o_conv],
                               jnp.zeros((depth, ZS_W - (w_in.shape[2] - o_mla) - 2 * GLA_GATE_RANK, d), BF16)],
                              axis=1)
    lr0 = MLA_ROPE
    fw = jnp.stack([jnp.pad(gla_fg_up[:, dd], ((0, 0), (lr0 + dd * GLA_GATE_RANK,
                                                        HEAD_W - lr0 - (dd + 1) * GLA_GATE_RANK), (0, 0)))
                    for dd in range(2)], axis=1)
    fw = fw.reshape(depth, 2, HEAD_W, HEADS, HEAD_W).transpose(0, 1, 3, 2, 4)
    fb = gla_fg_b.reshape(depth, 2, HEADS, 1, HEAD_W)
    wq = mla_w_uq.reshape(depth, MLA_Q_RANK, HEADS, HEAD_W + MLA_ROPE)
    wq = jnp.pad(wq, ((0, 0), (0, 0), (0, 0), (0, HEAD_W - MLA_ROPE))).reshape(depth, MLA_Q_RANK, -1).astype(BF16)
    wkv = mla_w_ukv.astype(BF16)
    wo = w_out.astype(BF16)
    w2 = ffn_w2.astype(BF16)
    row = lambda a: a.reshape(1, -1)

    for i in range(depth):
        last = i == depth - 1
        z, zl, qq, kk, vv = _win(xx, nt, mod, row(norm1_g[i]), w_conv, w_gla, w_small, row(mla_q_norm_g[i]),
                                 row(mla_kv_norm_g[i]), wq, wkv, cos_t, sin_t, n_lat, i)
        gla = _gla(z, zl, fw, fb[i], row(gla_onorm_g[i]), n_lat, i)
        cs = _conv_mixers(z, conf_dw[i], row(conf_dw_b[i]), row(conf_ln_g[i]), row(conf_ln_b[i]), sc_dw[i], n_lat)
        mla = _attention(qq, kk, vv, n_lat, not last)
        n_rows = n_lat if last else n_lat + n_ctx
        xn, h2 = _wout(gla, cs, mla, xx, mod, row(norm2_g[i]), wo, n_lat, n_rows, i)
        u = _ffn_up(h2, ffn_w1, ffn_w3, i)
        xx = _ffn_down(u, xn, mod, w2, row(final_norm_g), n_lat, last, i)
    return xx
```
